```python
import jax, jax.numpy as jnp
from jax import lax
import numpy as np

D_MODEL = 2048
BATCH = 4
SEQ = 4096
DEPTH = 2

D_MIX = D_MODEL
D_FOURIER = D_MIX // 2
D_ATTN = D_MIX - D_FOURIER
HEAD_DIM = 128
N_FOURIER_GROUPS = D_FOURIER // HEAD_DIM
N_HEADS = D_ATTN // HEAD_DIM
DILATED_PATTERNS = ((128, 1), (512, 4), (2048, 16))
ROPE_THETA = 10000.0
D_FF = 5632
N_MOD = 9
EPS = 1e-6
MASK_VALUE = -1e30

kernel_name = "hybrid_fnet_dilated_attn_macaron_encoder"


def rms_norm(x, gain=None):
    xf = x.astype(jnp.float32)
    y = xf * lax.rsqrt(jnp.mean(xf * xf, axis=-1, keepdims=True) + EPS)
    if gain is not None:
        y = y * gain.astype(jnp.float32)
    return y.astype(x.dtype)


def modulate(h, shift, scale):
    return h * (1.0 + scale[:, None, :]) + shift[:, None, :]


def swiglu(h, w_gu, w_down):
    g, u = jnp.split(h @ w_gu, 2, axis=-1)
    return (jax.nn.silu(g) * u) @ w_down


def rope(t, positions):
    half = HEAD_DIM // 2
    inv_freq = ROPE_THETA ** (-jnp.arange(half, dtype=jnp.float32) / half)
    ang = positions.astype(jnp.float32)[:, None] * inv_freq[None, :]
    cos = jnp.cos(ang)[None, :, None, :]
    sin = jnp.sin(ang)[None, :, None, :]
    t1, t2 = t[..., :half], t[..., half:]
    return jnp.concatenate([t1 * cos - t2 * sin, t2 * cos + t1 * sin], axis=-1)


def dilated_branch(q, k, v, window, dilation):
    B, S, H, Dh = q.shape
    L = S // dilation
    reach = (window // 2) // dilation
    blk = reach
    nb = -(-L // blk)
    Lp = nb * blk

    def to_sub(a):
        return a.reshape(B, L, dilation, H, Dh).transpose(0, 2, 3, 1, 4)

    qs = jnp.pad(to_sub(q), ((0, 0), (0, 0), (0, 0), (0, Lp - L), (0, 0)))
    qs = qs.reshape(B, dilation, H, nb, blk, Dh)

    def key_bands(a):
        ap = jnp.pad(to_sub(a), ((0, 0), (0, 0), (0, 0), (blk, Lp - L + blk), (0, 0)))
        ap = ap.reshape(B, dilation, H, nb + 2, blk, Dh)
        return jnp.concatenate([ap[:, :, :, :-2], ap[:, :, :, 1:-1], ap[:, :, :, 2:]], axis=4)

    kb = key_bands(k)
    vb = key_bands(v)
    t_q = jnp.arange(nb)[:, None, None] * blk + jnp.arange(blk)[None, :, None]
    t_k = (jnp.arange(nb)[:, None, None] - 1) * blk + jnp.arange(3 * blk)[None, None, :]
    valid = (jnp.abs(t_q - t_k) <= reach) & (t_k >= 0) & (t_k < L)

    s = jnp.einsum('brhnid,brhnjd->brhnij', qs, kb) * (Dh ** -0.5)
    s = jnp.where(valid, s, MASK_VALUE)
    m = jnp.max(s, axis=-1, keepdims=True)
    p = jnp.exp(s - m)
    l = jnp.sum(p, axis=-1, keepdims=True)
    o = jnp.einsum('brhnij,brhnjd->brhnid', p, vb) / l
    lse = (m + jnp.log(l))[..., 0]
    o = o.reshape(B, dilation, H, Lp, Dh)[:, :, :, :L].transpose(0, 3, 1, 2, 4).reshape(B, S, H, Dh)
    lse = lse.reshape(B, dilation, H, Lp)[:, :, :, :L].transpose(0, 3, 1, 2).reshape(B, S, H)
    return o, lse


def hybrid_mixer(h, w_in, w_fourier, g_fourier_out, g_attn_out, w_out):
    B, S, _ = h.shape
    proj = h @ w_in
    u, q, k, v = jnp.split(proj, [D_FOURIER, D_FOURIER + D_ATTN, D_FOURIER + 2 * D_ATTN], axis=-1)

    u = u.reshape(B, S, N_FOURIER_GROUPS, HEAD_DIM).astype(jnp.float32)
    f = jnp.fft.fft2(u, axes=(1, 3), norm="ortho").real
    y_f = jnp.einsum('bsgc,gce->bsge', f, w_fourier.astype(jnp.float32)).reshape(B, S, D_FOURIER)

    pos = jnp.arange(S)
    q = rope(q.reshape(B, S, N_HEADS, HEAD_DIM).astype(jnp.float32), pos)
    k = rope(k.reshape(B, S, N_HEADS, HEAD_DIM).astype(jnp.float32), pos)
    v = v.reshape(B, S, N_HEADS, HEAD_DIM).astype(jnp.float32)
    branches = [dilated_branch(q, k, v, w, d) for (w, d) in DILATED_PATTERNS]
    outs = jnp.stack([b[0] for b in branches], axis=0)
    lses = jnp.stack([b[1] for b in branches], axis=0)
    wts = jax.nn.softmax(lses, axis=0)
    y_a = jnp.einsum('pbsh,pbshd->bshd', wts, outs).reshape(B, S, D_ATTN)

    y = jnp.concatenate([rms_norm(y_f, g_fourier_out), rms_norm(y_a, g_attn_out)], axis=-1)
    return y.astype(h.dtype) @ w_out


def setup_inputs(seed: int = 0) -> dict:
    key = jax.random.key(seed)
    ks = jax.random.split(key, 16)
    f32 = jnp.float32
    n = lambda k, shape, s: jax.random.normal(k, shape, f32) * s
    return {
        "x": n(ks[0], (BATCH, SEQ, D_MODEL), 1.0),
        "c": n(ks[1], (BATCH, D_MODEL), 1.0),
        "w_ada": n(ks[2], (DEPTH, D_MODEL, N_MOD * D_MODEL), D_MODEL ** -0.5),
        "b_ada": n(ks[3], (DEPTH, N_MOD * D_MODEL), 0.01),
        "w_ffn1_gu": n(ks[4], (DEPTH, D_MODEL, 2 * D_FF), D_MODEL ** -0.5),
        "w_ffn1_down": n(ks[5], (DEPTH, D_FF, D_MODEL), D_FF ** -0.5),
        "w_mix_in": n(ks[6], (DEPTH, D_MODEL, D_FOURIER + 3 * D_ATTN), D_MODEL ** -0.5),
        "w_fourier": n(ks[7], (DEPTH, N_FOURIER_GROUPS, HEAD_DIM, HEAD_DIM), HEAD_DIM ** -0.5),
        "g_fourier_out": 1.0 + n(ks[8], (DEPTH, D_FOURIER), 0.02),
        "g_attn_out": 1.0 + n(ks[9], (DEPTH, D_ATTN), 0.02),
        "w_mix_out": n(ks[10], (DEPTH, D_MIX, D_MODEL), D_MIX ** -0.5),
        "w_ffn2_gu": n(ks[11], (DEPTH, D_MODEL, 2 * D_FF), D_MODEL ** -0.5),
        "w_ffn2_down": n(ks[12], (DEPTH, D_FF, D_MODEL), D_FF ** -0.5),
        "g_final": 1.0 + n(ks[13], (D_MODEL,), 0.02),
    }


def reference(x, c, w_ada, b_ada, w_ffn1_gu, w_ffn1_down, w_mix_in, w_fourier,
              g_fourier_out, g_attn_out, w_mix_out, w_ffn2_gu, w_ffn2_down, g_final):
    c_act = jax.nn.silu(c)
    for l in range(DEPTH):
        mod = c_act @ w_ada[l] + b_ada[l]
        sh1, sc1, g1, sh2, sc2, g2, sh3, sc3, g3 = jnp.split(mod, N_MOD, axis=-1)
        h = modulate(rms_norm(x), sh1, sc1)
        x = x + 0.5 * g1[:, None, :] * swiglu(h, w_ffn1_gu[l], w_ffn1_down[l])
        h = modulate(rms_norm(x), sh2, sc2)
        x = x + g2[:, None, :] * hybrid_mixer(h, w_mix_in[l], w_fourier[l], g_fourier_out[l],
                                               g_attn_out[l], w_mix_out[l])
        h = modulate(rms_norm(x), sh3, sc3)
        x = x + 0.5 * g3[:, None, :] * swiglu(h, w_ffn2_gu[l], w_ffn2_down[l])
    return rms_norm(x, g_final)
```

```python
import functools
import math

import jax
import jax.numpy as jnp
from jax import lax
from jax.experimental import pallas as pl
from jax.experimental.pallas import tpu as pltpu

F32 = jnp.float32
BF16 = jnp.bfloat16

EPS = 1e-6
HEAD_DIM = 128
ROPE_THETA = 10000.0
DILATED_PATTERNS = ((128, 1), (512, 4), (2048, 16))
DILATIONS = tuple(d for _, d in DILATED_PATTERNS)
REACH = (DILATED_PATTERNS[0][0] // 2) // DILATED_PATTERNS[0][1]
assert all((w // 2) // d == REACH for w, d in DILATED_PATTERNS)
MASK_VALUE = -1e30
N_MOD = 9

LANES = 128
VMEM_LIMIT_BYTES = 56 * 1024 * 1024

ATTN_BQ = 128
ATTN_KW = ATTN_BQ + 2 * REACH


def _compiler_params(semantics):
    return pltpu.CompilerParams(dimension_semantics=semantics,
                                vmem_limit_bytes=VMEM_LIMIT_BYTES)


def _rms(x):
    return x * lax.rsqrt(jnp.mean(x * x, axis=-1, keepdims=True) + EPS)


def _silu(x):
    return x * jax.nn.sigmoid(x)


def _ada_kernel(c_ref, w_ref, b_ref, o_ref):
    ca = _silu(c_ref[...]).astype(BF16)
    o_ref[0] = jnp.dot(ca, w_ref[0].astype(BF16),
                       preferred_element_type=F32) + b_ref[0]


def _ada(c_pad, w_ada, b_ada, *, tn):
    depth, d, n = w_ada.shape
    rows = c_pad.shape[0]
    return pl.pallas_call(
        _ada_kernel,
        out_shape=jax.ShapeDtypeStruct((depth, rows, n), F32),
        grid=(depth, n // tn),
        in_specs=[
            pl.BlockSpec((rows, d), lambda l, j: (0, 0)),
            pl.BlockSpec((1, d, tn), lambda l, j: (l, 0, j)),
            pl.BlockSpec((1, 1, tn), lambda l, j: (l, 0, j)),
        ],
        out_specs=pl.BlockSpec((1, rows, tn), lambda l, j: (l, 0, j)),
        compiler_params=_compiler_params(("parallel", "parallel")),
        name="ada",
    )(c_pad, w_ada, b_ada.reshape(depth, 1, n))


def _ffn_kernel(x_ref, sh_ref, sc_ref, gt_ref, wg_ref, wu_ref, wd_ref, gf_ref,
                o_ref, h_ref, *, final_norm):
    j = pl.program_id(1)

    @pl.when(j == 0)
    def _():
        h = _rms(x_ref[...]) * (1.0 + sc_ref[0]) + sh_ref[0]
        h_ref[...] = h.astype(BF16)
        o_ref[...] = jnp.zeros_like(o_ref)

    h = h_ref[...]
    g = jnp.dot(h, wg_ref[...], preferred_element_type=F32)
    u = jnp.dot(h, wu_ref[...], preferred_element_type=F32)
    a = (_silu(g) * u).astype(BF16)
    o_ref[...] += jnp.dot(a, wd_ref[...], preferred_element_type=F32)

    @pl.when(j == pl.num_programs(1) - 1)
    def _():
        y = x_ref[...] + (0.5 * gt_ref[0]) * o_ref[...]
        if final_norm:
            y = _rms(y) * gf_ref[...]
        o_ref[...] = y


def _ffn(x, sh, sc, gt, w_gu, w_down, g_final, *, seq, tm, tf, final_norm):
    t, d = x.shape
    f = w_down.shape[0]
    nj = f // tf
    tpb = seq // tm
    vec = pl.BlockSpec((1, 1, d), lambda i, j: (i // tpb, 0, 0))
    return pl.pallas_call(
        functools.partial(_ffn_kernel, final_norm=final_norm),
        out_shape=jax.ShapeDtypeStruct((t, d), F32),
        grid=(t // tm, nj),
        in_specs=[
            pl.BlockSpec((tm, d), lambda i, j: (i, 0)),
            vec, vec, vec,
            pl.BlockSpec((d, tf), lambda i, j: (0, j)),
            pl.BlockSpec((d, tf), lambda i, j: (0, j + nj)),
            pl.BlockSpec((tf, d), lambda i, j: (j, 0)),
            pl.BlockSpec((1, d), lambda i, j: (0, 0)),
        ],
        out_specs=pl.BlockSpec((tm, d), lambda i, j: (i, 0)),
        scratch_shapes=[pltpu.VMEM((tm, d), BF16)],
        compiler_params=_compiler_params(("parallel", "arbitrary")),
        name="ffn",
    )(x, sh, sc, gt, w_gu, w_gu, w_down, g_final)


def _fprep_kernel(cc_ref, sc_ref, w_ref, ab_ref, *, scale):
    w = w_ref[0]
    hc = w.shape[0]
    a = jnp.dot(cc_ref[...], w, preferred_element_type=F32,
                precision=lax.Precision.HIGHEST)
    b = jnp.dot(sc_ref[...], w, preferred_element_type=F32,
                precision=lax.Precision.HIGHEST)
    ab_ref[0, :, :hc] = (a * scale).astype(BF16)
    ab_ref[0, :, hc:] = (b * scale).astype(BF16)


def _fprep(w_fourier, seq):
    n, hc, _ = w_fourier.shape
    ang = _dft_angles(hc, jnp.arange(hc), jnp.arange(hc))
    scale = 1.0 / math.sqrt(seq * hc)
    sq = pl.BlockSpec((hc, hc), lambda g: (0, 0))
    return pl.pallas_call(
        functools.partial(_fprep_kernel, scale=scale),
        out_shape=jax.ShapeDtypeStruct((n, hc, 2 * hc), BF16),
        grid=(n,),
        in_specs=[sq, sq, pl.BlockSpec((1, hc, hc), lambda g: (g, 0, 0))],
        out_specs=pl.BlockSpec((1, hc, 2 * hc), lambda g: (g, 0, 0)),
        compiler_params=_compiler_params(("parallel",)),
        name="fprep",
    )(jnp.cos(ang), jnp.sin(ang), w_fourier)


def _dft_angles(n, rows, cols):
    idx = (rows[:, None] * cols[None, :]) % n
    return idx.astype(F32) * (2.0 * math.pi / n)


def _mixin_kernel(x_ref, sh_ref, sc_ref, w_ref, ab_ref, cos_ref, sin_ref,
                  pab_ref, q_ref, k_ref, v_ref, h_ref, *, q_scale):
    n = pl.program_id(1)
    hc = HEAD_DIM
    groups = pab_ref.shape[1] // (2 * hc)
    heads = q_ref.shape[1]

    @pl.when(n == 0)
    def _():
        h = _rms(x_ref[...]) * (1.0 + sc_ref[0]) + sh_ref[0]
        h_ref[...] = h.astype(BF16)

    p = jnp.dot(h_ref[...], w_ref[...], preferred_element_type=F32)

    @pl.when(n == 0)
    def _():
        pb = p.astype(BF16)
        for g in range(groups):
            r = jnp.dot(pb[:, g * hc:(g + 1) * hc], ab_ref[g],
                        preferred_element_type=F32)
            pab_ref[:, g * hc:(g + 1) * hc] = r[:, :hc].astype(BF16)
            pab_ref[:, (groups + g) * hc:(groups + g + 1) * hc] = (
                r[:, hc:].astype(BF16))

    def rope_to(dst_ref, scale):
        cos = cos_ref[...] * scale
        sin = sin_ref[...] * scale
        for hh in range(heads):
            t = p[:, hh * hc:(hh + 1) * hc]
            dst_ref[0, hh] = t * cos + pltpu.roll(t, hc // 2, 1) * sin

    @pl.when(n == 1)
    def _():
        rope_to(q_ref, q_scale)

    @pl.when(n == 2)
    def _():
        rope_to(k_ref, 1.0)

    @pl.when(n == 3)
    def _():
        for hh in range(heads):
            v_ref[0, hh] = p[:, hh * hc:(hh + 1) * hc]


def _mixin(x, sh, sc, w_in, ab, cos, sin, *, batch, seq, tm):
    t, d = x.shape
    dm = w_in.shape[1] // 4
    heads = dm // HEAD_DIM
    tpb = seq // tm
    vec = pl.BlockSpec((1, 1, d), lambda i, n: (i // tpb, 0, 0))
    tab = pl.BlockSpec((tm, HEAD_DIM), lambda i, n: (i % tpb, 0))
    hm = pl.BlockSpec((1, heads, tm, HEAD_DIM),
                      lambda i, n: (i // tpb, 0, i % tpb, 0))
    hm_shape = jax.ShapeDtypeStruct((batch, heads, seq, HEAD_DIM), F32)
    return pl.pallas_call(
        functools.partial(_mixin_kernel, q_scale=HEAD_DIM ** -0.5),
        out_shape=(jax.ShapeDtypeStruct((t, 2 * dm), BF16),
                   hm_shape, hm_shape, hm_shape),
        grid=(t // tm, 4),
        in_specs=[
            pl.BlockSpec((tm, d), lambda i, n: (i, 0)),
            vec, vec,
            pl.BlockSpec((d, dm), lambda i, n: (0, n)),
            pl.BlockSpec(ab.shape, lambda i, n: (0, 0, 0)),
            tab, tab,
        ],
        out_specs=(pl.BlockSpec((tm, 2 * dm), lambda i, n: (i, 0)),
                   hm, hm, hm),
        scratch_shapes=[pltpu.VMEM((tm, d), BF16)],
        compiler_params=_compiler_params(("parallel", "arbitrary")),
        name="mix_in",
    )(x, sh, sc, w_in, ab, cos, sin)


def _rope_tables(seq):
    half = HEAD_DIM // 2
    inv_freq = ROPE_THETA ** (-jnp.arange(half, dtype=F32) / half)
    ang = jnp.arange(seq, dtype=F32)[:, None] * inv_freq[None, :]
    cos, sin = jnp.cos(ang), jnp.sin(ang)
    return (jnp.concatenate([cos, cos], axis=-1),
            jnp.concatenate([-sin, sin], axis=-1))


def _dft_kernel(ca_ref, sa_ref, cb_ref, sb_ref, pa_ref, pb_ref, o_ref,
                cmat, smat):
    first = jnp.logical_and(pl.program_id(1) == 0, pl.program_id(2) == 0)

    @pl.when(first)
    def _():
        cb = cb_ref[...]
        sb = sb_ref[...]
        for a in range(ca_ref.shape[1]):
            ca = ca_ref[:, a:a + 1]
            sa = sa_ref[:, a:a + 1]
            cols = slice(a * LANES, (a + 1) * LANES)
            cmat[:, cols] = (ca * cb - sa * sb).astype(BF16)
            smat[:, cols] = (sa * cb + ca * sb).astype(BF16)

    o_ref[0] = (jnp.dot(cmat[...], pa_ref[0], preferred_element_type=F32)
                - jnp.dot(smat[...], pb_ref[0], preferred_element_type=F32))


def _dft(pab, tables, *, batch, seq, tm, tn):
    dm = pab.shape[-1] // 2
    ncb = dm // tn
    pab3 = pab.reshape(batch, seq, 2 * dm)
    ca, sa, cb, sb = tables
    coarse = pl.BlockSpec((tm, ca.shape[1]), lambda i, b, c: (i, 0))
    fine = pl.BlockSpec((tm, LANES), lambda i, b, c: (i, 0))
    return pl.pallas_call(
        _dft_kernel,
        out_shape=jax.ShapeDtypeStruct((batch, seq, dm), F32),
        grid=(seq // tm, batch, ncb),
        in_specs=[
            coarse, coarse, fine, fine,
            pl.BlockSpec((1, seq, tn), lambda i, b, c: (b, 0, c)),
            pl.BlockSpec((1, seq, tn), lambda i, b, c: (b, 0, c + ncb)),
        ],
        out_specs=pl.BlockSpec((1, tm, tn), lambda i, b, c: (b, i, c)),
        scratch_shapes=[pltpu.VMEM((tm, seq), BF16), pltpu.VMEM((tm, seq), BF16)],
        compiler_params=_compiler_params(("parallel", "arbitrary", "arbitrary")),
        name="dft",
    )(ca, sa, cb, sb, pab3, pab3)


def _dft_tables(seq):
    k = jnp.arange(seq)
    coarse = _dft_angles(seq, k, LANES * jnp.arange(seq // LANES))
    fine = _dft_angles(seq, k, jnp.arange(LANES))
    return jnp.cos(coarse), jnp.sin(coarse), jnp.cos(fine), jnp.sin(fine)


def _attn_kernel(q_ref, k_ref, v_ref, o_ref, qp, kp, vp, acc, m_s, l_s):
    seq = q_ref.shape[2]
    bq, kw = ATTN_BQ, ATTN_KW
    row = lax.broadcasted_iota(jnp.int32, (bq, kw), 0)
    col = lax.broadcasted_iota(jnp.int32, (bq, kw), 1)
    row_minus_col = row - col

    for bi, d in enumerate(DILATIONS):
        sub = seq // d
        bps = sub // bq
        first, last = bi == 0, bi == len(DILATIONS) - 1

        for j in range(d):
            src = pl.ds(j, sub, stride=d) if d > 1 else slice(None)
            dst = slice(j * sub, (j + 1) * sub)
            qp[dst, :] = q_ref[0, 0, src, :].astype(BF16)
            kp[dst, :] = k_ref[0, 0, src, :].astype(BF16)
            vp[dst, :] = v_ref[0, 0, src, :].astype(BF16)

        def block(n, carry, d=d, sub=sub, bps=bps, first=first, last=last):
            j = n // bps
            t0 = (n % bps) * bq
            base = j * sub
            ks = jnp.clip(t0 - REACH, 0, sub - kw)
            qb = qp[pl.ds(pl.multiple_of(base + t0, bq), bq), :]
            kstart = pl.multiple_of(base + ks, REACH)
            kb = kp[pl.ds(kstart, kw), :]
            vb = vp[pl.ds(kstart, kw), :]
            s = lax.dot_general(qb, kb, (((1,), (1,)), ((), ())),
                                preferred_element_type=F32)
            rel = row_minus_col + (t0 - ks)
            s = jnp.where(jnp.abs(rel) <= REACH, s, MASK_VALUE)
            m = jnp.max(s, axis=-1, keepdims=True)
            p = jnp.exp(s - m)
            l = jnp.sum(p, axis=-1, keepdims=True)
            pv = jnp.dot(p.astype(BF16), vb, preferred_element_type=F32)

            if d > 1:
                rows = pl.ds(j + d * t0, bq, stride=d)
            else:
                rows = pl.ds(pl.multiple_of(t0, bq), bq)
            if not first:
                m_old = m_s[rows, :]
                m_new = jnp.maximum(m_old, m)
                w_old = jnp.exp(m_old - m_new)
                w_cur = jnp.exp(m - m_new)
                pv = acc[rows, :] * w_old + pv * w_cur
                l = l_s[rows, :] * w_old + l * w_cur
                m = m_new
            if last:
                o_ref[0, rows, :] = pv / l
            else:
                acc[rows, :] = pv
                m_s[rows, :] = m
                l_s[rows, :] = l
            return carry

        lax.fori_loop(0, seq // bq, block, 0)


def _attn(q, k, v):
    batch, heads, seq, hd = q.shape
    blk = pl.BlockSpec((1, 1, seq, hd), lambda b, h: (b, h, 0, 0))
    return pl.pallas_call(
        _attn_kernel,
        out_shape=jax.ShapeDtypeStruct((batch, seq, heads * hd), F32),
        grid=(batch, heads),
        in_specs=[blk, blk, blk],
        out_specs=pl.BlockSpec((1, seq, hd), lambda b, h: (b, 0, h)),
        scratch_shapes=[pltpu.VMEM((seq, hd), BF16)] * 3 + [
            pltpu.VMEM((seq, hd), F32),
            pltpu.VMEM((seq, 1), F32),
            pltpu.VMEM((seq, 1), F32),
        ],
        compiler_params=_compiler_params(("parallel", "parallel")),
        name="attn",
    )(q, k, v)


def _mixout_kernel(x_ref, yf_ref, ya_ref, gf_ref, ga_ref, gt_ref, w_ref, o_ref):
    df = yf_ref.shape[1]
    nf = (_rms(yf_ref[...]) * gf_ref[...]).astype(BF16)
    na = (_rms(ya_ref[...]) * ga_ref[...]).astype(BF16)
    out = (jnp.dot(nf, w_ref[:df, :], preferred_element_type=F32)
           + jnp.dot(na, w_ref[df:, :], preferred_element_type=F32))
    o_ref[...] = x_ref[...] + gt_ref[0] * out


def _mixout(x, yf, ya, gf, ga, gt, w_out, *, seq, tm):
    t, d = x.shape
    df, da = yf.shape[1], ya.shape[1]
    tpb = seq // tm
    return pl.pallas_call(
        _mixout_kernel,
        out_shape=jax.ShapeDtypeStruct((t, d), F32),
        grid=(t // tm,),
        in_specs=[
            pl.BlockSpec((tm, d), lambda i: (i, 0)),
            pl.BlockSpec((tm, df), lambda i: (i, 0)),
            pl.BlockSpec((tm, da), lambda i: (i, 0)),
            pl.BlockSpec((1, df), lambda i: (0, 0)),
            pl.BlockSpec((1, da), lambda i: (0, 0)),
            pl.BlockSpec((1, 1, d), lambda i: (i // tpb, 0, 0)),
            pl.BlockSpec((df + da, d), lambda i: (0, 0)),
        ],
        out_specs=pl.BlockSpec((tm, d), lambda i: (i, 0)),
        compiler_params=_compiler_params(("parallel",)),
        name="mix_out",
    )(x, yf, ya, gf, ga, gt, w_out)


def _tiles(seq, d_ff):
    tm = min(512, seq)
    tf = 512 if d_ff % 512 == 0 else LANES
    return tm, tf


def kernel(x, c, w_ada, b_ada, w_ffn1_gu, w_ffn1_down, w_mix_in, w_fourier,
           g_fourier_out, g_attn_out, w_mix_out, w_ffn2_gu, w_ffn2_down, g_final):
    batch, seq, d = x.shape
    depth = w_ada.shape[0]
    d_ff = w_ffn1_down.shape[1]
    t = batch * seq
    tm, tf = _tiles(seq, d_ff)

    rows = 8 * pl.cdiv(batch, 8)
    c_pad = jnp.zeros((rows, d), F32).at[:batch].set(c)
    mod = _ada(c_pad, w_ada, b_ada, tn=min(1024, d))
    mod = mod[:, :batch].reshape(depth, batch, N_MOD, 1, d)

    groups = w_fourier.shape[1]
    ab = _fprep(w_fourier.reshape(depth * groups, HEAD_DIM, HEAD_DIM), seq)
    ab = ab.reshape(depth, groups, HEAD_DIM, 2 * HEAD_DIM)
    rope_cos, rope_sin = _rope_tables(seq)
    dft_tables = _dft_tables(seq)
    g_fin = g_final.reshape(1, d)

    xt = x.reshape(t, d)
    for l in range(depth):
        sh1, sc1, g1, sh2, sc2, g2, sh3, sc3, g3 = (mod[l, :, i] for i in range(N_MOD))
        xt = _ffn(xt, sh1, sc1, g1, w_ffn1_gu[l].astype(BF16),
                  w_ffn1_down[l].astype(BF16), g_fin,
                  seq=seq, tm=tm, tf=tf, final_norm=False)
        pab, q, k, v = _mixin(xt, sh2, sc2, w_mix_in[l].astype(BF16), ab[l],
                              rope_cos, rope_sin, batch=batch, seq=seq, tm=tm)
        yf = _dft(pab, dft_tables, batch=batch, seq=seq, tm=tm, tn=min(512, pab.shape[1] // 2))
        ya = _attn(q, k, v)
        xt = _mixout(xt, yf.reshape(t, -1), ya.reshape(t, -1),
                     g_fourier_out[l].reshape(1, -1), g_attn_out[l].reshape(1, -1),
                     g2, w_mix_out[l].astype(BF16), seq=seq, tm=tm)
        xt = _ffn(xt, sh3, sc3, g3, w_ffn2_gu[l].astype(BF16),
                  w_ffn2_down[l].astype(BF16), g_fin,
                  seq=seq, tm=tm, tf=tf, final_norm=(l == depth - 1))
    return xt.reshape(batch, seq, d)
```

```python
import functools
import math

import jax
import jax.numpy as jnp
from jax import lax
from jax.experimental import pallas as pl
from jax.experimental.pallas import tpu as pltpu

F32 = jnp.float32
BF16 = jnp.bfloat16

EPS = 1e-6
HEAD_DIM = 128
ROPE_THETA = 10000.0
DILATED_PATTERNS = ((128, 1), (512, 4), (2048, 16))
DILATIONS = tuple(d for _, d in DILATED_PATTERNS)
REACH = (DILATED_PATTERNS[0][0] // 2) // DILATED_PATTERNS[0][1]
assert all((w // 2) // d == REACH for w, d in DILATED_PATTERNS)
MASK_VALUE = -1e30
N_MOD = 9

LANES = 128
VMEM_LIMIT_BYTES = 56 * 1024 * 1024

ATTN_BQ = 128
ATTN_KW = ATTN_BQ + 2 * REACH
ATTN_COPY_ROWS = 128
ATTN_UNROLL = 16


def _compiler_params(semantics):
    return pltpu.CompilerParams(dimension_semantics=semantics,
                                vmem_limit_bytes=VMEM_LIMIT_BYTES)


def _rms(x):
    return x * lax.rsqrt(jnp.mean(x * x, axis=-1, keepdims=True) + EPS)


def _silu(x):
    return x * jax.nn.sigmoid(x)


def _ada_kernel(c_ref, w_ref, b_ref, o_ref):
    ca = _silu(c_ref[...]).astype(BF16)
    o_ref[0] = jnp.dot(ca, w_ref[0].astype(BF16),
                       preferred_element_type=F32) + b_ref[0]


def _ada(c_pad, w_ada, b_ada, *, tn):
    depth, d, n = w_ada.shape
    rows = c_pad.shape[0]
    return pl.pallas_call(
        _ada_kernel,
        out_shape=jax.ShapeDtypeStruct((depth, rows, n), F32),
        grid=(depth, n // tn),
        in_specs=[
            pl.BlockSpec((rows, d), lambda l, j: (0, 0)),
            pl.BlockSpec((1, d, tn), lambda l, j: (l, 0, j)),
            pl.BlockSpec((1, 1, tn), lambda l, j: (l, 0, j)),
        ],
        out_specs=pl.BlockSpec((1, rows, tn), lambda l, j: (l, 0, j)),
        compiler_params=_compiler_params(("parallel", "parallel")),
        name="ada",
    )(c_pad, w_ada, b_ada.reshape(depth, 1, n))


def _ffn_kernel(x_ref, sh_ref, sc_ref, gt_ref, wg_ref, wu_ref, wd_ref, gf_ref,
                o_ref, h_ref, *, final_norm):
    j = pl.program_id(1)

    @pl.when(j == 0)
    def _():
        h = _rms(x_ref[...]) * (1.0 + sc_ref[0]) + sh_ref[0]
        h_ref[...] = h.astype(BF16)
        o_ref[...] = jnp.zeros_like(o_ref)

    h = h_ref[...]
    g = jnp.dot(h, wg_ref[...], preferred_element_type=F32)
    u = jnp.dot(h, wu_ref[...], preferred_element_type=F32)
    a = (_silu(g) * u).astype(BF16)
    o_ref[...] += jnp.dot(a, wd_ref[...], preferred_element_type=F32)

    @pl.when(j == pl.num_programs(1) - 1)
    def _():
        y = x_ref[...] + (0.5 * gt_ref[0]) * o_ref[...]
        if final_norm:
            y = _rms(y) * gf_ref[...]
        o_ref[...] = y


def _ffn(x, sh, sc, gt, w_gu, w_down, g_final, *, seq, tm, tf, final_norm):
    t, d = x.shape
    f = w_down.shape[0]
    nj = f // tf
    tpb = seq // tm
    vec = pl.BlockSpec((1, 1, d), lambda i, j: (i // tpb, 0, 0))
    return pl.pallas_call(
        functools.partial(_ffn_kernel, final_norm=final_norm),
        out_shape=jax.ShapeDtypeStruct((t, d), F32),
        grid=(t // tm, nj),
        in_specs=[
            pl.BlockSpec((tm, d), lambda i, j: (i, 0)),
            vec, vec, vec,
            pl.BlockSpec((d, tf), lambda i, j: (0, j)),
            pl.BlockSpec((d, tf), lambda i, j: (0, j + nj)),
            pl.BlockSpec((tf, d), lambda i, j: (j, 0)),
            pl.BlockSpec((1, d), lambda i, j: (0, 0)),
        ],
        out_specs=pl.BlockSpec((tm, d), lambda i, j: (i, 0)),
        scratch_shapes=[pltpu.VMEM((tm, d), BF16)],
        compiler_params=_compiler_params(("parallel", "arbitrary")),
        name="ffn",
    )(x, sh, sc, gt, w_gu, w_gu, w_down, g_final)


def _fprep_kernel(cc_ref, sc_ref, w_ref, ab_ref, *, scale):
    w = w_ref[0]
    hc = w.shape[0]
    a = jnp.dot(cc_ref[...], w, preferred_element_type=F32,
                precision=lax.Precision.HIGHEST)
    b = jnp.dot(sc_ref[...], w, preferred_element_type=F32,
                precision=lax.Precision.HIGHEST)
    ab_ref[0, :, :hc] = (a * scale).astype(BF16)
    ab_ref[0, :, hc:] = (b * scale).astype(BF16)


def _fprep(w_fourier, seq):
    n, hc, _ = w_fourier.shape
    ang = _dft_angles(hc, jnp.arange(hc), jnp.arange(hc))
    scale = 1.0 / math.sqrt(seq * hc)
    sq = pl.BlockSpec((hc, hc), lambda g: (0, 0))
    return pl.pallas_call(
        functools.partial(_fprep_kernel, scale=scale),
        out_shape=jax.ShapeDtypeStruct((n, hc, 2 * hc), BF16),
        grid=(n,),
        in_specs=[sq, sq, pl.BlockSpec((1, hc, hc), lambda g: (g, 0, 0))],
        out_specs=pl.BlockSpec((1, hc, 2 * hc), lambda g: (g, 0, 0)),
        compiler_params=_compiler_params(("parallel",)),
        name="fprep",
    )(jnp.cos(ang), jnp.sin(ang), w_fourier)


def _dft_angles(n, rows, cols):
    idx = (rows[:, None] * cols[None, :]) % n
    return idx.astype(F32) * (2.0 * math.pi / n)


def _mixin_kernel(x_ref, sh_ref, sc_ref, w_ref, ab_ref, cos_ref, sin_ref,
                  pab_ref, q_ref, k_ref, v_ref, h_ref, *, q_scale):
    n = pl.program_id(1)
    hc = HEAD_DIM
    groups = pab_ref.shape[1] // (2 * hc)
    heads = q_ref.shape[1]

    @pl.when(n == 0)
    def _():
        h = _rms(x_ref[...]) * (1.0 + sc_ref[0]) + sh_ref[0]
        h_ref[...] = h.astype(BF16)

    p = jnp.dot(h_ref[...], w_ref[...], preferred_element_type=F32)

    @pl.when(n == 0)
    def _():
        pb = p.astype(BF16)
        for g in range(groups):
            r = jnp.dot(pb[:, g * hc:(g + 1) * hc], ab_ref[g],
                        preferred_element_type=F32)
            pab_ref[:, g * hc:(g + 1) * hc] = r[:, :hc].astype(BF16)
            pab_ref[:, (groups + g) * hc:(groups + g + 1) * hc] = (
                r[:, hc:].astype(BF16))

    def rope_to(dst_ref, scale):
        cos = cos_ref[...] * scale
        sin = sin_ref[...] * scale
        for hh in range(heads):
            t = p[:, hh * hc:(hh + 1) * hc]
            dst_ref[0, hh] = t * cos + pltpu.roll(t, hc // 2, 1) * sin

    @pl.when(n == 1)
    def _():
        rope_to(q_ref, q_scale)

    @pl.when(n == 2)
    def _():
        rope_to(k_ref, 1.0)

    @pl.when(n == 3)
    def _():
        for hh in range(heads):
            v_ref[0, hh] = p[:, hh * hc:(hh + 1) * hc]


def _mixin(x, sh, sc, w_in, ab, cos, sin, *, batch, seq, tm):
    t, d = x.shape
    dm = w_in.shape[1] // 4
    heads = dm // HEAD_DIM
    tpb = seq // tm
    vec = pl.BlockSpec((1, 1, d), lambda i, n: (i // tpb, 0, 0))
    tab = pl.BlockSpec((tm, HEAD_DIM), lambda i, n: (i % tpb, 0))
    hm = pl.BlockSpec((1, heads, tm, HEAD_DIM),
                      lambda i, n: (i // tpb, 0, i % tpb, 0))
    hm_shape = jax.ShapeDtypeStruct((batch, heads, seq, HEAD_DIM), F32)
    return pl.pallas_call(
        functools.partial(_mixin_kernel, q_scale=HEAD_DIM ** -0.5 * math.log2(math.e)),
        out_shape=(jax.ShapeDtypeStruct((t, 2 * dm), BF16),
                   hm_shape, hm_shape, hm_shape),
        grid=(t // tm, 4),
        in_specs=[
            pl.BlockSpec((tm, d), lambda i, n: (i, 0)),
            vec, vec,
            pl.BlockSpec((d, dm), lambda i, n: (0, n)),
            pl.BlockSpec(ab.shape, lambda i, n: (0, 0, 0)),
            tab, tab,
        ],
        out_specs=(pl.BlockSpec((tm, 2 * dm), lambda i, n: (i, 0)),
                   hm, hm, hm),
        scratch_shapes=[pltpu.VMEM((tm, d), BF16)],
        compiler_params=_compiler_params(("parallel", "arbitrary")),
        name="mix_in",
    )(x, sh, sc, w_in, ab, cos, sin)


def _rope_tables(seq):
    half = HEAD_DIM // 2
    inv_freq = ROPE_THETA ** (-jnp.arange(half, dtype=F32) / half)
    ang = jnp.arange(seq, dtype=F32)[:, None] * inv_freq[None, :]
    cos, sin = jnp.cos(ang), jnp.sin(ang)
    return (jnp.concatenate([cos, cos], axis=-1),
            jnp.concatenate([-sin, sin], axis=-1))


def _dft_kernel(ca_ref, sa_ref, cb_ref, sb_ref, pa_ref, pb_ref, o_ref,
                cmat, smat):
    first = jnp.logical_and(pl.program_id(1) == 0, pl.program_id(2) == 0)

    @pl.when(first)
    def _():
        cb = cb_ref[...]
        sb = sb_ref[...]
        for a in range(ca_ref.shape[1]):
            ca = ca_ref[:, a:a + 1]
            sa = sa_ref[:, a:a + 1]
            cols = slice(a * LANES, (a + 1) * LANES)
            cmat[:, cols] = (ca * cb - sa * sb).astype(BF16)
            smat[:, cols] = (sa * cb + ca * sb).astype(BF16)

    o_ref[0] = (jnp.dot(cmat[...], pa_ref[0], preferred_element_type=F32)
                - jnp.dot(smat[...], pb_ref[0], preferred_element_type=F32))


def _dft(pab, tables, *, batch, seq, tm, tn):
    dm = pab.shape[-1] // 2
    ncb = dm // tn
    pab3 = pab.reshape(batch, seq, 2 * dm)
    ca, sa, cb, sb = tables
    coarse = pl.BlockSpec((tm, ca.shape[1]), lambda i, b, c: (i, 0))
    fine = pl.BlockSpec((tm, LANES), lambda i, b, c: (i, 0))
    return pl.pallas_call(
        _dft_kernel,
        out_shape=jax.ShapeDtypeStruct((batch, seq, dm), F32),
        grid=(seq // tm, batch, ncb),
        in_specs=[
            coarse, coarse, fine, fine,
            pl.BlockSpec((1, seq, tn), lambda i, b, c: (b, 0, c)),
            pl.BlockSpec((1, seq, tn), lambda i, b, c: (b, 0, c + ncb)),
        ],
        out_specs=pl.BlockSpec((1, tm, tn), lambda i, b, c: (b, i, c)),
        scratch_shapes=[pltpu.VMEM((tm, seq), BF16), pltpu.VMEM((tm, seq), BF16)],
        compiler_params=_compiler_params(("parallel", "arbitrary", "arbitrary")),
        name="dft",
    )(ca, sa, cb, sb, pab3, pab3)


def _dft_tables(seq):
    k = jnp.arange(seq)
    coarse = _dft_angles(seq, k, LANES * jnp.arange(seq // LANES))
    fine = _dft_angles(seq, k, jnp.arange(LANES))
    return jnp.cos(coarse), jnp.sin(coarse), jnp.cos(fine), jnp.sin(fine)


def _attn_kernel(q_ref, k_ref, v_ref, bias_ref, o_ref,
                 qp, kp, vp, s_buf, p_buf, pv, mb, lb):
    seq, hd = q_ref.shape[2], q_ref.shape[3]
    bq, kw = ATTN_BQ, ATTN_KW
    nblk = seq // bq

    assert kw == 2 * hd and s_buf.shape == (2, seq, hd)
    for src_ref, dst in ((q_ref, qp), (k_ref, kp), (v_ref, vp)):
        def load_prev(start, size, stride, src_ref=src_ref):
            return src_ref[0, 0, pl.ds(start, size, stride=stride), :]
        d_prev = 1
        for bi, d in enumerate(DILATIONS):
            ratio, sub, sub_prev = d // d_prev, seq // d, seq // d_prev
            piece = min(sub, ATTN_COPY_ROWS)
            for jp in range(d_prev):
                for r in range(ratio):
                    for c in range(sub // piece):
                        row0 = (jp + d_prev * r) * sub + c * piece
                        rows = slice(row0, row0 + piece)
                        x = load_prev(jp * sub_prev + r + c * piece * ratio, piece, ratio)
                        dst[bi, rows, :] = x.astype(BF16)
                        if 0 < bi < len(DILATIONS) - 1:
                            s_buf[bi % 2, rows, :] = x
            if d == 1:
                continue

            def load_prev(start, size, stride, half=bi % 2):
                return s_buf[half, pl.ds(start, size, stride=stride), :]
            d_prev = d

    ones = jnp.ones((kw, hd), BF16)

    for bi, d in enumerate(DILATIONS):
        sub = seq // d
        bps = sub // bq

        def place(n, d=d, sub=sub, bps=bps):
            j = n // bps
            t0 = (n % bps) * bq
            ks = jnp.clip(t0 - REACH, 0, sub - kw)
            qrows = pl.ds(pl.multiple_of(j * sub + t0, bq), bq)
            krows = pl.ds(pl.multiple_of(j * sub + ks, REACH), kw)
            if d > 1:
                orows = pl.ds(j + d * t0, bq, stride=d)
            else:
                orows = pl.ds(pl.multiple_of(t0, bq), bq)
            return qrows, krows, orows, (t0 - ks) // REACH

        def scores(n, carry, place=place, bi=bi):
            qrows, krows, _, edge = place(n)
            s = lax.dot_general(qp[bi, qrows, :], kp[bi, krows, :],
                                (((1,), (1,)), ((), ())), preferred_element_type=F32)
            s = s + bias_ref[edge]
            rows = pl.ds(pl.multiple_of(n * bq, bq), bq)
            s_buf[0, rows, :] = s[:, :hd]
            s_buf[1, rows, :] = s[:, hd:]
            return carry

        def softmax(n, carry, place=place, bi=bi):
            _, _, orows, _ = place(n)
            rows = pl.ds(pl.multiple_of(n * bq, bq), bq)
            s = jnp.concatenate([s_buf[0, rows, :], s_buf[1, rows, :]], axis=1)
            m = jnp.max(s, axis=-1, keepdims=True)
            p_buf[n] = jnp.exp2(s - m).astype(BF16)
            mb[bi, orows, :] = jnp.broadcast_to(m, (bq, hd))
            return carry

        def values(n, carry, place=place, bi=bi):
            _, krows, orows, _ = place(n)
            v1 = jnp.concatenate([vp[bi, krows, :], ones], axis=1)
            r = jnp.dot(p_buf[n], v1, preferred_element_type=F32)
            pv[bi, orows, :] = r[:, :hd]
            lb[bi, orows, :] = r[:, hd:]
            return carry

        lax.fori_loop(0, nblk, scores, 0, unroll=ATTN_UNROLL)
        lax.fori_loop(0, nblk, softmax, 0, unroll=ATTN_UNROLL)
        lax.fori_loop(0, nblk, values, 0, unroll=ATTN_UNROLL)

    def merge(c, carry):
        rows = pl.ds(pl.multiple_of(c * bq, bq), bq)
        branches = range(len(DILATIONS))
        ms = [mb[p, rows, :] for p in branches]
        top = functools.reduce(jnp.maximum, ms)
        es = [jnp.exp2(m - top) for m in ms]
        num = sum(es[p] * pv[p, rows, :] for p in branches)
        den = sum(es[p] * lb[p, rows, :] for p in branches)
        o_ref[0, rows, :] = num / den
        return carry

    lax.fori_loop(0, nblk, merge, 0, unroll=2)


def _attn_bias():
    r = jnp.arange(ATTN_BQ)[:, None]
    c = jnp.arange(ATTN_KW)[None, :]
    return jnp.stack([
        jnp.where(jnp.abs(edge * REACH + r - c) <= REACH, 0.0, MASK_VALUE).astype(F32)
        for edge in range(3)])


def _attn(q, k, v):
    batch, heads, seq, hd = q.shape
    nb = len(DILATIONS)
    nblk = seq // ATTN_BQ
    bias = _attn_bias()
    blk = pl.BlockSpec((1, 1, seq, hd), lambda b, h: (b, h, 0, 0))
    stat = pltpu.VMEM((nb, seq, hd), F32)
    gathered = pltpu.VMEM((nb, seq, hd), BF16)
    return pl.pallas_call(
        _attn_kernel,
        out_shape=jax.ShapeDtypeStruct((batch, seq, heads * hd), F32),
        grid=(batch, heads),
        in_specs=[blk, blk, blk, pl.BlockSpec(bias.shape, lambda b, h: (0, 0, 0))],
        out_specs=pl.BlockSpec((1, seq, hd), lambda b, h: (b, 0, h)),
        scratch_shapes=[
            gathered, gathered, gathered,
            pltpu.VMEM((ATTN_KW // hd, seq, hd), F32),
            pltpu.VMEM((nblk, ATTN_BQ, ATTN_KW), BF16),
            stat, stat, stat,
        ],
        compiler_params=_compiler_params(("parallel", "parallel")),
        name="attn",
    )(q, k, v, bias)


def _mixout_kernel(x_ref, yf_ref, ya_ref, gf_ref, ga_ref, gt_ref, w_ref, o_ref):
    df = yf_ref.shape[1]
    nf = (_rms(yf_ref[...]) * gf_ref[...]).astype(BF16)
    na = (_rms(ya_ref[...]) * ga_ref[...]).astype(BF16)
    out = (jnp.dot(nf, w_ref[:df, :], preferred_element_type=F32)
           + jnp.dot(na, w_ref[df:, :], preferred_element_type=F32))
    o_ref[...] = x_ref[...] + gt_ref[0] * out


def _mixout(x, yf, ya, gf, ga, gt, w_out, *, seq, tm):
    t, d = x.shape
    df, da = yf.shape[1], ya.shape[1]
    tpb = seq // tm
    return pl.pallas_call(
        _mixout_kernel,
        out_shape=jax.ShapeDtypeStruct((t, d), F32),
        grid=(t // tm,),
        in_specs=[
            pl.BlockSpec((tm, d), lambda i: (i, 0)),
            pl.BlockSpec((tm, df), lambda i: (i, 0)),
            pl.BlockSpec((tm, da), lambda i: (i, 0)),
            pl.BlockSpec((1, df), lambda i: (0, 0)),
            pl.BlockSpec((1, da), lambda i: (0, 0)),
            pl.BlockSpec((1, 1, d), lambda i: (i // tpb, 0, 0)),
            pl.BlockSpec((df + da, d), lambda i: (0, 0)),
        ],
        out_specs=pl.BlockSpec((tm, d), lambda i: (i, 0)),
        compiler_params=_compiler_params(("parallel",)),
        name="mix_out",
    )(x, yf, ya, gf, ga, gt, w_out)


def _tiles(seq, d_ff):
    tm = min(512, seq)
    tf = 512 if d_ff % 512 == 0 else LANES
    return tm, tf


def kernel(x, c, w_ada, b_ada, w_ffn1_gu, w_ffn1_down, w_mix_in, w_fourier,
           g_fourier_out, g_attn_out, w_mix_out, w_ffn2_gu, w_ffn2_down, g_final):
    batch, seq, d = x.shape
    depth = w_ada.shape[0]
    d_ff = w_ffn1_down.shape[1]
    t = batch * seq
    tm, tf = _tiles(seq, d_ff)

    rows = 8 * pl.cdiv(batch, 8)
    c_pad = jnp.zeros((rows, d), F32).at[:batch].set(c)
    mod = _ada(c_pad, w_ada, b_ada, tn=min(1024, d))
    mod = mod[:, :batch].reshape(depth, batch, N_MOD, 1, d)

    groups = w_fourier.shape[1]
    ab = _fprep(w_fourier.reshape(depth * groups, HEAD_DIM, HEAD_DIM), seq)
    ab = ab.reshape(depth, groups, HEAD_DIM, 2 * HEAD_DIM)
    rope_cos, rope_sin = _rope_tables(seq)
    dft_tables = _dft_tables(seq)
    g_fin = g_final.reshape(1, d)

    xt = x.reshape(t, d)
    for l in range(depth):
        sh1, sc1, g1, sh2, sc2, g2, sh3, sc3, g3 = (mod[l, :, i] for i in range(N_MOD))
        xt = _ffn(xt, sh1, sc1, g1, w_ffn1_gu[l].astype(BF16),
                  w_ffn1_down[l].astype(BF16), g_fin,
                  seq=seq, tm=tm, tf=tf, final_norm=False)
        pab, q, k, v = _mixin(xt, sh2, sc2, w_mix_in[l].astype(BF16), ab[l],
                              rope_cos, rope_sin, batch=batch, seq=seq, tm=tm)
        yf = _dft(pab, dft_tables, batch=batch, seq=seq, tm=tm, tn=min(512, pab.shape[1] // 2))
        ya = _attn(q, k, v)
        xt = _mixout(xt, yf.reshape(t, -1), ya.reshape(t, -1),
                     g_fourier_out[l].reshape(1, -1), g_attn_out[l].reshape(1, -1),
                     g2, w_mix_out[l].astype(BF16), seq=seq, tm=tm)
        xt = _ffn(xt, sh3, sc3, g3, w_ffn2_gu[l].astype(BF16),
                  w_ffn2_down[l].astype(BF16), g_fin,
                  seq=seq, tm=tm, tf=tf, final_norm=(l == depth - 1))
    return xt.reshape(batch, seq, d)
```

```python
import functools
import math
from typing import NamedTuple

import jax
import jax.numpy as jnp
from jax import lax
from jax.experimental import pallas as pl
from jax.experimental.pallas import tpu as pltpu

F32 = jnp.float32
BF16 = jnp.bfloat16

EPS = 1e-6
HEAD_DIM = 128
ROPE_THETA = 10000.0
DILATED_PATTERNS = ((128, 1), (512, 4), (2048, 16))
DILATIONS = tuple(d for _, d in DILATED_PATTERNS)
REACH = (DILATED_PATTERNS[0][0] // 2) // DILATED_PATTERNS[0][1]
assert all((w // 2) // d == REACH for w, d in DILATED_PATTERNS)
MASK_VALUE = -1e30
N_MOD = 9

LANES = 128
VMEM_LIMIT_BYTES = 56 * 1024 * 1024

ATTN_BQ = 128
ATTN_KW = ATTN_BQ + 2 * REACH
ATTN_COPY_ROWS = 128
ATTN_UNROLL = 16


def _compiler_params(semantics):
    return pltpu.CompilerParams(dimension_semantics=semantics,
                                vmem_limit_bytes=VMEM_LIMIT_BYTES)


def _rms(x):
    return x * lax.rsqrt(jnp.mean(x * x, axis=-1, keepdims=True) + EPS)


def _silu(x):
    return x * jax.nn.sigmoid(x)


def _ada_kernel(c_ref, w_ref, b_ref, o_ref):
    ca = _silu(c_ref[...]).astype(BF16)
    o_ref[0] = jnp.dot(ca, w_ref[0].astype(BF16),
                       preferred_element_type=F32) + b_ref[0]


def _ada(c_pad, w_ada, b_ada, *, tn):
    depth, d, n = w_ada.shape
    rows = c_pad.shape[0]
    return pl.pallas_call(
        _ada_kernel,
        out_shape=jax.ShapeDtypeStruct((depth, rows, n), F32),
        grid=(depth, n // tn),
        in_specs=[
            pl.BlockSpec((rows, d), lambda l, j: (0, 0)),
            pl.BlockSpec((1, d, tn), lambda l, j: (l, 0, j)),
            pl.BlockSpec((1, 1, tn), lambda l, j: (l, 0, j)),
        ],
        out_specs=pl.BlockSpec((1, rows, tn), lambda l, j: (l, 0, j)),
        compiler_params=_compiler_params(("parallel", "parallel")),
        name="ada",
    )(c_pad, w_ada, b_ada.reshape(depth, 1, n))


def _ffn_kernel(x_ref, sh_ref, sc_ref, gt_ref, wg_ref, wu_ref, wd_ref, gf_ref,
                o_ref, h_ref, *, final_norm):
    j = pl.program_id(1)

    @pl.when(j == 0)
    def _():
        h = _rms(x_ref[...]) * (1.0 + sc_ref[0]) + sh_ref[0]
        h_ref[...] = h.astype(BF16)
        o_ref[...] = jnp.zeros_like(o_ref)

    h = h_ref[...]
    g = jnp.dot(h, wg_ref[...], preferred_element_type=F32)
    u = jnp.dot(h, wu_ref[...], preferred_element_type=F32)
    a = (_silu(g) * u).astype(BF16)
    o_ref[...] += jnp.dot(a, wd_ref[...], preferred_element_type=F32)

    @pl.when(j == pl.num_programs(1) - 1)
    def _():
        y = x_ref[...] + (0.5 * gt_ref[0]) * o_ref[...]
        if final_norm:
            y = _rms(y) * gf_ref[...]
        o_ref[...] = y


def _ffn(x, sh, sc, gt, w_gu, w_down, g_final, *, layer, seq, tm, tf, final_norm):
    t, d = x.shape
    f = w_down.shape[1]
    nj = f // tf
    tpb = seq // tm
    vec = pl.BlockSpec((1, 1, d), lambda i, j: (i // tpb, 0, 0))
    return pl.pallas_call(
        functools.partial(_ffn_kernel, final_norm=final_norm),
        out_shape=jax.ShapeDtypeStruct((t, d), F32),
        grid=(t // tm, nj),
        in_specs=[
            pl.BlockSpec((tm, d), lambda i, j: (i, 0), pipeline_mode=pl.Buffered(1)),
            vec, vec, vec,
            pl.BlockSpec((None, d, tf), lambda i, j: (layer, 0, j)),
            pl.BlockSpec((None, d, tf), lambda i, j: (layer, 0, j + nj)),
            pl.BlockSpec((None, tf, d), lambda i, j: (layer, j, 0)),
            pl.BlockSpec((1, d), lambda i, j: (0, 0)),
        ],
        out_specs=pl.BlockSpec((tm, d), lambda i, j: (i, 0)),
        scratch_shapes=[pltpu.VMEM((tm, d), BF16)],
        compiler_params=_compiler_params(("parallel", "arbitrary")),
        name="ffn",
    )(x, sh, sc, gt, w_gu, w_gu, w_down, g_final)


def _fprep_kernel(cc_ref, sc_ref, w_ref, ab_ref, *, scale):
    w = w_ref[0]
    hc = w.shape[0]
    a = jnp.dot(cc_ref[...], w, preferred_element_type=F32,
                precision=lax.Precision.HIGHEST)
    b = jnp.dot(sc_ref[...], w, preferred_element_type=F32,
                precision=lax.Precision.HIGHEST)
    ab_ref[0, :, :hc] = (a * scale).astype(BF16)
    ab_ref[0, :, hc:] = (b * scale).astype(BF16)


def _fprep(w_fourier, seq):
    n, hc, _ = w_fourier.shape
    ang = _dft_angles(hc, jnp.arange(hc), jnp.arange(hc))
    scale = 1.0 / math.sqrt(seq * hc)
    sq = pl.BlockSpec((hc, hc), lambda g: (0, 0))
    return pl.pallas_call(
        functools.partial(_fprep_kernel, scale=scale),
        out_shape=jax.ShapeDtypeStruct((n, hc, 2 * hc), BF16),
        grid=(n,),
        in_specs=[sq, sq, pl.BlockSpec((1, hc, hc), lambda g: (g, 0, 0))],
        out_specs=pl.BlockSpec((1, hc, 2 * hc), lambda g: (g, 0, 0)),
        compiler_params=_compiler_params(("parallel",)),
        name="fprep",
    )(jnp.cos(ang), jnp.sin(ang), w_fourier)


def _dft_angles(n, rows, cols):
    idx = (rows[:, None] * cols[None, :]) % n
    return idx.astype(F32) * (2.0 * math.pi / n)


def _mixin_kernel(x_ref, sh_ref, sc_ref, w_ref, ab_ref, cos_ref, sin_ref,
                  pab_ref, q_ref, k_ref, v_ref, h_ref, *, q_scale):
    n = pl.program_id(1)
    hc = HEAD_DIM
    groups = pab_ref.shape[1] // (2 * hc)
    heads = q_ref.shape[1]

    @pl.when(n == 0)
    def _():
        h = _rms(x_ref[...]) * (1.0 + sc_ref[0]) + sh_ref[0]
        h_ref[...] = h.astype(BF16)

    p = jnp.dot(h_ref[...], w_ref[...], preferred_element_type=F32)

    @pl.when(n == 0)
    def _():
        pb = p.astype(BF16)
        for g in range(groups):
            r = jnp.dot(pb[:, g * hc:(g + 1) * hc], ab_ref[g],
                        preferred_element_type=F32)
            pab_ref[:, g * hc:(g + 1) * hc] = r[:, :hc].astype(BF16)
            pab_ref[:, (groups + g) * hc:(groups + g + 1) * hc] = (
                r[:, hc:].astype(BF16))

    def rope_to(dst_ref, scale):
        cos = cos_ref[...] * scale
        sin = sin_ref[...] * scale
        for hh in range(heads):
            t = p[:, hh * hc:(hh + 1) * hc]
            dst_ref[0, hh] = t * cos + pltpu.roll(t, hc // 2, 1) * sin

    @pl.when(n == 1)
    def _():
        rope_to(q_ref, q_scale)

    @pl.when(n == 2)
    def _():
        rope_to(k_ref, 1.0)

    @pl.when(n == 3)
    def _():
        for hh in range(heads):
            v_ref[0, hh] = p[:, hh * hc:(hh + 1) * hc]


def _mixin(x, sh, sc, w_in, ab, cos, sin, *, layer, batch, seq, tm):
    t, d = x.shape
    dm = w_in.shape[2] // 4
    heads = dm // HEAD_DIM
    tpb = seq // tm
    vec = pl.BlockSpec((1, 1, d), lambda i, n: (i // tpb, 0, 0))
    tab = pl.BlockSpec((tm, HEAD_DIM), lambda i, n: (i % tpb, 0))
    hm = pl.BlockSpec((1, heads, tm, HEAD_DIM),
                      lambda i, n: (i // tpb, 0, i % tpb, 0))
    hm_shape = jax.ShapeDtypeStruct((batch, heads, seq, HEAD_DIM), F32)
    return pl.pallas_call(
        functools.partial(_mixin_kernel, q_scale=HEAD_DIM ** -0.5 * math.log2(math.e)),
        out_shape=(jax.ShapeDtypeStruct((t, 2 * dm), BF16),
                   hm_shape, hm_shape, hm_shape),
        grid=(t // tm, 4),
        in_specs=[
            pl.BlockSpec((tm, d), lambda i, n: (i, 0)),
            vec, vec,
            pl.BlockSpec((None, d, dm), lambda i, n: (layer, 0, n)),
            pl.BlockSpec((None,) + ab.shape[1:], lambda i, n: (layer, 0, 0, 0)),
            tab, tab,
        ],
        out_specs=(pl.BlockSpec((tm, 2 * dm), lambda i, n: (i, 0)),
                   hm, hm, hm),
        scratch_shapes=[pltpu.VMEM((tm, d), BF16)],
        compiler_params=_compiler_params(("parallel", "arbitrary")),
        name="mix_in",
    )(x, sh, sc, w_in, ab, cos, sin)


def _rope_tables(seq):
    half = HEAD_DIM // 2
    inv_freq = ROPE_THETA ** (-jnp.arange(half, dtype=F32) / half)
    ang = jnp.arange(seq, dtype=F32)[:, None] * inv_freq[None, :]
    cos, sin = jnp.cos(ang), jnp.sin(ang)
    return (jnp.concatenate([cos, cos], axis=-1),
            jnp.concatenate([-sin, sin], axis=-1))


def _dft_kernel(ca_ref, sa_ref, cb_ref, sb_ref, pa_ref, pb_ref, o_ref,
                cmat, smat):
    first = jnp.logical_and(pl.program_id(1) == 0, pl.program_id(2) == 0)

    @pl.when(first)
    def _():
        cb = cb_ref[...]
        sb = sb_ref[...]
        for a in range(ca_ref.shape[1]):
            ca = ca_ref[:, a:a + 1]
            sa = sa_ref[:, a:a + 1]
            cols = slice(a * LANES, (a + 1) * LANES)
            cmat[:, cols] = (ca * cb - sa * sb).astype(BF16)
            smat[:, cols] = (sa * cb + ca * sb).astype(BF16)

    o_ref[0] = (jnp.dot(cmat[...], pa_ref[0], preferred_element_type=F32)
                - jnp.dot(smat[...], pb_ref[0], preferred_element_type=F32))


def _dft(pab, tables, *, batch, seq, tm, tn):
    dm = pab.shape[-1] // 2
    ncb = dm // tn
    pab3 = pab.reshape(batch, seq, 2 * dm)
    ca, sa, cb, sb = tables
    coarse = pl.BlockSpec((tm, ca.shape[1]), lambda i, b, c: (i, 0))
    fine = pl.BlockSpec((tm, LANES), lambda i, b, c: (i, 0))
    return pl.pallas_call(
        _dft_kernel,
        out_shape=jax.ShapeDtypeStruct((batch, seq, dm), F32),
        grid=(seq // tm, batch, ncb),
        in_specs=[
            coarse, coarse, fine, fine,
            pl.BlockSpec((1, seq, tn), lambda i, b, c: (b, 0, c)),
            pl.BlockSpec((1, seq, tn), lambda i, b, c: (b, 0, c + ncb)),
        ],
        out_specs=pl.BlockSpec((1, tm, tn), lambda i, b, c: (b, i, c)),
        scratch_shapes=[pltpu.VMEM((tm, seq), BF16), pltpu.VMEM((tm, seq), BF16)],
        compiler_params=_compiler_params(("parallel", "arbitrary", "arbitrary")),
        name="dft",
    )(ca, sa, cb, sb, pab3, pab3)


def _dft_tables(seq):
    k = jnp.arange(seq)
    coarse = _dft_angles(seq, k, LANES * jnp.arange(seq // LANES))
    fine = _dft_angles(seq, k, jnp.arange(LANES))
    return jnp.cos(coarse), jnp.sin(coarse), jnp.cos(fine), jnp.sin(fine)


def _attn_kernel(q_ref, k_ref, v_ref, bias_ref, o_ref,
                 qp, kp, vp, s_buf, p_buf, pv, mb, lb):
    seq, hd = q_ref.shape[2], q_ref.shape[3]
    bq, kw = ATTN_BQ, ATTN_KW
    nblk = seq // bq

    assert kw == 2 * hd and s_buf.shape == (2, seq, hd)
    for src_ref, dst in ((q_ref, qp), (k_ref, kp), (v_ref, vp)):
        def load_prev(start, size, stride, src_ref=src_ref):
            return src_ref[0, 0, pl.ds(start, size, stride=stride), :]
        d_prev = 1
        for bi, d in enumerate(DILATIONS):
            ratio, sub, sub_prev = d // d_prev, seq // d, seq // d_prev
            piece = min(sub, ATTN_COPY_ROWS)
            for jp in range(d_prev):
                for r in range(ratio):
                    for c in range(sub // piece):
                        row0 = (jp + d_prev * r) * sub + c * piece
                        rows = slice(row0, row0 + piece)
                        x = load_prev(jp * sub_prev + r + c * piece * ratio, piece, ratio)
                        dst[bi, rows, :] = x.astype(BF16)
                        if 0 < bi < len(DILATIONS) - 1:
                            s_buf[bi % 2, rows, :] = x
            if d == 1:
                continue

            def load_prev(start, size, stride, half=bi % 2):
                return s_buf[half, pl.ds(start, size, stride=stride), :]
            d_prev = d

    ones = jnp.ones((kw, hd), BF16)

    for bi, d in enumerate(DILATIONS):
        sub = seq // d
        bps = sub // bq

        def place(n, d=d, sub=sub, bps=bps):
            j = n // bps
            t0 = (n % bps) * bq
            ks = jnp.clip(t0 - REACH, 0, sub - kw)
            qrows = pl.ds(pl.multiple_of(j * sub + t0, bq), bq)
            krows = pl.ds(pl.multiple_of(j * sub + ks, REACH), kw)
            if d > 1:
                orows = pl.ds(j + d * t0, bq, stride=d)
            else:
                orows = pl.ds(pl.multiple_of(t0, bq), bq)
            return qrows, krows, orows, (t0 - ks) // REACH

        def scores(n, carry, place=place, bi=bi):
            qrows, krows, _, edge = place(n)
            s = lax.dot_general(qp[bi, qrows, :], kp[bi, krows, :],
                                (((1,), (1,)), ((), ())), preferred_element_type=F32)
            s = s + bias_ref[edge]
            rows = pl.ds(pl.multiple_of(n * bq, bq), bq)
            s_buf[0, rows, :] = s[:, :hd]
            s_buf[1, rows, :] = s[:, hd:]
            return carry

        def softmax(n, carry, place=place, bi=bi):
            _, _, orows, _ = place(n)
            rows = pl.ds(pl.multiple_of(n * bq, bq), bq)
            s = jnp.concatenate([s_buf[0, rows, :], s_buf[1, rows, :]], axis=1)
            m = jnp.max(s, axis=-1, keepdims=True)
            p_buf[n] = jnp.exp2(s - m).astype(BF16)
            mb[bi, orows, :] = jnp.broadcast_to(m, (bq, hd))
            return carry

        def values(n, carry, place=place, bi=bi):
            _, krows, orows, _ = place(n)
            v1 = jnp.concatenate([vp[bi, krows, :], ones], axis=1)
            r = jnp.dot(p_buf[n], v1, preferred_element_type=F32)
            pv[bi, orows, :] = r[:, :hd]
            lb[bi, orows, :] = r[:, hd:]
            return carry

        lax.fori_loop(0, nblk, scores, 0, unroll=ATTN_UNROLL)
        lax.fori_loop(0, nblk, softmax, 0, unroll=ATTN_UNROLL)
        lax.fori_loop(0, nblk, values, 0, unroll=ATTN_UNROLL)

    def merge(c, carry):
        rows = pl.ds(pl.multiple_of(c * bq, bq), bq)
        branches = range(len(DILATIONS))
        ms = [mb[p, rows, :] for p in branches]
        top = functools.reduce(jnp.maximum, ms)
        es = [jnp.exp2(m - top) for m in ms]
        num = sum(es[p] * pv[p, rows, :] for p in branches)
        den = sum(es[p] * lb[p, rows, :] for p in branches)
        o_ref[0, rows, :] = num / den
        return carry

    lax.fori_loop(0, nblk, merge, 0, unroll=2)


def _attn_bias():
    r = jnp.arange(ATTN_BQ)[:, None]
    c = jnp.arange(ATTN_KW)[None, :]
    return jnp.stack([
        jnp.where(jnp.abs(edge * REACH + r - c) <= REACH, 0.0, MASK_VALUE).astype(F32)
        for edge in range(3)])


def _attn(q, k, v):
    batch, heads, seq, hd = q.shape
    nb = len(DILATIONS)
    nblk = seq // ATTN_BQ
    bias = _attn_bias()
    blk = pl.BlockSpec((1, 1, seq, hd), lambda b, h: (b, h, 0, 0))
    stat = pltpu.VMEM((nb, seq, hd), F32)
    gathered = pltpu.VMEM((nb, seq, hd), BF16)
    return pl.pallas_call(
        _attn_kernel,
        out_shape=jax.ShapeDtypeStruct((batch, seq, heads * hd), F32),
        grid=(batch, heads),
        in_specs=[blk, blk, blk, pl.BlockSpec(bias.shape, lambda b, h: (0, 0, 0))],
        out_specs=pl.BlockSpec((1, seq, hd), lambda b, h: (b, 0, h)),
        scratch_shapes=[
            gathered, gathered, gathered,
            pltpu.VMEM((ATTN_KW // hd, seq, hd), F32),
            pltpu.VMEM((nblk, ATTN_BQ, ATTN_KW), BF16),
            stat, stat, stat,
        ],
        compiler_params=_compiler_params(("parallel", "parallel")),
        name="attn",
    )(q, k, v, bias)


def _mixout_kernel(x_ref, yf_ref, ya_ref, gf_ref, ga_ref, gt_ref, w_ref, o_ref):
    df = yf_ref.shape[1]
    nf = (_rms(yf_ref[...]) * gf_ref[...]).astype(BF16)
    na = (_rms(ya_ref[...]) * ga_ref[...]).astype(BF16)
    out = (jnp.dot(nf, w_ref[:df, :], preferred_element_type=F32)
           + jnp.dot(na, w_ref[df:, :], preferred_element_type=F32))
    o_ref[...] = x_ref[...] + gt_ref[0] * out


def _mixout(x, yf, ya, gf, ga, gt, w_out, *, layer, seq, tm):
    t, d = x.shape
    df, da = yf.shape[1], ya.shape[1]
    tpb = seq // tm
    return pl.pallas_call(
        _mixout_kernel,
        out_shape=jax.ShapeDtypeStruct((t, d), F32),
        grid=(t // tm,),
        in_specs=[
            pl.BlockSpec((tm, d), lambda i: (i, 0)),
            pl.BlockSpec((tm, df), lambda i: (i, 0)),
            pl.BlockSpec((tm, da), lambda i: (i, 0)),
            pl.BlockSpec((1, df), lambda i: (0, 0)),
            pl.BlockSpec((1, da), lambda i: (0, 0)),
            pl.BlockSpec((1, 1, d), lambda i: (i // tpb, 0, 0)),
            pl.BlockSpec((None, df + da, d), lambda i: (layer, 0, 0)),
        ],
        out_specs=pl.BlockSpec((tm, d), lambda i: (i, 0)),
        compiler_params=_compiler_params(("parallel",)),
        name="mix_out",
    )(x, yf, ya, gf, ga, gt, w_out)


class _Tiles(NamedTuple):
    ffn_rows: int
    ffn_cols: int
    mix_rows: int
    dft_rows: int
    dft_cols: int
    ada_cols: int


def _tiles(seq, d, d_ff):
    ffn_cols = 512 if d_ff % 512 == 0 else LANES
    return _Tiles(ffn_rows=min(1024, seq), ffn_cols=ffn_cols, mix_rows=min(512, seq),
                  dft_rows=min(512, seq), dft_cols=min(512, d // 2), ada_cols=min(1024, d))


def kernel(x, c, w_ada, b_ada, w_ffn1_gu, w_ffn1_down, w_mix_in, w_fourier,
           g_fourier_out, g_attn_out, w_mix_out, w_ffn2_gu, w_ffn2_down, g_final):
    batch, seq, d = x.shape
    depth = w_ada.shape[0]
    d_ff = w_ffn1_down.shape[1]
    t = batch * seq
    tl = _tiles(seq, d, d_ff)

    rows = 8 * pl.cdiv(batch, 8)
    c_pad = jnp.zeros((rows, d), F32).at[:batch].set(c)
    mod = _ada(c_pad, w_ada, b_ada, tn=tl.ada_cols)
    mod = mod[:, :batch].reshape(depth, batch, N_MOD, 1, d)

    groups = w_fourier.shape[1]
    ab = _fprep(w_fourier.reshape(depth * groups, HEAD_DIM, HEAD_DIM), seq)
    ab = ab.reshape(depth, groups, HEAD_DIM, 2 * HEAD_DIM)
    rope_cos, rope_sin = _rope_tables(seq)
    dft_tables = _dft_tables(seq)
    g_fin = g_final.reshape(1, d)

    wgu1, wd1, wgu2, wd2, w_in, w_out = (
        w.astype(BF16) for w in (w_ffn1_gu, w_ffn1_down, w_ffn2_gu, w_ffn2_down,
                                 w_mix_in, w_mix_out))

    xt = x.reshape(t, d)
    for l in range(depth):
        sh1, sc1, g1, sh2, sc2, g2, sh3, sc3, g3 = (mod[l, :, i] for i in range(N_MOD))
        xt = _ffn(xt, sh1, sc1, g1, wgu1, wd1, g_fin, layer=l,
                  seq=seq, tm=tl.ffn_rows, tf=tl.ffn_cols, final_norm=False)
        pab, q, k, v = _mixin(xt, sh2, sc2, w_in, ab, rope_cos, rope_sin, layer=l,
                              batch=batch, seq=seq, tm=tl.mix_rows)
        yf = _dft(pab, dft_tables, batch=batch, seq=seq, tm=tl.dft_rows, tn=tl.dft_cols)
        ya = _attn(q, k, v)
        xt = _mixout(xt, yf.reshape(t, -1), ya.reshape(t, -1),
                     g_fourier_out[l].reshape(1, -1), g_attn_out[l].reshape(1, -1),
                     g2, w_out, layer=l, seq=seq, tm=tl.mix_rows)
        xt = _ffn(xt, sh3, sc3, g3, wgu2, wd2, g_fin, layer=l,
                  seq=seq, tm=tl.ffn_rows, tf=tl.ffn_cols, final_norm=(l == depth - 1))
    return xt.reshape(batch, seq, d)
```

```python
import functools
import math
from typing import NamedTuple

import jax
import jax.numpy as jnp
from jax import lax
from jax.experimental import pallas as pl
from jax.experimental.pallas import tpu as pltpu

F32 = jnp.float32
BF16 = jnp.bfloat16

EPS = 1e-6
HEAD_DIM = 128
ROPE_THETA = 10000.0
DILATED_PATTERNS = ((128, 1), (512, 4), (2048, 16))
DILATIONS = tuple(d for _, d in DILATED_PATTERNS)
REACH = (DILATED_PATTERNS[0][0] // 2) // DILATED_PATTERNS[0][1]
assert all((w // 2) // d == REACH for w, d in DILATED_PATTERNS)
MASK_VALUE = -1e30
N_MOD = 9

LANES = 128
VMEM_LIMIT_BYTES = 56 * 1024 * 1024

ATTN_BQ = 128
ATTN_KW = ATTN_BQ + 2 * REACH
ATTN_COPY_ROWS = 128
ATTN_UNROLL = 16


def _compiler_params(semantics):
    return pltpu.CompilerParams(dimension_semantics=semantics,
                                vmem_limit_bytes=VMEM_LIMIT_BYTES)


def _rms(x):
    return x * lax.rsqrt(jnp.mean(x * x, axis=-1, keepdims=True) + EPS)


def _silu(x):
    return x * jax.nn.sigmoid(x)


def _ada_kernel(c_ref, w_ref, b_ref, o_ref):
    ca = _silu(c_ref[...]).astype(BF16)
    o_ref[0] = jnp.dot(ca, w_ref[0].astype(BF16),
                       preferred_element_type=F32) + b_ref[0]


def _ada(c_pad, w_ada, b_ada, *, tn):
    depth, d, n = w_ada.shape
    rows = c_pad.shape[0]
    return pl.pallas_call(
        _ada_kernel,
        out_shape=jax.ShapeDtypeStruct((depth, rows, n), F32),
        grid=(depth, n // tn),
        in_specs=[
            pl.BlockSpec((rows, d), lambda l, j: (0, 0)),
            pl.BlockSpec((1, d, tn), lambda l, j: (l, 0, j)),
            pl.BlockSpec((1, 1, tn), lambda l, j: (l, 0, j)),
        ],
        out_specs=pl.BlockSpec((1, rows, tn), lambda l, j: (l, 0, j)),
        compiler_params=_compiler_params(("parallel", "parallel")),
        name="ada",
    )(c_pad, w_ada, b_ada.reshape(depth, 1, n))


def _ffn_kernel(x_ref, sh_ref, sc_ref, gt_ref, wg_ref, wu_ref, wd_ref, gf_ref,
                o_ref, h_ref, *, final_norm):
    j = pl.program_id(1)

    @pl.when(j == 0)
    def _():
        h = _rms(x_ref[...]) * (1.0 + sc_ref[0]) + sh_ref[0]
        h_ref[...] = h.astype(BF16)
        o_ref[...] = jnp.zeros_like(o_ref)

    h = h_ref[...]
    g = jnp.dot(h, wg_ref[...], preferred_element_type=F32)
    u = jnp.dot(h, wu_ref[...], preferred_element_type=F32)
    a = (_silu(g) * u).astype(BF16)
    o_ref[...] += jnp.dot(a, wd_ref[...], preferred_element_type=F32)

    @pl.when(j == pl.num_programs(1) - 1)
    def _():
        y = x_ref[...] + (0.5 * gt_ref[0]) * o_ref[...]
        if final_norm:
            y = _rms(y) * gf_ref[...]
        o_ref[...] = y


def _ffn(x, sh, sc, gt, w_gu, w_down, g_final, *, layer, seq, tm, tf, final_norm):
    t, d = x.shape
    f = w_down.shape[1]
    nj = f // tf
    tpb = seq // tm
    vec = pl.BlockSpec((1, 1, d), lambda i, j: (i // tpb, 0, 0))
    return pl.pallas_call(
        functools.partial(_ffn_kernel, final_norm=final_norm),
        out_shape=jax.ShapeDtypeStruct((t, d), F32),
        grid=(t // tm, nj),
        in_specs=[
            pl.BlockSpec((tm, d), lambda i, j: (i, 0)),
            vec, vec, vec,
            pl.BlockSpec((None, d, tf), lambda i, j: (layer, 0, j)),
            pl.BlockSpec((None, d, tf), lambda i, j: (layer, 0, j + nj)),
            pl.BlockSpec((None, tf, d), lambda i, j: (layer, j, 0)),
            pl.BlockSpec((1, d), lambda i, j: (0, 0)),
        ],
        out_specs=pl.BlockSpec((tm, d), lambda i, j: (i, 0)),
        scratch_shapes=[pltpu.VMEM((tm, d), BF16)],
        compiler_params=_compiler_params(("parallel", "arbitrary")),
        name="ffn",
    )(x, sh, sc, gt, w_gu, w_gu, w_down, g_final)


def _fprep_kernel(cc_ref, sc_ref, w_ref, ab_ref, *, scale):
    w = w_ref[0]
    hc = w.shape[0]
    a = jnp.dot(cc_ref[...], w, preferred_element_type=F32,
                precision=lax.Precision.HIGHEST)
    b = jnp.dot(sc_ref[...], w, preferred_element_type=F32,
                precision=lax.Precision.HIGHEST)
    ab_ref[0, :, :hc] = (a * scale).astype(BF16)
    ab_ref[0, :, hc:] = (b * scale).astype(BF16)


def _fprep(w_fourier, seq):
    n, hc, _ = w_fourier.shape
    ang = _dft_angles(hc, jnp.arange(hc), jnp.arange(hc))
    scale = 1.0 / math.sqrt(seq * hc)
    sq = pl.BlockSpec((hc, hc), lambda g: (0, 0))
    return pl.pallas_call(
        functools.partial(_fprep_kernel, scale=scale),
        out_shape=jax.ShapeDtypeStruct((n, hc, 2 * hc), BF16),
        grid=(n,),
        in_specs=[sq, sq, pl.BlockSpec((1, hc, hc), lambda g: (g, 0, 0))],
        out_specs=pl.BlockSpec((1, hc, 2 * hc), lambda g: (g, 0, 0)),
        compiler_params=_compiler_params(("parallel",)),
        name="fprep",
    )(jnp.cos(ang), jnp.sin(ang), w_fourier)


def _dft_angles(n, rows, cols):
    idx = (rows[:, None] * cols[None, :]) % n
    return idx.astype(F32) * (2.0 * math.pi / n)


def _mixin_kernel(x_ref, sh_ref, sc_ref, w_ref, ab_ref, cos_ref, sin_ref,
                  pab_ref, q_ref, k_ref, v_ref, h_ref, *, q_scale):
    n = pl.program_id(1)
    hc = HEAD_DIM
    groups = pab_ref.shape[1] // (2 * hc)
    heads = q_ref.shape[1]

    @pl.when(n == 0)
    def _():
        h = _rms(x_ref[...]) * (1.0 + sc_ref[0]) + sh_ref[0]
        h_ref[...] = h.astype(BF16)

    p = jnp.dot(h_ref[...], w_ref[...], preferred_element_type=F32)

    @pl.when(n == 0)
    def _():
        pb = p.astype(BF16)
        for g in range(groups):
            r = jnp.dot(pb[:, g * hc:(g + 1) * hc], ab_ref[g],
                        preferred_element_type=F32)
            pab_ref[:, g * hc:(g + 1) * hc] = r[:, :hc].astype(BF16)
            pab_ref[:, (groups + g) * hc:(groups + g + 1) * hc] = (
                r[:, hc:].astype(BF16))

    def rope_to(dst_ref, scale):
        cos = cos_ref[...] * scale
        sin = sin_ref[...] * scale
        for hh in range(heads):
            t = p[:, hh * hc:(hh + 1) * hc]
            dst_ref[0, hh] = t * cos + pltpu.roll(t, hc // 2, 1) * sin

    @pl.when(n == 1)
    def _():
        rope_to(q_ref, q_scale)

    @pl.when(n == 2)
    def _():
        rope_to(k_ref, 1.0)

    @pl.when(n == 3)
    def _():
        for hh in range(heads):
            v_ref[0, hh] = p[:, hh * hc:(hh + 1) * hc]


def _mixin(x, sh, sc, w_in, ab, cos, sin, *, layer, batch, seq, tm):
    t, d = x.shape
    dm = w_in.shape[2] // 4
    heads = dm // HEAD_DIM
    tpb = seq // tm
    vec = pl.BlockSpec((1, 1, d), lambda i, n: (i // tpb, 0, 0))
    tab = pl.BlockSpec((tm, HEAD_DIM), lambda i, n: (i % tpb, 0))
    hm = pl.BlockSpec((1, heads, tm, HEAD_DIM),
                      lambda i, n: (i // tpb, 0, i % tpb, 0))
    hm_shape = jax.ShapeDtypeStruct((batch, heads, seq, HEAD_DIM), F32)
    return pl.pallas_call(
        functools.partial(_mixin_kernel, q_scale=HEAD_DIM ** -0.5 * math.log2(math.e)),
        out_shape=(jax.ShapeDtypeStruct((t, 2 * dm), BF16),
                   hm_shape, hm_shape, hm_shape),
        grid=(t // tm, 4),
        in_specs=[
            pl.BlockSpec((tm, d), lambda i, n: (i, 0)),
            vec, vec,
            pl.BlockSpec((None, d, dm), lambda i, n: (layer, 0, n)),
            pl.BlockSpec((None,) + ab.shape[1:], lambda i, n: (layer, 0, 0, 0)),
            tab, tab,
        ],
        out_specs=(pl.BlockSpec((tm, 2 * dm), lambda i, n: (i, 0)),
                   hm, hm, hm),
        scratch_shapes=[pltpu.VMEM((tm, d), BF16)],
        compiler_params=_compiler_params(("parallel", "arbitrary")),
        name="mix_in",
    )(x, sh, sc, w_in, ab, cos, sin)


def _rope_tables(seq):
    half = HEAD_DIM // 2
    inv_freq = ROPE_THETA ** (-jnp.arange(half, dtype=F32) / half)
    ang = jnp.arange(seq, dtype=F32)[:, None] * inv_freq[None, :]
    cos, sin = jnp.cos(ang), jnp.sin(ang)
    return (jnp.concatenate([cos, cos], axis=-1),
            jnp.concatenate([-sin, sin], axis=-1))


DFT_RADIX = 4


def _dft_fold_kernel(z0_ref, z1_ref, z2_ref, z3_ref, w_ref):
    dm = w_ref.shape[3] // 2
    a = [z[0, 0, :, :dm].astype(F32) for z in (z0_ref, z1_ref, z2_ref, z3_ref)]
    p = [z[0, 0, :, dm:].astype(F32) for z in (z0_ref, z1_ref, z2_ref, z3_ref)]
    t0r, t0i = a[0] + a[2], -(p[0] + p[2])
    t1r, t1i = a[0] - a[2], p[2] - p[0]
    t2r, t2i = a[1] + a[3], -(p[1] + p[3])
    t3r, t3i = a[1] - a[3], p[3] - p[1]
    parts = ((t0r + t2r, t0i + t2i),
             (t1r + t3i, t1i - t3r),
             (t0r - t2r, t0i - t2i),
             (t1r - t3i, t1i + t3r))
    for r, (re, im) in enumerate(parts):
        w_ref[0, r, :, :dm] = re.astype(BF16)
        w_ref[0, r, :, dm:] = im.astype(BF16)


def _dft_fold(pab, *, batch, seq, tr):
    width = pab.shape[-1]
    nq = seq // DFT_RADIX
    z = pab.reshape(batch, DFT_RADIX, nq, width)
    quarter = [pl.BlockSpec((1, 1, tr, width), lambda b, i, q=q: (b, q, i, 0))
               for q in range(DFT_RADIX)]
    return pl.pallas_call(
        _dft_fold_kernel,
        out_shape=jax.ShapeDtypeStruct((batch, DFT_RADIX, nq, width), BF16),
        grid=(batch, nq // tr),
        in_specs=quarter,
        out_specs=pl.BlockSpec((1, DFT_RADIX, tr, width), lambda b, i: (b, 0, i, 0)),
        compiler_params=_compiler_params(("parallel", "parallel")),
        name="dft_fold",
    )(z, z, z, z)


def _dft_kernel(ca_ref, sa_ref, cb_ref, sb_ref, wre_ref, wim_ref, o_ref, cmat, smat):
    first = jnp.logical_and(pl.program_id(2) == 0, pl.program_id(3) == 0)

    @pl.when(first)
    def _():
        cb = cb_ref[...]
        sb = sb_ref[...]
        for a in range(ca_ref.shape[1]):
            ca = ca_ref[:, a:a + 1]
            sa = sa_ref[:, a:a + 1]
            cols = slice(a * LANES, (a + 1) * LANES)
            cmat[:, cols] = (ca * cb - sa * sb).astype(BF16)
            smat[:, cols] = (sa * cb + ca * sb).astype(BF16)

    o_ref[0] = (jnp.dot(cmat[...], wre_ref[0, 0], preferred_element_type=F32)
                + jnp.dot(smat[...], wim_ref[0, 0], preferred_element_type=F32))


def _dft(w, tables, *, batch, seq, tm, tn):
    dm = w.shape[-1] // 2
    nq = seq // DFT_RADIX
    ncb, nrb = dm // tn, nq // tm
    ca, sa, cb, sb = tables
    coarse = pl.BlockSpec((tm, ca.shape[1]), lambda r, i, b, c: (r * nrb + i, 0))
    fine = pl.BlockSpec((tm, LANES), lambda r, i, b, c: (r * nrb + i, 0))
    out = pl.pallas_call(
        _dft_kernel,
        out_shape=jax.ShapeDtypeStruct((batch, nq, DFT_RADIX * dm), F32),
        grid=(DFT_RADIX, nrb, batch, ncb),
        in_specs=[
            coarse, coarse, fine, fine,
            pl.BlockSpec((1, 1, nq, tn), lambda r, i, b, c: (b, r, 0, c)),
            pl.BlockSpec((1, 1, nq, tn), lambda r, i, b, c: (b, r, 0, c + ncb)),
        ],
        out_specs=pl.BlockSpec((1, tm, tn), lambda r, i, b, c: (b, i, r * ncb + c)),
        scratch_shapes=[pltpu.VMEM((tm, nq), BF16), pltpu.VMEM((tm, nq), BF16)],
        compiler_params=_compiler_params(
            ("parallel", "parallel", "arbitrary", "arbitrary")),
        name="dft",
    )(ca, sa, cb, sb, w, w)
    return out.reshape(batch, seq, dm)


def _dft_tables(seq):
    nq = seq // DFT_RADIX
    row = jnp.arange(seq)
    k = DFT_RADIX * (row % nq) + row // nq
    coarse = _dft_angles(seq, k, LANES * jnp.arange(nq // LANES))
    fine = _dft_angles(seq, k, jnp.arange(LANES))
    return jnp.cos(coarse), jnp.sin(coarse), jnp.cos(fine), jnp.sin(fine)


def _attn_kernel(q_ref, k_ref, v_ref, bias_ref, o_ref,
                 qp, kp, vp, s_buf, p_buf, pv, mb, lb):
    seq, hd = q_ref.shape[2], q_ref.shape[3]
    bq, kw = ATTN_BQ, ATTN_KW
    nblk = seq // bq

    assert kw == 2 * hd and s_buf.shape == (2, seq, hd)
    for src_ref, dst in ((q_ref, qp), (k_ref, kp), (v_ref, vp)):
        def load_prev(start, size, stride, src_ref=src_ref):
            return src_ref[0, 0, pl.ds(start, size, stride=stride), :]
        d_prev = 1
        for bi, d in enumerate(DILATIONS):
            ratio, sub, sub_prev = d // d_prev, seq // d, seq // d_prev
            piece = min(sub, ATTN_COPY_ROWS)
            for jp in range(d_prev):
                for r in range(ratio):
                    for c in range(sub // piece):
                        row0 = (jp + d_prev * r) * sub + c * piece
                        rows = slice(row0, row0 + piece)
                        x = load_prev(jp * sub_prev + r + c * piece * ratio, piece, ratio)
                        dst[bi, rows, :] = x.astype(BF16)
                        if 0 < bi < len(DILATIONS) - 1:
                            s_buf[bi % 2, rows, :] = x
            if d == 1:
                continue

            def load_prev(start, size, stride, half=bi % 2):
                return s_buf[half, pl.ds(start, size, stride=stride), :]
            d_prev = d

    ones = jnp.ones((kw, hd), BF16)

    for bi, d in enumerate(DILATIONS):
        sub = seq // d
        bps = sub // bq

        def place(n, d=d, sub=sub, bps=bps):
            j = n // bps
            t0 = (n % bps) * bq
            ks = jnp.clip(t0 - REACH, 0, sub - kw)
            qrows = pl.ds(pl.multiple_of(j * sub + t0, bq), bq)
            krows = pl.ds(pl.multiple_of(j * sub + ks, REACH), kw)
            if d > 1:
                orows = pl.ds(j + d * t0, bq, stride=d)
            else:
                orows = pl.ds(pl.multiple_of(t0, bq), bq)
            return qrows, krows, orows, (t0 - ks) // REACH

        def scores(n, carry, place=place, bi=bi):
            qrows, krows, _, edge = place(n)
            s = lax.dot_general(qp[bi, qrows, :], kp[bi, krows, :],
                                (((1,), (1,)), ((), ())), preferred_element_type=F32)
            s = s + bias_ref[edge]
            rows = pl.ds(pl.multiple_of(n * bq, bq), bq)
            s_buf[0, rows, :] = s[:, :hd]
            s_buf[1, rows, :] = s[:, hd:]
            return carry

        def softmax(n, carry, place=place, bi=bi):
            _, _, orows, _ = place(n)
            rows = pl.ds(pl.multiple_of(n * bq, bq), bq)
            s = jnp.concatenate([s_buf[0, rows, :], s_buf[1, rows, :]], axis=1)
            m = jnp.max(s, axis=-1, keepdims=True)
            p_buf[n] = jnp.exp2(s - m).astype(BF16)
            mb[bi, orows, :] = jnp.broadcast_to(m, (bq, hd))
            return carry

        def values(n, carry, place=place, bi=bi):
            _, krows, orows, _ = place(n)
            v1 = jnp.concatenate([vp[bi, krows, :], ones], axis=1)
            r = jnp.dot(p_buf[n], v1, preferred_element_type=F32)
            pv[bi, orows, :] = r[:, :hd]
            lb[bi, orows, :] = r[:, hd:]
            return carry

        lax.fori_loop(0, nblk, scores, 0, unroll=ATTN_UNROLL)
        lax.fori_loop(0, nblk, softmax, 0, unroll=ATTN_UNROLL)
        lax.fori_loop(0, nblk, values, 0, unroll=ATTN_UNROLL)

    def merge(c, carry):
        rows = pl.ds(pl.multiple_of(c * bq, bq), bq)
        branches = range(len(DILATIONS))
        ms = [mb[p, rows, :] for p in branches]
        top = functools.reduce(jnp.maximum, ms)
        es = [jnp.exp2(m - top) for m in ms]
        num = sum(es[p] * pv[p, rows, :] for p in branches)
        den = sum(es[p] * lb[p, rows, :] for p in branches)
        o_ref[0, rows, :] = num / den
        return carry

    lax.fori_loop(0, nblk, merge, 0, unroll=2)


def _attn_bias():
    r = jnp.arange(ATTN_BQ)[:, None]
    c = jnp.arange(ATTN_KW)[None, :]
    return jnp.stack([
        jnp.where(jnp.abs(edge * REACH + r - c) <= REACH, 0.0, MASK_VALUE).astype(F32)
        for edge in range(3)])


def _attn(q, k, v):
    batch, heads, seq, hd = q.shape
    nb = len(DILATIONS)
    nblk = seq // ATTN_BQ
    bias = _attn_bias()
    blk = pl.BlockSpec((1, 1, seq, hd), lambda b, h: (b, h, 0, 0))
    stat = pltpu.VMEM((nb, seq, hd), F32)
    gathered = pltpu.VMEM((nb, seq, hd), BF16)
    return pl.pallas_call(
        _attn_kernel,
        out_shape=jax.ShapeDtypeStruct((batch, seq, heads * hd), F32),
        grid=(batch, heads),
        in_specs=[blk, blk, blk, pl.BlockSpec(bias.shape, lambda b, h: (0, 0, 0))],
        out_specs=pl.BlockSpec((1, seq, hd), lambda b, h: (b, 0, h)),
        scratch_shapes=[
            gathered, gathered, gathered,
            pltpu.VMEM((ATTN_KW // hd, seq, hd), F32),
            pltpu.VMEM((nblk, ATTN_BQ, ATTN_KW), BF16),
            stat, stat, stat,
        ],
        compiler_params=_compiler_params(("parallel", "parallel")),
        name="attn",
    )(q, k, v, bias)


def _mixout_kernel(x_ref, yf_ref, ya_ref, gf_ref, ga_ref, gt_ref, w_ref, o_ref):
    df = yf_ref.shape[1]
    nf = (_rms(yf_ref[...]) * gf_ref[...]).astype(BF16)
    na = (_rms(ya_ref[...]) * ga_ref[...]).astype(BF16)
    out = (jnp.dot(nf, w_ref[:df, :], preferred_element_type=F32)
           + jnp.dot(na, w_ref[df:, :], preferred_element_type=F32))
    o_ref[...] = x_ref[...] + gt_ref[0] * out


def _mixout(x, yf, ya, gf, ga, gt, w_out, *, layer, seq, tm):
    t, d = x.shape
    df, da = yf.shape[1], ya.shape[1]
    tpb = seq // tm
    return pl.pallas_call(
        _mixout_kernel,
        out_shape=jax.ShapeDtypeStruct((t, d), F32),
        grid=(t // tm,),
        in_specs=[
            pl.BlockSpec((tm, d), lambda i: (i, 0)),
            pl.BlockSpec((tm, df), lambda i: (i, 0)),
            pl.BlockSpec((tm, da), lambda i: (i, 0)),
            pl.BlockSpec((1, df), lambda i: (0, 0)),
            pl.BlockSpec((1, da), lambda i: (0, 0)),
            pl.BlockSpec((1, 1, d), lambda i: (i // tpb, 0, 0)),
            pl.BlockSpec((None, df + da, d), lambda i: (layer, 0, 0)),
        ],
        out_specs=pl.BlockSpec((tm, d), lambda i: (i, 0)),
        compiler_params=_compiler_params(("parallel",)),
        name="mix_out",
    )(x, yf, ya, gf, ga, gt, w_out)


class _Tiles(NamedTuple):
    ffn_rows: int
    ffn_cols: int
    mix_rows: int
    fold_rows: int
    dft_rows: int
    dft_cols: int
    ada_cols: int


def _tiles(seq, d, d_ff):
    ffn_cols = 512 if d_ff % 512 == 0 else LANES
    return _Tiles(ffn_rows=min(512, seq), ffn_cols=ffn_cols, mix_rows=min(512, seq),
                  fold_rows=min(256, seq // DFT_RADIX),
                  dft_rows=min(1024, seq // DFT_RADIX), dft_cols=min(1024, d // 2), ada_cols=min(1024, d))


def kernel(x, c, w_ada, b_ada, w_ffn1_gu, w_ffn1_down, w_mix_in, w_fourier,
           g_fourier_out, g_attn_out, w_mix_out, w_ffn2_gu, w_ffn2_down, g_final):
    batch, seq, d = x.shape
    depth = w_ada.shape[0]
    d_ff = w_ffn1_down.shape[1]
    t = batch * seq
    tl = _tiles(seq, d, d_ff)

    rows = 8 * pl.cdiv(batch, 8)
    c_pad = jnp.zeros((rows, d), F32).at[:batch].set(c)
    mod = _ada(c_pad, w_ada, b_ada, tn=tl.ada_cols)
    mod = mod[:, :batch].reshape(depth, batch, N_MOD, 1, d)

    groups = w_fourier.shape[1]
    ab = _fprep(w_fourier.reshape(depth * groups, HEAD_DIM, HEAD_DIM), seq)
    ab = ab.reshape(depth, groups, HEAD_DIM, 2 * HEAD_DIM)
    rope_cos, rope_sin = _rope_tables(seq)
    dft_tables = _dft_tables(seq)
    g_fin = g_final.reshape(1, d)

    wgu1, wd1, wgu2, wd2, w_in, w_out = (
        w.astype(BF16) for w in (w_ffn1_gu, w_ffn1_down, w_ffn2_gu, w_ffn2_down,
                                 w_mix_in, w_mix_out))

    xt = x.reshape(t, d)
    for l in range(depth):
        sh1, sc1, g1, sh2, sc2, g2, sh3, sc3, g3 = (mod[l, :, i] for i in range(N_MOD))
        xt = _ffn(xt, sh1, sc1, g1, wgu1, wd1, g_fin, layer=l,
                  seq=seq, tm=tl.ffn_rows, tf=tl.ffn_cols, final_norm=False)
        pab, q, k, v = _mixin(xt, sh2, sc2, w_in, ab, rope_cos, rope_sin, layer=l,
                              batch=batch, seq=seq, tm=tl.mix_rows)
        w = _dft_fold(pab, batch=batch, seq=seq, tr=tl.fold_rows)
        yf = _dft(w, dft_tables, batch=batch, seq=seq, tm=tl.dft_rows, tn=tl.dft_cols)
        ya = _attn(q, k, v)
        xt = _mixout(xt, yf.reshape(t, -1), ya.reshape(t, -1),
                     g_fourier_out[l].reshape(1, -1), g_attn_out[l].reshape(1, -1),
                     g2, w_out, layer=l, seq=seq, tm=tl.mix_rows)
        xt = _ffn(xt, sh3, sc3, g3, wgu2, wd2, g_fin, layer=l,
                  seq=seq, tm=tl.ffn_rows, tf=tl.ffn_cols, final_norm=(l == depth - 1))
    return xt.reshape(batch, seq, d)
```

```python
import functools
import math
from typing import NamedTuple

import jax
import jax.numpy as jnp
from jax import lax
from jax.experimental import pallas as pl
from jax.experimental.pallas import tpu as pltpu

F32 = jnp.float32
BF16 = jnp.bfloat16

EPS = 1e-6
HEAD_DIM = 128
ROPE_THETA = 10000.0
DILATED_PATTERNS = ((128, 1), (512, 4), (2048, 16))
DILATIONS = tuple(d for _, d in DILATED_PATTERNS)
REACH = (DILATED_PATTERNS[0][0] // 2) // DILATED_PATTERNS[0][1]
assert all((w // 2) // d == REACH for w, d in DILATED_PATTERNS)
MASK_VALUE = -1e30
N_MOD = 9

LANES = 128
VMEM_LIMIT_BYTES = 56 * 1024 * 1024

FFN_COL_SPLIT = 2

ATTN_BQ = 128
ATTN_KW = ATTN_BQ + 2 * REACH
ATTN_COPY_ROWS = 128
ATTN_UNROLL = 16


def _compiler_params(semantics):
    return pltpu.CompilerParams(dimension_semantics=semantics,
                                vmem_limit_bytes=VMEM_LIMIT_BYTES)


def _rms(x):
    return x * lax.rsqrt(jnp.mean(x * x, axis=-1, keepdims=True) + EPS)


def _silu(x):
    return x * jax.nn.sigmoid(x)


def _ada_kernel(c_ref, w_ref, b_ref, o_ref):
    ca = _silu(c_ref[...]).astype(BF16)
    o_ref[0] = jnp.dot(ca, w_ref[0].astype(BF16),
                       preferred_element_type=F32) + b_ref[0]


def _ada(c_pad, w_ada, b_ada, *, tn):
    depth, d, n = w_ada.shape
    rows = c_pad.shape[0]
    return pl.pallas_call(
        _ada_kernel,
        out_shape=jax.ShapeDtypeStruct((depth, rows, n), F32),
        grid=(depth, n // tn),
        in_specs=[
            pl.BlockSpec((rows, d), lambda l, j: (0, 0)),
            pl.BlockSpec((1, d, tn), lambda l, j: (l, 0, j)),
            pl.BlockSpec((1, 1, tn), lambda l, j: (l, 0, j)),
        ],
        out_specs=pl.BlockSpec((1, rows, tn), lambda l, j: (l, 0, j)),
        compiler_params=_compiler_params(("parallel", "parallel")),
        name="ada",
    )(c_pad, w_ada, b_ada.reshape(depth, 1, n))


def _ffn_kernel(x_ref, sh_ref, sc_ref, gt_ref, wg_ref, wu_ref, wd_ref, gf_ref,
                o_ref, h_ref, *, final_norm):
    j = pl.program_id(1)

    @pl.when(j == 0)
    def _():
        h = _rms(x_ref[...]) * (1.0 + sc_ref[0]) + sh_ref[0]
        h_ref[...] = h.astype(BF16)
        o_ref[...] = jnp.zeros_like(o_ref)

    h = h_ref[...]
    tf = wd_ref.shape[0]
    part = tf // FFN_COL_SPLIT
    gu = []
    for s in range(FFN_COL_SPLIT):
        cols = slice(s * part, (s + 1) * part)
        gu.append((jnp.dot(h, wg_ref[:, cols], preferred_element_type=F32),
                   jnp.dot(h, wu_ref[:, cols], preferred_element_type=F32)))
    for s, (g, u) in enumerate(gu):
        a = (_silu(g) * u).astype(BF16)
        o_ref[...] += jnp.dot(a, wd_ref[s * part:(s + 1) * part, :],
                              preferred_element_type=F32)

    @pl.when(j == pl.num_programs(1) - 1)
    def _():
        y = x_ref[...] + (0.5 * gt_ref[0]) * o_ref[...]
        if final_norm:
            y = _rms(y) * gf_ref[...]
        o_ref[...] = y


def _ffn(x, sh, sc, gt, w_gu, w_down, g_final, *, layer, seq, tm, tf, final_norm):
    t, d = x.shape
    f = w_down.shape[1]
    nj = f // tf
    tpb = seq // tm
    vec = pl.BlockSpec((1, 1, d), lambda i, j: (i // tpb, 0, 0))
    return pl.pallas_call(
        functools.partial(_ffn_kernel, final_norm=final_norm),
        out_shape=jax.ShapeDtypeStruct((t, d), F32),
        grid=(t // tm, nj),
        in_specs=[
            pl.BlockSpec((tm, d), lambda i, j: (i, 0)),
            vec, vec, vec,
            pl.BlockSpec((None, d, tf), lambda i, j: (layer, 0, j)),
            pl.BlockSpec((None, d, tf), lambda i, j: (layer, 0, j + nj)),
            pl.BlockSpec((None, tf, d), lambda i, j: (layer, j, 0)),
            pl.BlockSpec((1, d), lambda i, j: (0, 0)),
        ],
        out_specs=pl.BlockSpec((tm, d), lambda i, j: (i, 0)),
        scratch_shapes=[pltpu.VMEM((tm, d), BF16)],
        compiler_params=_compiler_params(("parallel", "arbitrary")),
        name="ffn",
    )(x, sh, sc, gt, w_gu, w_gu, w_down, g_final)


def _fprep_kernel(cc_ref, sc_ref, w_ref, ab_ref, *, scale):
    w = w_ref[0]
    hc = w.shape[0]
    a = jnp.dot(cc_ref[...], w, preferred_element_type=F32,
                precision=lax.Precision.HIGHEST)
    b = jnp.dot(sc_ref[...], w, preferred_element_type=F32,
                precision=lax.Precision.HIGHEST)
    ab_ref[0, :, :hc] = (a * scale).astype(BF16)
    ab_ref[0, :, hc:] = (b * scale).astype(BF16)


def _fprep(w_fourier, seq):
    n, hc, _ = w_fourier.shape
    ang = _dft_angles(hc, jnp.arange(hc), jnp.arange(hc))
    scale = 1.0 / math.sqrt(seq * hc)
    sq = pl.BlockSpec((hc, hc), lambda g: (0, 0))
    return pl.pallas_call(
        functools.partial(_fprep_kernel, scale=scale),
        out_shape=jax.ShapeDtypeStruct((n, hc, 2 * hc), BF16),
        grid=(n,),
        in_specs=[sq, sq, pl.BlockSpec((1, hc, hc), lambda g: (g, 0, 0))],
        out_specs=pl.BlockSpec((1, hc, 2 * hc), lambda g: (g, 0, 0)),
        compiler_params=_compiler_params(("parallel",)),
        name="fprep",
    )(jnp.cos(ang), jnp.sin(ang), w_fourier)


def _dft_angles(n, rows, cols):
    idx = (rows[:, None] * cols[None, :]) % n
    return idx.astype(F32) * (2.0 * math.pi / n)


def _mixin_kernel(x_ref, sh_ref, sc_ref, w_ref, ab_ref, cos_ref, sin_ref,
                  pab_ref, q_ref, k_ref, v_ref, h_ref, *, q_scale):
    n = pl.program_id(1)
    hc = HEAD_DIM
    groups = pab_ref.shape[1] // (2 * hc)
    heads = q_ref.shape[1]

    @pl.when(n == 0)
    def _():
        h = _rms(x_ref[...]) * (1.0 + sc_ref[0]) + sh_ref[0]
        h_ref[...] = h.astype(BF16)

    p = jnp.dot(h_ref[...], w_ref[...], preferred_element_type=F32)

    @pl.when(n == 0)
    def _():
        pb = p.astype(BF16)
        for g in range(groups):
            r = jnp.dot(pb[:, g * hc:(g + 1) * hc], ab_ref[g],
                        preferred_element_type=F32)
            pab_ref[:, g * hc:(g + 1) * hc] = r[:, :hc].astype(BF16)
            pab_ref[:, (groups + g) * hc:(groups + g + 1) * hc] = (
                r[:, hc:].astype(BF16))

    def rope_to(dst_ref, scale):
        cos = cos_ref[...] * scale
        sin = sin_ref[...] * scale
        for hh in range(heads):
            t = p[:, hh * hc:(hh + 1) * hc]
            dst_ref[0, hh] = t * cos + pltpu.roll(t, hc // 2, 1) * sin

    @pl.when(n == 1)
    def _():
        rope_to(q_ref, q_scale)

    @pl.when(n == 2)
    def _():
        rope_to(k_ref, 1.0)

    @pl.when(n == 3)
    def _():
        for hh in range(heads):
            v_ref[0, hh] = p[:, hh * hc:(hh + 1) * hc]


def _mixin(x, sh, sc, w_in, ab, cos, sin, *, layer, batch, seq, tm):
    t, d = x.shape
    dm = w_in.shape[2] // 4
    heads = dm // HEAD_DIM
    tpb = seq // tm
    vec = pl.BlockSpec((1, 1, d), lambda i, n: (i // tpb, 0, 0))
    tab = pl.BlockSpec((tm, HEAD_DIM), lambda i, n: (i % tpb, 0))
    hm = pl.BlockSpec((1, heads, tm, HEAD_DIM),
                      lambda i, n: (i // tpb, 0, i % tpb, 0))
    hm_shape = jax.ShapeDtypeStruct((batch, heads, seq, HEAD_DIM), F32)
    return pl.pallas_call(
        functools.partial(_mixin_kernel, q_scale=HEAD_DIM ** -0.5 * math.log2(math.e)),
        out_shape=(jax.ShapeDtypeStruct((t, 2 * dm), BF16),
                   hm_shape, hm_shape, hm_shape),
        grid=(t // tm, 4),
        in_specs=[
            pl.BlockSpec((tm, d), lambda i, n: (i, 0)),
            vec, vec,
            pl.BlockSpec((None, d, dm), lambda i, n: (layer, 0, n)),
            pl.BlockSpec((None,) + ab.shape[1:], lambda i, n: (layer, 0, 0, 0)),
            tab, tab,
        ],
        out_specs=(pl.BlockSpec((tm, 2 * dm), lambda i, n: (i, 0)),
                   hm, hm, hm),
        scratch_shapes=[pltpu.VMEM((tm, d), BF16)],
        compiler_params=_compiler_params(("parallel", "arbitrary")),
        name="mix_in",
    )(x, sh, sc, w_in, ab, cos, sin)


def _rope_tables(seq):
    half = HEAD_DIM // 2
    inv_freq = ROPE_THETA ** (-jnp.arange(half, dtype=F32) / half)
    ang = jnp.arange(seq, dtype=F32)[:, None] * inv_freq[None, :]
    cos, sin = jnp.cos(ang), jnp.sin(ang)
    return (jnp.concatenate([cos, cos], axis=-1),
            jnp.concatenate([-sin, sin], axis=-1))


DFT_RADIX = 4


def _dft_fold_kernel(z0_ref, z1_ref, z2_ref, z3_ref, w_ref):
    dm = w_ref.shape[3] // 2
    a = [z[0, 0, :, :dm].astype(F32) for z in (z0_ref, z1_ref, z2_ref, z3_ref)]
    p = [z[0, 0, :, dm:].astype(F32) for z in (z0_ref, z1_ref, z2_ref, z3_ref)]
    t0r, t0i = a[0] + a[2], -(p[0] + p[2])
    t1r, t1i = a[0] - a[2], p[2] - p[0]
    t2r, t2i = a[1] + a[3], -(p[1] + p[3])
    t3r, t3i = a[1] - a[3], p[3] - p[1]
    parts = ((t0r + t2r, t0i + t2i),
             (t1r + t3i, t1i - t3r),
             (t0r - t2r, t0i - t2i),
             (t1r - t3i, t1i + t3r))
    for r, (re, im) in enumerate(parts):
        w_ref[0, r, :, :dm] = re.astype(BF16)
        w_ref[0, r, :, dm:] = im.astype(BF16)


def _dft_fold(pab, *, batch, seq, tr):
    width = pab.shape[-1]
    nq = seq // DFT_RADIX
    z = pab.reshape(batch, DFT_RADIX, nq, width)
    quarter = [pl.BlockSpec((1, 1, tr, width), lambda b, i, q=q: (b, q, i, 0))
               for q in range(DFT_RADIX)]
    return pl.pallas_call(
        _dft_fold_kernel,
        out_shape=jax.ShapeDtypeStruct((batch, DFT_RADIX, nq, width), BF16),
        grid=(batch, nq // tr),
        in_specs=quarter,
        out_specs=pl.BlockSpec((1, DFT_RADIX, tr, width), lambda b, i: (b, 0, i, 0)),
        compiler_params=_compiler_params(("parallel", "parallel")),
        name="dft_fold",
    )(z, z, z, z)


def _dft_kernel(ca_ref, sa_ref, cb_ref, sb_ref, wre_ref, wim_ref, o_ref,
                cmat, smat, stage):
    r = pl.program_id(3)
    first = jnp.logical_and(pl.program_id(1) == 0, pl.program_id(2) == 0)

    @pl.when(first)
    def _():
        cb = cb_ref[0]
        sb = sb_ref[0]
        for a in range(ca_ref.shape[2]):
            ca = ca_ref[0, :, a:a + 1]
            sa = sa_ref[0, :, a:a + 1]
            cols = slice(a * LANES, (a + 1) * LANES)
            cmat[r, :, cols] = (ca * cb - sa * sb).astype(BF16)
            smat[r, :, cols] = (sa * cb + ca * sb).astype(BF16)

    y = (jnp.dot(cmat[r], wre_ref[0, 0], preferred_element_type=F32)
         + jnp.dot(smat[r], wim_ref[0, 0], preferred_element_type=F32))
    for cc in range(stage.shape[0]):
        stage[cc, pl.ds(r, y.shape[0], stride=DFT_RADIX), :] = (
            y[:, cc * LANES:(cc + 1) * LANES])

    @pl.when(r == DFT_RADIX - 1)
    def _():
        for cc in range(stage.shape[0]):
            o_ref[0, :, cc * LANES:(cc + 1) * LANES] = stage[cc]


def _dft(w, tables, *, batch, seq, tm, tn):
    dm = w.shape[-1] // 2
    nq = seq // DFT_RADIX
    ncb = dm // tn
    ca, sa, cb, sb = tables
    coarse = pl.BlockSpec((1, tm, ca.shape[2]), lambda i, b, c, r: (r, i, 0))
    fine = pl.BlockSpec((1, tm, LANES), lambda i, b, c, r: (r, i, 0))
    twiddle = pltpu.VMEM((DFT_RADIX, tm, nq), BF16)
    return pl.pallas_call(
        _dft_kernel,
        out_shape=jax.ShapeDtypeStruct((batch, seq, dm), F32),
        grid=(nq // tm, batch, ncb, DFT_RADIX),
        in_specs=[
            coarse, coarse, fine, fine,
            pl.BlockSpec((1, 1, nq, tn), lambda i, b, c, r: (b, r, 0, c)),
            pl.BlockSpec((1, 1, nq, tn), lambda i, b, c, r: (b, r, 0, c + ncb)),
        ],
        out_specs=pl.BlockSpec((1, DFT_RADIX * tm, tn), lambda i, b, c, r: (b, i, c)),
        scratch_shapes=[twiddle, twiddle,
                        pltpu.VMEM((tn // LANES, DFT_RADIX * tm, LANES), F32)],
        compiler_params=_compiler_params(
            ("parallel", "arbitrary", "arbitrary", "arbitrary")),
        name="dft",
    )(ca, sa, cb, sb, w, w)


def _dft_tables(seq):
    nq = seq // DFT_RADIX
    k = (DFT_RADIX * jnp.arange(nq)[None, :] + jnp.arange(DFT_RADIX)[:, None]).reshape(-1)
    coarse = _dft_angles(seq, k, LANES * jnp.arange(nq // LANES))
    fine = _dft_angles(seq, k, jnp.arange(LANES))
    return tuple(f(t).reshape(DFT_RADIX, nq, -1)
                 for t in (coarse, fine) for f in (jnp.cos, jnp.sin))


def _attn_kernel(q_ref, k_ref, v_ref, bias_ref, o_ref,
                 qp, kp, vp, s_buf, p_buf, pv, mb, lb):
    seq, hd = q_ref.shape[2], q_ref.shape[3]
    bq, kw = ATTN_BQ, ATTN_KW
    nblk = seq // bq

    assert kw == 2 * hd and s_buf.shape == (2, seq, hd)
    for src_ref, dst in ((q_ref, qp), (k_ref, kp), (v_ref, vp)):
        def load_prev(start, size, stride, src_ref=src_ref):
            return src_ref[0, 0, pl.ds(start, size, stride=stride), :]
        d_prev = 1
        for bi, d in enumerate(DILATIONS):
            ratio, sub, sub_prev = d // d_prev, seq // d, seq // d_prev
            piece = min(sub, ATTN_COPY_ROWS)
            for jp in range(d_prev):
                for r in range(ratio):
                    for c in range(sub // piece):
                        row0 = (jp + d_prev * r) * sub + c * piece
                        rows = slice(row0, row0 + piece)
                        x = load_prev(jp * sub_prev + r + c * piece * ratio, piece, ratio)
                        dst[bi, rows, :] = x.astype(BF16)
                        if 0 < bi < len(DILATIONS) - 1:
                            s_buf[bi % 2, rows, :] = x
            if d == 1:
                continue

            def load_prev(start, size, stride, half=bi % 2):
                return s_buf[half, pl.ds(start, size, stride=stride), :]
            d_prev = d

    ones = jnp.ones((kw, hd), BF16)

    for bi, d in enumerate(DILATIONS):
        sub = seq // d
        bps = sub // bq

        def place(n, d=d, sub=sub, bps=bps):
            j = n // bps
            t0 = (n % bps) * bq
            ks = jnp.clip(t0 - REACH, 0, sub - kw)
            qrows = pl.ds(pl.multiple_of(j * sub + t0, bq), bq)
            krows = pl.ds(pl.multiple_of(j * sub + ks, REACH), kw)
            if d > 1:
                orows = pl.ds(j + d * t0, bq, stride=d)
            else:
                orows = pl.ds(pl.multiple_of(t0, bq), bq)
            return qrows, krows, orows, (t0 - ks) // REACH

        def scores(n, carry, place=place, bi=bi):
            qrows, krows, _, edge = place(n)
            s = lax.dot_general(qp[bi, qrows, :], kp[bi, krows, :],
                                (((1,), (1,)), ((), ())), preferred_element_type=F32)
            s = s + bias_ref[edge]
            rows = pl.ds(pl.multiple_of(n * bq, bq), bq)
            s_buf[0, rows, :] = s[:, :hd]
            s_buf[1, rows, :] = s[:, hd:]
            return carry

        def softmax(n, carry, place=place, bi=bi):
            _, _, orows, _ = place(n)
            rows = pl.ds(pl.multiple_of(n * bq, bq), bq)
            s = jnp.concatenate([s_buf[0, rows, :], s_buf[1, rows, :]], axis=1)
            m = jnp.max(s, axis=-1, keepdims=True)
            p_buf[n] = jnp.exp2(s - m).astype(BF16)
            mb[bi, orows, :] = jnp.broadcast_to(m, (bq, hd))
            return carry

        def values(n, carry, place=place, bi=bi):
            _, krows, orows, _ = place(n)
            v1 = jnp.concatenate([vp[bi, krows, :], ones], axis=1)
            r = jnp.dot(p_buf[n], v1, preferred_element_type=F32)
            pv[bi, orows, :] = r[:, :hd]
            lb[bi, orows, :] = r[:, hd:]
            return carry

        lax.fori_loop(0, nblk, scores, 0, unroll=ATTN_UNROLL)
        lax.fori_loop(0, nblk, softmax, 0, unroll=ATTN_UNROLL)
        lax.fori_loop(0, nblk, values, 0, unroll=ATTN_UNROLL)

    def merge(c, carry):
        rows = pl.ds(pl.multiple_of(c * bq, bq), bq)
        branches = range(len(DILATIONS))
        ms = [mb[p, rows, :] for p in branches]
        top = functools.reduce(jnp.maximum, ms)
        es = [jnp.exp2(m - top) for m in ms]
        num = sum(es[p] * pv[p, rows, :] for p in branches)
        den = sum(es[p] * lb[p, rows, :] for p in branches)
        o_ref[0, rows, :] = num / den
        return carry

    lax.fori_loop(0, nblk, merge, 0, unroll=2)


def _attn_bias():
    r = jnp.arange(ATTN_BQ)[:, None]
    c = jnp.arange(ATTN_KW)[None, :]
    return jnp.stack([
        jnp.where(jnp.abs(edge * REACH + r - c) <= REACH, 0.0, MASK_VALUE).astype(F32)
        for edge in range(3)])


def _attn(q, k, v):
    batch, heads, seq, hd = q.shape
    nb = len(DILATIONS)
    nblk = seq // ATTN_BQ
    bias = _attn_bias()
    blk = pl.BlockSpec((1, 1, seq, hd), lambda b, h: (b, h, 0, 0))
    stat = pltpu.VMEM((nb, seq, hd), F32)
    gathered = pltpu.VMEM((nb, seq, hd), BF16)
    return pl.pallas_call(
        _attn_kernel,
        out_shape=jax.ShapeDtypeStruct((batch, seq, heads * hd), F32),
        grid=(batch, heads),
        in_specs=[blk, blk, blk, pl.BlockSpec(bias.shape, lambda b, h: (0, 0, 0))],
        out_specs=pl.BlockSpec((1, seq, hd), lambda b, h: (b, 0, h)),
        scratch_shapes=[
            gathered, gathered, gathered,
            pltpu.VMEM((ATTN_KW // hd, seq, hd), F32),
            pltpu.VMEM((nblk, ATTN_BQ, ATTN_KW), BF16),
            stat, stat, stat,
        ],
        compiler_params=_compiler_params(("parallel", "parallel")),
        name="attn",
    )(q, k, v, bias)


def _mixout_kernel(x_ref, yf_ref, ya_ref, gf_ref, ga_ref, gt_ref, w_ref, o_ref):
    df = yf_ref.shape[1]
    nf = (_rms(yf_ref[...]) * gf_ref[...]).astype(BF16)
    na = (_rms(ya_ref[...]) * ga_ref[...]).astype(BF16)
    out = (jnp.dot(nf, w_ref[:df, :], preferred_element_type=F32)
           + jnp.dot(na, w_ref[df:, :], preferred_element_type=F32))
    o_ref[...] = x_ref[...] + gt_ref[0] * out


def _mixout(x, yf, ya, gf, ga, gt, w_out, *, layer, seq, tm):
    t, d = x.shape
    df, da = yf.shape[1], ya.shape[1]
    tpb = seq // tm
    return pl.pallas_call(
        _mixout_kernel,
        out_shape=jax.ShapeDtypeStruct((t, d), F32),
        grid=(t // tm,),
        in_specs=[
            pl.BlockSpec((tm, d), lambda i: (i, 0)),
            pl.BlockSpec((tm, df), lambda i: (i, 0)),
            pl.BlockSpec((tm, da), lambda i: (i, 0)),
            pl.BlockSpec((1, df), lambda i: (0, 0)),
            pl.BlockSpec((1, da), lambda i: (0, 0)),
            pl.BlockSpec((1, 1, d), lambda i: (i // tpb, 0, 0)),
            pl.BlockSpec((None, df + da, d), lambda i: (layer, 0, 0)),
        ],
        out_specs=pl.BlockSpec((tm, d), lambda i: (i, 0)),
        compiler_params=_compiler_params(("parallel",)),
        name="mix_out",
    )(x, yf, ya, gf, ga, gt, w_out)


class _Tiles(NamedTuple):
    ffn_rows: int
    ffn_cols: int
    mix_rows: int
    fold_rows: int
    dft_rows: int
    dft_cols: int
    ada_cols: int


def _tiles(seq, d, d_ff):
    ffn_cols = 512 if d_ff % 512 == 0 else LANES
    return _Tiles(ffn_rows=min(512, seq), ffn_cols=ffn_cols, mix_rows=min(512, seq),
                  fold_rows=min(256, seq // DFT_RADIX),
                  dft_rows=min(512, seq // DFT_RADIX), dft_cols=min(1024, d // 2), ada_cols=min(1024, d))


def kernel(x, c, w_ada, b_ada, w_ffn1_gu, w_ffn1_down, w_mix_in, w_fourier,
           g_fourier_out, g_attn_out, w_mix_out, w_ffn2_gu, w_ffn2_down, g_final):
    batch, seq, d = x.shape
    depth = w_ada.shape[0]
    d_ff = w_ffn1_down.shape[1]
    t = batch * seq
    tl = _tiles(seq, d, d_ff)

    rows = 8 * pl.cdiv(batch, 8)
    c_pad = jnp.zeros((rows, d), F32).at[:batch].set(c)
    mod = _ada(c_pad, w_ada, b_ada, tn=tl.ada_cols)
    mod = mod[:, :batch].reshape(depth, batch, N_MOD, 1, d)

    groups = w_fourier.shape[1]
    ab = _fprep(w_fourier.reshape(depth * groups, HEAD_DIM, HEAD_DIM), seq)
    ab = ab.reshape(depth, groups, HEAD_DIM, 2 * HEAD_DIM)
    rope_cos, rope_sin = _rope_tables(seq)
    dft_tables = _dft_tables(seq)
    g_fin = g_final.reshape(1, d)

    wgu1, wd1, wgu2, wd2, w_in, w_out = (
        w.astype(BF16) for w in (w_ffn1_gu, w_ffn1_down, w_ffn2_gu, w_ffn2_down,
                                 w_mix_in, w_mix_out))

    xt = x.reshape(t, d)
    for l in range(depth):
        sh1, sc1, g1, sh2, sc2, g2, sh3, sc3, g3 = (mod[l, :, i] for i in range(N_MOD))
        xt = _ffn(xt, sh1, sc1, g1, wgu1, wd1, g_fin, layer=l,
                  seq=seq, tm=tl.ffn_rows, tf=tl.ffn_cols, final_norm=False)
        pab, q, k, v = _mixin(xt, sh2, sc2, w_in, ab, rope_cos, rope_sin, layer=l,
                              batch=batch, seq=seq, tm=tl.mix_rows)
        w = _dft_fold(pab, batch=batch, seq=seq, tr=tl.fold_rows)
        yf = _dft(w, dft_tables, batch=batch, seq=seq, tm=tl.dft_rows, tn=tl.dft_cols)
        ya = _attn(q, k, v)
        xt = _mixout(xt, yf.reshape(t, -1), ya.reshape(t, -1),
                     g_fourier_out[l].reshape(1, -1), g_attn_out[l].reshape(1, -1),
                     g2, w_out, layer=l, seq=seq, tm=tl.mix_rows)
        xt = _ffn(xt, sh3, sc3, g3, wgu2, wd2, g_fin, layer=l,
                  seq=seq, tm=tl.ffn_rows, tf=tl.ffn_cols, final_norm=(l == depth - 1))
    return xt.reshape(batch, seq, d)
```

```python
import functools
import math
from typing import NamedTuple

import jax
import jax.numpy as jnp
from jax import lax
from jax.experimental import pallas as pl
from jax.experimental.pallas import tpu as pltpu

F32 = jnp.float32
BF16 = jnp.bfloat16

EPS = 1e-6
HEAD_DIM = 128
ROPE_THETA = 10000.0
DILATED_PATTERNS = ((128, 1), (512, 4), (2048, 16))
DILATIONS = tuple(d for _, d in DILATED_PATTERNS)
REACH = (DILATED_PATTERNS[0][0] // 2) // DILATED_PATTERNS[0][1]
assert all((w // 2) // d == REACH for w, d in DILATED_PATTERNS)
MASK_VALUE = -1e30
N_MOD = 9

LANES = 128
VMEM_LIMIT_BYTES = 56 * 1024 * 1024

FFN_COL_SPLIT = 2

ATTN_BQ = 128
ATTN_KW = ATTN_BQ + 2 * REACH
ATTN_COPY_ROWS = 128
ATTN_UNROLL = 16


def _compiler_params(semantics):
    return pltpu.CompilerParams(dimension_semantics=semantics,
                                vmem_limit_bytes=VMEM_LIMIT_BYTES)


def _rms(x):
    return x * lax.rsqrt(jnp.mean(x * x, axis=-1, keepdims=True) + EPS)


def _silu(x):
    return x * jax.nn.sigmoid(x)


def _ada_kernel(c_ref, w_ref, b_ref, o_ref):
    ca = _silu(c_ref[...]).astype(BF16)
    o_ref[0] = jnp.dot(ca, w_ref[0].astype(BF16),
                       preferred_element_type=F32) + b_ref[0]


def _ada(c_pad, w_ada, b_ada, *, tn):
    depth, d, n = w_ada.shape
    rows = c_pad.shape[0]
    return pl.pallas_call(
        _ada_kernel,
        out_shape=jax.ShapeDtypeStruct((depth, rows, n), F32),
        grid=(depth, n // tn),
        in_specs=[
            pl.BlockSpec((rows, d), lambda l, j: (0, 0)),
            pl.BlockSpec((1, d, tn), lambda l, j: (l, 0, j)),
            pl.BlockSpec((1, 1, tn), lambda l, j: (l, 0, j)),
        ],
        out_specs=pl.BlockSpec((1, rows, tn), lambda l, j: (l, 0, j)),
        compiler_params=_compiler_params(("parallel", "parallel")),
        name="ada",
    )(c_pad, w_ada, b_ada.reshape(depth, 1, n))


def _ffn_kernel(x_ref, sh_ref, sc_ref, gt_ref, wg_ref, wu_ref, wd_ref, gf_ref,
                o_ref, h_ref, *, final_norm):
    j = pl.program_id(1)

    @pl.when(j == 0)
    def _():
        h = _rms(x_ref[...]) * (1.0 + sc_ref[0]) + sh_ref[0]
        h_ref[...] = h.astype(BF16)
        o_ref[...] = jnp.zeros_like(o_ref)

    h = h_ref[...]
    tf = wd_ref.shape[0]
    part = tf // FFN_COL_SPLIT
    gu = []
    for s in range(FFN_COL_SPLIT):
        cols = slice(s * part, (s + 1) * part)
        gu.append((jnp.dot(h, wg_ref[:, cols], preferred_element_type=F32),
                   jnp.dot(h, wu_ref[:, cols], preferred_element_type=F32)))
    for s, (g, u) in enumerate(gu):
        a = (_silu(g) * u).astype(BF16)
        o_ref[...] += jnp.dot(a, wd_ref[s * part:(s + 1) * part, :],
                              preferred_element_type=F32)

    @pl.when(j == pl.num_programs(1) - 1)
    def _():
        y = x_ref[...] + (0.5 * gt_ref[0]) * o_ref[...]
        if final_norm:
            y = _rms(y) * gf_ref[...]
        o_ref[...] = y


def _ffn(x, sh, sc, gt, w_gu, w_down, g_final, *, layer, seq, tm, tf, final_norm):
    t, d = x.shape
    f = w_down.shape[1]
    nj = f // tf
    tpb = seq // tm
    vec = pl.BlockSpec((1, 1, d), lambda i, j: (i // tpb, 0, 0))
    return pl.pallas_call(
        functools.partial(_ffn_kernel, final_norm=final_norm),
        out_shape=jax.ShapeDtypeStruct((t, d), F32),
        grid=(t // tm, nj),
        in_specs=[
            pl.BlockSpec((tm, d), lambda i, j: (i, 0)),
            vec, vec, vec,
            pl.BlockSpec((None, d, tf), lambda i, j: (layer, 0, j)),
            pl.BlockSpec((None, d, tf), lambda i, j: (layer, 0, j + nj)),
            pl.BlockSpec((None, tf, d), lambda i, j: (layer, j, 0)),
            pl.BlockSpec((1, d), lambda i, j: (0, 0)),
        ],
        out_specs=pl.BlockSpec((tm, d), lambda i, j: (i, 0)),
        scratch_shapes=[pltpu.VMEM((tm, d), BF16)],
        compiler_params=_compiler_params(("parallel", "arbitrary")),
        name="ffn",
    )(x, sh, sc, gt, w_gu, w_gu, w_down, g_final)


def _fprep_kernel(cc_ref, sc_ref, w_ref, ab_ref, *, scale):
    w = w_ref[0]
    hc = w.shape[0]
    a = jnp.dot(cc_ref[...], w, preferred_element_type=F32,
                precision=lax.Precision.HIGHEST)
    b = jnp.dot(sc_ref[...], w, preferred_element_type=F32,
                precision=lax.Precision.HIGHEST)
    ab_ref[0, :, :hc] = (a * scale).astype(BF16)
    ab_ref[0, :, hc:] = (b * scale).astype(BF16)


def _fprep(w_fourier, seq):
    n, hc, _ = w_fourier.shape
    ang = _dft_angles(hc, jnp.arange(hc), jnp.arange(hc))
    scale = 1.0 / math.sqrt(seq * hc)
    sq = pl.BlockSpec((hc, hc), lambda g: (0, 0))
    return pl.pallas_call(
        functools.partial(_fprep_kernel, scale=scale),
        out_shape=jax.ShapeDtypeStruct((n, hc, 2 * hc), BF16),
        grid=(n,),
        in_specs=[sq, sq, pl.BlockSpec((1, hc, hc), lambda g: (g, 0, 0))],
        out_specs=pl.BlockSpec((1, hc, 2 * hc), lambda g: (g, 0, 0)),
        compiler_params=_compiler_params(("parallel",)),
        name="fprep",
    )(jnp.cos(ang), jnp.sin(ang), w_fourier)


def _dft_angles(n, rows, cols):
    idx = (rows[:, None] * cols[None, :]) % n
    return idx.astype(F32) * (2.0 * math.pi / n)


def _mixin_kernel(x_ref, sh_ref, sc_ref, w_ref, ab_ref, cos_ref, sin_ref,
                  pab_ref, q_ref, k_ref, v_ref, *, q_scale):
    hc = HEAD_DIM
    dm = w_ref.shape[1] // 4
    groups = heads = dm // hc
    h = (_rms(x_ref[...]) * (1.0 + sc_ref[0]) + sh_ref[0]).astype(BF16)

    def project(n):
        return jnp.dot(h, w_ref[:, n * dm:(n + 1) * dm], preferred_element_type=F32)

    def rope_to(dst_ref, p, scale):
        cos = cos_ref[...] * scale
        sin = sin_ref[...] * scale
        for hh in range(heads):
            t = p[:, hh * hc:(hh + 1) * hc]
            dst_ref[0, hh] = t * cos + pltpu.roll(t, hc // 2, 1) * sin

    u = project(0).astype(BF16)
    pq = project(1)
    for g in range(groups):
        r = jnp.dot(u[:, g * hc:(g + 1) * hc], ab_ref[g], preferred_element_type=F32)
        pab_ref[:, g * hc:(g + 1) * hc] = r[:, :hc].astype(BF16)
        pab_ref[:, (groups + g) * hc:(groups + g + 1) * hc] = r[:, hc:].astype(BF16)
    rope_to(q_ref, pq, q_scale)
    rope_to(k_ref, project(2), 1.0)
    pv = project(3)
    for hh in range(heads):
        v_ref[0, hh] = pv[:, hh * hc:(hh + 1) * hc]


def _mixin(x, sh, sc, w_in, ab, cos, sin, *, layer, batch, seq, tm):
    t, d = x.shape
    dm = w_in.shape[2] // 4
    heads = dm // HEAD_DIM
    tpb = seq // tm
    vec = pl.BlockSpec((1, 1, d), lambda i: (i // tpb, 0, 0))
    tab = pl.BlockSpec((tm, HEAD_DIM), lambda i: (i % tpb, 0))
    hm = pl.BlockSpec((1, heads, tm, HEAD_DIM), lambda i: (i // tpb, 0, i % tpb, 0))
    hm_shape = jax.ShapeDtypeStruct((batch, heads, seq, HEAD_DIM), F32)
    resident = pl.Buffered(1)
    return pl.pallas_call(
        functools.partial(_mixin_kernel, q_scale=HEAD_DIM ** -0.5 * math.log2(math.e)),
        out_shape=(jax.ShapeDtypeStruct((t, 2 * dm), BF16),
                   hm_shape, hm_shape, hm_shape),
        grid=(t // tm,),
        in_specs=[
            pl.BlockSpec((tm, d), lambda i: (i, 0)),
            vec, vec,
            pl.BlockSpec((None, d, 4 * dm), lambda i: (layer, 0, 0), pipeline_mode=resident),
            pl.BlockSpec((None,) + ab.shape[1:], lambda i: (layer, 0, 0, 0),
                         pipeline_mode=resident),
            tab, tab,
        ],
        out_specs=(pl.BlockSpec((tm, 2 * dm), lambda i: (i, 0)), hm, hm, hm),
        compiler_params=_compiler_params(("parallel",)),
        name="mix_in",
    )(x, sh, sc, w_in, ab, cos, sin)


def _rope_tables(seq):
    half = HEAD_DIM // 2
    inv_freq = ROPE_THETA ** (-jnp.arange(half, dtype=F32) / half)
    ang = jnp.arange(seq, dtype=F32)[:, None] * inv_freq[None, :]
    cos, sin = jnp.cos(ang), jnp.sin(ang)
    return (jnp.concatenate([cos, cos], axis=-1),
            jnp.concatenate([-sin, sin], axis=-1))


DFT_RADIX = 4


def _dft_fold_kernel(z0_ref, z1_ref, z2_ref, z3_ref, w_ref):
    dm = w_ref.shape[3] // 2
    a = [z[0, 0, :, :dm].astype(F32) for z in (z0_ref, z1_ref, z2_ref, z3_ref)]
    p = [z[0, 0, :, dm:].astype(F32) for z in (z0_ref, z1_ref, z2_ref, z3_ref)]
    t0r, t0i = a[0] + a[2], -(p[0] + p[2])
    t1r, t1i = a[0] - a[2], p[2] - p[0]
    t2r, t2i = a[1] + a[3], -(p[1] + p[3])
    t3r, t3i = a[1] - a[3], p[3] - p[1]
    parts = ((t0r + t2r, t0i + t2i),
             (t1r + t3i, t1i - t3r),
             (t0r - t2r, t0i - t2i),
             (t1r - t3i, t1i + t3r))
    for r, (re, im) in enumerate(parts):
        w_ref[0, r, :, :dm] = re.astype(BF16)
        w_ref[0, r, :, dm:] = im.astype(BF16)


def _dft_fold(pab, *, batch, seq, tr):
    width = pab.shape[-1]
    nq = seq // DFT_RADIX
    z = pab.reshape(batch, DFT_RADIX, nq, width)
    quarter = [pl.BlockSpec((1, 1, tr, width), lambda b, i, q=q: (b, q, i, 0))
               for q in range(DFT_RADIX)]
    return pl.pallas_call(
        _dft_fold_kernel,
        out_shape=jax.ShapeDtypeStruct((batch, DFT_RADIX, nq, width), BF16),
        grid=(batch, nq // tr),
        in_specs=quarter,
        out_specs=pl.BlockSpec((1, DFT_RADIX, tr, width), lambda b, i: (b, 0, i, 0)),
        compiler_params=_compiler_params(("parallel", "parallel")),
        name="dft_fold",
    )(z, z, z, z)


def _dft_kernel(ca_ref, sa_ref, cb_ref, sb_ref, wre_ref, wim_ref, o_ref,
                cmat, smat, stage):
    r = pl.program_id(3)
    first = jnp.logical_and(pl.program_id(1) == 0, pl.program_id(2) == 0)

    @pl.when(first)
    def _():
        cb = cb_ref[0]
        sb = sb_ref[0]
        for a in range(ca_ref.shape[2]):
            ca = ca_ref[0, :, a:a + 1]
            sa = sa_ref[0, :, a:a + 1]
            cols = slice(a * LANES, (a + 1) * LANES)
            cmat[r, :, cols] = (ca * cb - sa * sb).astype(BF16)
            smat[r, :, cols] = (sa * cb + ca * sb).astype(BF16)

    y = (jnp.dot(cmat[r], wre_ref[0, 0], preferred_element_type=F32)
         + jnp.dot(smat[r], wim_ref[0, 0], preferred_element_type=F32))
    for cc in range(stage.shape[0]):
        stage[cc, pl.ds(r, y.shape[0], stride=DFT_RADIX), :] = (
            y[:, cc * LANES:(cc + 1) * LANES])

    @pl.when(r == DFT_RADIX - 1)
    def _():
        for cc in range(stage.shape[0]):
            o_ref[0, :, cc * LANES:(cc + 1) * LANES] = stage[cc]


def _dft(w, tables, *, batch, seq, tm, tn):
    dm = w.shape[-1] // 2
    nq = seq // DFT_RADIX
    ncb = dm // tn
    ca, sa, cb, sb = tables
    coarse = pl.BlockSpec((1, tm, ca.shape[2]), lambda i, b, c, r: (r, i, 0))
    fine = pl.BlockSpec((1, tm, LANES), lambda i, b, c, r: (r, i, 0))
    twiddle = pltpu.VMEM((DFT_RADIX, tm, nq), BF16)
    return pl.pallas_call(
        _dft_kernel,
        out_shape=jax.ShapeDtypeStruct((batch, seq, dm), F32),
        grid=(nq // tm, batch, ncb, DFT_RADIX),
        in_specs=[
            coarse, coarse, fine, fine,
            pl.BlockSpec((1, 1, nq, tn), lambda i, b, c, r: (b, r, 0, c)),
            pl.BlockSpec((1, 1, nq, tn), lambda i, b, c, r: (b, r, 0, c + ncb)),
        ],
        out_specs=pl.BlockSpec((1, DFT_RADIX * tm, tn), lambda i, b, c, r: (b, i, c)),
        scratch_shapes=[twiddle, twiddle,
                        pltpu.VMEM((tn // LANES, DFT_RADIX * tm, LANES), F32)],
        compiler_params=_compiler_params(
            ("parallel", "arbitrary", "arbitrary", "arbitrary")),
        name="dft",
    )(ca, sa, cb, sb, w, w)


def _dft_tables(seq):
    nq = seq // DFT_RADIX
    k = (DFT_RADIX * jnp.arange(nq)[None, :] + jnp.arange(DFT_RADIX)[:, None]).reshape(-1)
    coarse = _dft_angles(seq, k, LANES * jnp.arange(nq // LANES))
    fine = _dft_angles(seq, k, jnp.arange(LANES))
    return tuple(f(t).reshape(DFT_RADIX, nq, -1)
                 for t in (coarse, fine) for f in (jnp.cos, jnp.sin))


def _attn_kernel(q_ref, k_ref, v_ref, bias_ref, o_ref,
                 qp, kp, vp, s_buf, p_buf, pv, mb, lb):
    seq, hd = q_ref.shape[2], q_ref.shape[3]
    bq, kw = ATTN_BQ, ATTN_KW
    nblk = seq // bq

    assert kw == 2 * hd and s_buf.shape == (2, seq, hd)
    for src_ref, dst in ((q_ref, qp), (k_ref, kp), (v_ref, vp)):
        def load_prev(start, size, stride, src_ref=src_ref):
            return src_ref[0, 0, pl.ds(start, size, stride=stride), :]
        d_prev = 1
        for bi, d in enumerate(DILATIONS):
            ratio, sub, sub_prev = d // d_prev, seq // d, seq // d_prev
            piece = min(sub, ATTN_COPY_ROWS)
            for jp in range(d_prev):
                for r in range(ratio):
                    for c in range(sub // piece):
                        row0 = (jp + d_prev * r) * sub + c * piece
                        rows = slice(row0, row0 + piece)
                        x = load_prev(jp * sub_prev + r + c * piece * ratio, piece, ratio)
                        dst[bi, rows, :] = x.astype(BF16)
                        if 0 < bi < len(DILATIONS) - 1:
                            s_buf[bi % 2, rows, :] = x
            if d == 1:
                continue

            def load_prev(start, size, stride, half=bi % 2):
                return s_buf[half, pl.ds(start, size, stride=stride), :]
            d_prev = d

    ones = jnp.ones((kw, hd), BF16)

    for bi, d in enumerate(DILATIONS):
        sub = seq // d
        bps = sub // bq

        def place(n, d=d, sub=sub, bps=bps):
            j = n // bps
            t0 = (n % bps) * bq
            ks = jnp.clip(t0 - REACH, 0, sub - kw)
            qrows = pl.ds(pl.multiple_of(j * sub + t0, bq), bq)
            krows = pl.ds(pl.multiple_of(j * sub + ks, REACH), kw)
            if d > 1:
                orows = pl.ds(j + d * t0, bq, stride=d)
            else:
                orows = pl.ds(pl.multiple_of(t0, bq), bq)
            return qrows, krows, orows, (t0 - ks) // REACH

        def scores(n, carry, place=place, bi=bi):
            qrows, krows, _, edge = place(n)
            s = lax.dot_general(qp[bi, qrows, :], kp[bi, krows, :],
                                (((1,), (1,)), ((), ())), preferred_element_type=F32)
            s = s + bias_ref[edge]
            rows = pl.ds(pl.multiple_of(n * bq, bq), bq)
            s_buf[0, rows, :] = s[:, :hd]
            s_buf[1, rows, :] = s[:, hd:]
            return carry

        def softmax(n, carry, place=place, bi=bi):
            _, _, orows, _ = place(n)
            rows = pl.ds(pl.multiple_of(n * bq, bq), bq)
            s = jnp.concatenate([s_buf[0, rows, :], s_buf[1, rows, :]], axis=1)
            m = jnp.max(s, axis=-1, keepdims=True)
            p_buf[n] = jnp.exp2(s - m).astype(BF16)
            mb[bi, orows, :] = jnp.broadcast_to(m, (bq, hd))
            return carry

        def values(n, carry, place=place, bi=bi):
            _, krows, orows, _ = place(n)
            v1 = jnp.concatenate([vp[bi, krows, :], ones], axis=1)
            r = jnp.dot(p_buf[n], v1, preferred_element_type=F32)
            pv[bi, orows, :] = r[:, :hd]
            lb[bi, orows, :] = r[:, hd:]
            return carry

        lax.fori_loop(0, nblk, scores, 0, unroll=ATTN_UNROLL)
        lax.fori_loop(0, nblk, softmax, 0, unroll=ATTN_UNROLL)
        lax.fori_loop(0, nblk, values, 0, unroll=ATTN_UNROLL)

    def merge(c, carry):
        rows = pl.ds(pl.multiple_of(c * bq, bq), bq)
        branches = range(len(DILATIONS))
        ms = [mb[p, rows, :] for p in branches]
        top = functools.reduce(jnp.maximum, ms)
        es = [jnp.exp2(m - top) for m in ms]
        num = sum(es[p] * pv[p, rows, :] for p in branches)
        den = sum(es[p] * lb[p, rows, :] for p in branches)
        o_ref[0, rows, :] = num / den
        return carry

    lax.fori_loop(0, nblk, merge, 0, unroll=2)


def _attn_bias():
    r = jnp.arange(ATTN_BQ)[:, None]
    c = jnp.arange(ATTN_KW)[None, :]
    return jnp.stack([
        jnp.where(jnp.abs(edge * REACH + r - c) <= REACH, 0.0, MASK_VALUE).astype(F32)
        for edge in range(3)])


def _attn(q, k, v):
    batch, heads, seq, hd = q.shape
    nb = len(DILATIONS)
    nblk = seq // ATTN_BQ
    bias = _attn_bias()
    blk = pl.BlockSpec((1, 1, seq, hd), lambda b, h: (b, h, 0, 0))
    stat = pltpu.VMEM((nb, seq, hd), F32)
    gathered = pltpu.VMEM((nb, seq, hd), BF16)
    return pl.pallas_call(
        _attn_kernel,
        out_shape=jax.ShapeDtypeStruct((batch, seq, heads * hd), F32),
        grid=(batch, heads),
        in_specs=[blk, blk, blk, pl.BlockSpec(bias.shape, lambda b, h: (0, 0, 0))],
        out_specs=pl.BlockSpec((1, seq, hd), lambda b, h: (b, 0, h)),
        scratch_shapes=[
            gathered, gathered, gathered,
            pltpu.VMEM((ATTN_KW // hd, seq, hd), F32),
            pltpu.VMEM((nblk, ATTN_BQ, ATTN_KW), BF16),
            stat, stat, stat,
        ],
        compiler_params=_compiler_params(("parallel", "parallel")),
        name="attn",
    )(q, k, v, bias)


def _mixout_kernel(x_ref, yf_ref, ya_ref, gf_ref, ga_ref, gt_ref, w_ref, o_ref):
    df = yf_ref.shape[1]
    nf = (_rms(yf_ref[...]) * gf_ref[...]).astype(BF16)
    na = (_rms(ya_ref[...]) * ga_ref[...]).astype(BF16)
    out = (jnp.dot(nf, w_ref[:df, :], preferred_element_type=F32)
           + jnp.dot(na, w_ref[df:, :], preferred_element_type=F32))
    o_ref[...] = x_ref[...] + gt_ref[0] * out


def _mixout(x, yf, ya, gf, ga, gt, w_out, *, layer, seq, tm):
    t, d = x.shape
    df, da = yf.shape[1], ya.shape[1]
    tpb = seq // tm
    return pl.pallas_call(
        _mixout_kernel,
        out_shape=jax.ShapeDtypeStruct((t, d), F32),
        grid=(t // tm,),
        in_specs=[
            pl.BlockSpec((tm, d), lambda i: (i, 0)),
            pl.BlockSpec((tm, df), lambda i: (i, 0)),
            pl.BlockSpec((tm, da), lambda i: (i, 0)),
            pl.BlockSpec((1, df), lambda i: (0, 0)),
            pl.BlockSpec((1, da), lambda i: (0, 0)),
            pl.BlockSpec((1, 1, d), lambda i: (i // tpb, 0, 0)),
            pl.BlockSpec((None, df + da, d), lambda i: (layer, 0, 0)),
        ],
        out_specs=pl.BlockSpec((tm, d), lambda i: (i, 0)),
        compiler_params=_compiler_params(("parallel",)),
        name="mix_out",
    )(x, yf, ya, gf, ga, gt, w_out)


class _Tiles(NamedTuple):
    ffn_rows: int
    ffn_cols: int
    mix_rows: int
    fold_rows: int
    dft_rows: int
    dft_cols: int
    ada_cols: int


def _tiles(seq, d, d_ff):
    ffn_cols = 512 if d_ff % 512 == 0 else LANES
    return _Tiles(ffn_rows=min(512, seq), ffn_cols=ffn_cols, mix_rows=min(512, seq),
                  fold_rows=min(256, seq // DFT_RADIX),
                  dft_rows=min(512, seq // DFT_RADIX), dft_cols=min(1024, d // 2), ada_cols=min(1024, d))


def kernel(x, c, w_ada, b_ada, w_ffn1_gu, w_ffn1_down, w_mix_in, w_fourier,
           g_fourier_out, g_attn_out, w_mix_out, w_ffn2_gu, w_ffn2_down, g_final):
    batch, seq, d = x.shape
    depth = w_ada.shape[0]
    d_ff = w_ffn1_down.shape[1]
    t = batch * seq
    tl = _tiles(seq, d, d_ff)

    rows = 8 * pl.cdiv(batch, 8)
    c_pad = jnp.zeros((rows, d), F32).at[:batch].set(c)
    mod = _ada(c_pad, w_ada, b_ada, tn=tl.ada_cols)
    mod = mod[:, :batch].reshape(depth, batch, N_MOD, 1, d)

    groups = w_fourier.shape[1]
    ab = _fprep(w_fourier.reshape(depth * groups, HEAD_DIM, HEAD_DIM), seq)
    ab = ab.reshape(depth, groups, HEAD_DIM, 2 * HEAD_DIM)
    rope_cos, rope_sin = _rope_tables(seq)
    dft_tables = _dft_tables(seq)
    g_fin = g_final.reshape(1, d)

    wgu1, wd1, wgu2, wd2, w_in, w_out = (
        w.astype(BF16) for w in (w_ffn1_gu, w_ffn1_down, w_ffn2_gu, w_ffn2_down,
                                 w_mix_in, w_mix_out))

    xt = x.reshape(t, d)
    for l in range(depth):
        sh1, sc1, g1, sh2, sc2, g2, sh3, sc3, g3 = (mod[l, :, i] for i in range(N_MOD))
        xt = _ffn(xt, sh1, sc1, g1, wgu1, wd1, g_fin, layer=l,
                  seq=seq, tm=tl.ffn_rows, tf=tl.ffn_cols, final_norm=False)
        pab, q, k, v = _mixin(xt, sh2, sc2, w_in, ab, rope_cos, rope_sin, layer=l,
                              batch=batch, seq=seq, tm=tl.mix_rows)
        w = _dft_fold(pab, batch=batch, seq=seq, tr=tl.fold_rows)
        yf = _dft(w, dft_tables, batch=batch, seq=seq, tm=tl.dft_rows, tn=tl.dft_cols)
        ya = _attn(q, k, v)
        xt = _mixout(xt, yf.reshape(t, -1), ya.reshape(t, -1),
                     g_fourier_out[l].reshape(1, -1), g_attn_out[l].reshape(1, -1),
                     g2, w_out, layer=l, seq=seq, tm=tl.mix_rows)
        xt = _ffn(xt, sh3, sc3, g3, wgu2, wd2, g_fin, layer=l,
                  seq=seq, tm=tl.ffn_rows, tf=tl.ffn_cols, final_norm=(l == depth - 1))
    return xt.reshape(batch, seq, d)
```

```python
import functools
import math
from typing import NamedTuple

import jax
import jax.numpy as jnp
from jax import lax
from jax.experimental import pallas as pl
from jax.experimental.pallas import tpu as pltpu

F32 = jnp.float32
BF16 = jnp.bfloat16

EPS = 1e-6
HEAD_DIM = 128
ROPE_THETA = 10000.0
DILATED_PATTERNS = ((128, 1), (512, 4), (2048, 16))
DILATIONS = tuple(d for _, d in DILATED_PATTERNS)
REACH = (DILATED_PATTERNS[0][0] // 2) // DILATED_PATTERNS[0][1]
assert all((w // 2) // d == REACH for w, d in DILATED_PATTERNS)
MASK_VALUE = -1e30
N_MOD = 9

LANES = 128
VMEM_LIMIT_BYTES = 56 * 1024 * 1024

FFN_COL_SPLIT = 2

ATTN_BQ = 128
ATTN_KW = ATTN_BQ + 2 * REACH
ATTN_COPY_ROWS = 128
ATTN_UNROLL = True
ATTN_VMEM_LIMIT_BYTES = 60 * 1024 * 1024


def _compiler_params(semantics, vmem_limit_bytes=VMEM_LIMIT_BYTES):
    return pltpu.CompilerParams(dimension_semantics=semantics,
                                vmem_limit_bytes=vmem_limit_bytes)


def _rms(x):
    return x * lax.rsqrt(jnp.mean(x * x, axis=-1, keepdims=True) + EPS)


def _silu(x):
    return x * jax.nn.sigmoid(x)


def _ada_kernel(c_ref, w_ref, b_ref, o_ref):
    ca = _silu(c_ref[...]).astype(BF16)
    o_ref[0] = jnp.dot(ca, w_ref[0].astype(BF16),
                       preferred_element_type=F32) + b_ref[0]


def _ada(c_pad, w_ada, b_ada, *, tn):
    depth, d, n = w_ada.shape
    rows = c_pad.shape[0]
    return pl.pallas_call(
        _ada_kernel,
        out_shape=jax.ShapeDtypeStruct((depth, rows, n), F32),
        grid=(depth, n // tn),
        in_specs=[
            pl.BlockSpec((rows, d), lambda l, j: (0, 0)),
            pl.BlockSpec((1, d, tn), lambda l, j: (l, 0, j)),
            pl.BlockSpec((1, 1, tn), lambda l, j: (l, 0, j)),
        ],
        out_specs=pl.BlockSpec((1, rows, tn), lambda l, j: (l, 0, j)),
        compiler_params=_compiler_params(("parallel", "parallel")),
        name="ada",
    )(c_pad, w_ada, b_ada.reshape(depth, 1, n))


def _ffn_kernel(x_ref, sh_ref, sc_ref, gt_ref, wg_ref, wu_ref, wd_ref, gf_ref,
                o_ref, h_ref, *, final_norm):
    j = pl.program_id(1)

    @pl.when(j == 0)
    def _():
        h = _rms(x_ref[...]) * (1.0 + sc_ref[0]) + sh_ref[0]
        h_ref[...] = h.astype(BF16)
        o_ref[...] = jnp.zeros_like(o_ref)

    h = h_ref[...]
    tf = wd_ref.shape[0]
    part = tf // FFN_COL_SPLIT
    gu = []
    for s in range(FFN_COL_SPLIT):
        cols = slice(s * part, (s + 1) * part)
        gu.append((jnp.dot(h, wg_ref[:, cols], preferred_element_type=F32),
                   jnp.dot(h, wu_ref[:, cols], preferred_element_type=F32)))
    for s, (g, u) in enumerate(gu):
        a = (_silu(g) * u).astype(BF16)
        o_ref[...] += jnp.dot(a, wd_ref[s * part:(s + 1) * part, :],
                              preferred_element_type=F32)

    @pl.when(j == pl.num_programs(1) - 1)
    def _():
        y = x_ref[...] + (0.5 * gt_ref[0]) * o_ref[...]
        if final_norm:
            y = _rms(y) * gf_ref[...]
        o_ref[...] = y


def _ffn(x, sh, sc, gt, w_gu, w_down, g_final, *, layer, seq, tm, tf, final_norm):
    t, d = x.shape
    f = w_down.shape[1]
    nj = f // tf
    tpb = seq // tm
    vec = pl.BlockSpec((1, 1, d), lambda i, j: (i // tpb, 0, 0))
    return pl.pallas_call(
        functools.partial(_ffn_kernel, final_norm=final_norm),
        out_shape=jax.ShapeDtypeStruct((t, d), F32),
        grid=(t // tm, nj),
        in_specs=[
            pl.BlockSpec((tm, d), lambda i, j: (i, 0)),
            vec, vec, vec,
            pl.BlockSpec((None, d, tf), lambda i, j: (layer, 0, j)),
            pl.BlockSpec((None, d, tf), lambda i, j: (layer, 0, j + nj)),
            pl.BlockSpec((None, tf, d), lambda i, j: (layer, j, 0)),
            pl.BlockSpec((1, d), lambda i, j: (0, 0)),
        ],
        out_specs=pl.BlockSpec((tm, d), lambda i, j: (i, 0)),
        scratch_shapes=[pltpu.VMEM((tm, d), BF16)],
        compiler_params=_compiler_params(("parallel", "arbitrary")),
        name="ffn",
    )(x, sh, sc, gt, w_gu, w_gu, w_down, g_final)


def _fprep_kernel(cc_ref, sc_ref, w_ref, ab_ref, *, scale):
    w = w_ref[0]
    hc = w.shape[0]
    a = jnp.dot(cc_ref[...], w, preferred_element_type=F32,
                precision=lax.Precision.HIGHEST)
    b = jnp.dot(sc_ref[...], w, preferred_element_type=F32,
                precision=lax.Precision.HIGHEST)
    ab_ref[0, :, :hc] = (a * scale).astype(BF16)
    ab_ref[0, :, hc:] = (b * scale).astype(BF16)


def _fprep(w_fourier, seq):
    n, hc, _ = w_fourier.shape
    ang = _dft_angles(hc, jnp.arange(hc), jnp.arange(hc))
    scale = 1.0 / math.sqrt(seq * hc)
    sq = pl.BlockSpec((hc, hc), lambda g: (0, 0))
    return pl.pallas_call(
        functools.partial(_fprep_kernel, scale=scale),
        out_shape=jax.ShapeDtypeStruct((n, hc, 2 * hc), BF16),
        grid=(n,),
        in_specs=[sq, sq, pl.BlockSpec((1, hc, hc), lambda g: (g, 0, 0))],
        out_specs=pl.BlockSpec((1, hc, 2 * hc), lambda g: (g, 0, 0)),
        compiler_params=_compiler_params(("parallel",)),
        name="fprep",
    )(jnp.cos(ang), jnp.sin(ang), w_fourier)


def _dft_angles(n, rows, cols):
    idx = (rows[:, None] * cols[None, :]) % n
    return idx.astype(F32) * (2.0 * math.pi / n)


def _mixin_kernel(x_ref, sh_ref, sc_ref, w_ref, ab_ref, cos_ref, sin_ref,
                  pab_ref, q_ref, k_ref, v_ref, *, q_scale):
    hc = HEAD_DIM
    dm = w_ref.shape[1] // 4
    groups = heads = dm // hc
    h = (_rms(x_ref[...]) * (1.0 + sc_ref[0]) + sh_ref[0]).astype(BF16)

    def project(n):
        return jnp.dot(h, w_ref[:, n * dm:(n + 1) * dm], preferred_element_type=F32)

    def rope_to(dst_ref, p, scale):
        cos = cos_ref[...] * scale
        sin = sin_ref[...] * scale
        for hh in range(heads):
            t = p[:, hh * hc:(hh + 1) * hc]
            dst_ref[0, hh] = t * cos + pltpu.roll(t, hc // 2, 1) * sin

    u = project(0).astype(BF16)
    pq = project(1)
    for g in range(groups):
        r = jnp.dot(u[:, g * hc:(g + 1) * hc], ab_ref[g], preferred_element_type=F32)
        pab_ref[:, g * hc:(g + 1) * hc] = r[:, :hc].astype(BF16)
        pab_ref[:, (groups + g) * hc:(groups + g + 1) * hc] = r[:, hc:].astype(BF16)
    rope_to(q_ref, pq, q_scale)
    rope_to(k_ref, project(2), 1.0)
    pv = project(3)
    for hh in range(heads):
        v_ref[0, hh] = pv[:, hh * hc:(hh + 1) * hc]


def _mixin(x, sh, sc, w_in, ab, cos, sin, *, layer, batch, seq, tm):
    t, d = x.shape
    dm = w_in.shape[2] // 4
    heads = dm // HEAD_DIM
    tpb = seq // tm
    vec = pl.BlockSpec((1, 1, d), lambda i: (i // tpb, 0, 0))
    tab = pl.BlockSpec((tm, HEAD_DIM), lambda i: (i % tpb, 0))
    hm = pl.BlockSpec((1, heads, tm, HEAD_DIM), lambda i: (i // tpb, 0, i % tpb, 0))
    hm_shape = jax.ShapeDtypeStruct((batch, heads, seq, HEAD_DIM), F32)
    resident = pl.Buffered(1)
    return pl.pallas_call(
        functools.partial(_mixin_kernel, q_scale=HEAD_DIM ** -0.5 * math.log2(math.e)),
        out_shape=(jax.ShapeDtypeStruct((t, 2 * dm), BF16),
                   hm_shape, hm_shape, hm_shape),
        grid=(t // tm,),
        in_specs=[
            pl.BlockSpec((tm, d), lambda i: (i, 0)),
            vec, vec,
            pl.BlockSpec((None, d, 4 * dm), lambda i: (layer, 0, 0), pipeline_mode=resident),
            pl.BlockSpec((None,) + ab.shape[1:], lambda i: (layer, 0, 0, 0),
                         pipeline_mode=resident),
            tab, tab,
        ],
        out_specs=(pl.BlockSpec((tm, 2 * dm), lambda i: (i, 0)), hm, hm, hm),
        compiler_params=_compiler_params(("parallel",)),
        name="mix_in",
    )(x, sh, sc, w_in, ab, cos, sin)


def _rope_tables(seq):
    half = HEAD_DIM // 2
    inv_freq = ROPE_THETA ** (-jnp.arange(half, dtype=F32) / half)
    ang = jnp.arange(seq, dtype=F32)[:, None] * inv_freq[None, :]
    cos, sin = jnp.cos(ang), jnp.sin(ang)
    return (jnp.concatenate([cos, cos], axis=-1),
            jnp.concatenate([-sin, sin], axis=-1))


DFT_RADIX = 4


def _dft_fold_kernel(z0_ref, z1_ref, z2_ref, z3_ref, w_ref):
    dm = w_ref.shape[3] // 2
    a = [z[0, 0, :, :dm].astype(F32) for z in (z0_ref, z1_ref, z2_ref, z3_ref)]
    p = [z[0, 0, :, dm:].astype(F32) for z in (z0_ref, z1_ref, z2_ref, z3_ref)]
    t0r, t0i = a[0] + a[2], -(p[0] + p[2])
    t1r, t1i = a[0] - a[2], p[2] - p[0]
    t2r, t2i = a[1] + a[3], -(p[1] + p[3])
    t3r, t3i = a[1] - a[3], p[3] - p[1]
    parts = ((t0r + t2r, t0i + t2i),
             (t1r + t3i, t1i - t3r),
             (t0r - t2r, t0i - t2i),
             (t1r - t3i, t1i + t3r))
    for r, (re, im) in enumerate(parts):
        w_ref[0, r, :, :dm] = re.astype(BF16)
        w_ref[0, r, :, dm:] = im.astype(BF16)


def _dft_fold(pab, *, batch, seq, tr):
    width = pab.shape[-1]
    nq = seq // DFT_RADIX
    z = pab.reshape(batch, DFT_RADIX, nq, width)
    quarter = [pl.BlockSpec((1, 1, tr, width), lambda b, i, q=q: (b, q, i, 0))
               for q in range(DFT_RADIX)]
    return pl.pallas_call(
        _dft_fold_kernel,
        out_shape=jax.ShapeDtypeStruct((batch, DFT_RADIX, nq, width), BF16),
        grid=(batch, nq // tr),
        in_specs=quarter,
        out_specs=pl.BlockSpec((1, DFT_RADIX, tr, width), lambda b, i: (b, 0, i, 0)),
        compiler_params=_compiler_params(("parallel", "parallel")),
        name="dft_fold",
    )(z, z, z, z)


def _dft_kernel(ca_ref, sa_ref, cb_ref, sb_ref, wre_ref, wim_ref, o_ref,
                cmat, smat, stage):
    r = pl.program_id(3)
    first = jnp.logical_and(pl.program_id(1) == 0, pl.program_id(2) == 0)

    @pl.when(first)
    def _():
        cb = cb_ref[0]
        sb = sb_ref[0]
        for a in range(ca_ref.shape[2]):
            ca = ca_ref[0, :, a:a + 1]
            sa = sa_ref[0, :, a:a + 1]
            cols = slice(a * LANES, (a + 1) * LANES)
            cmat[r, :, cols] = (ca * cb - sa * sb).astype(BF16)
            smat[r, :, cols] = (sa * cb + ca * sb).astype(BF16)

    y = (jnp.dot(cmat[r], wre_ref[0, 0], preferred_element_type=F32)
         + jnp.dot(smat[r], wim_ref[0, 0], preferred_element_type=F32))
    for cc in range(stage.shape[0]):
        stage[cc, pl.ds(r, y.shape[0], stride=DFT_RADIX), :] = (
            y[:, cc * LANES:(cc + 1) * LANES])

    @pl.when(r == DFT_RADIX - 1)
    def _():
        for cc in range(stage.shape[0]):
            o_ref[0, :, cc * LANES:(cc + 1) * LANES] = stage[cc]


def _dft(w, tables, *, batch, seq, tm, tn):
    dm = w.shape[-1] // 2
    nq = seq // DFT_RADIX
    ncb = dm // tn
    ca, sa, cb, sb = tables
    coarse = pl.BlockSpec((1, tm, ca.shape[2]), lambda i, b, c, r: (r, i, 0))
    fine = pl.BlockSpec((1, tm, LANES), lambda i, b, c, r: (r, i, 0))
    twiddle = pltpu.VMEM((DFT_RADIX, tm, nq), BF16)
    return pl.pallas_call(
        _dft_kernel,
        out_shape=jax.ShapeDtypeStruct((batch, seq, dm), F32),
        grid=(nq // tm, batch, ncb, DFT_RADIX),
        in_specs=[
            coarse, coarse, fine, fine,
            pl.BlockSpec((1, 1, nq, tn), lambda i, b, c, r: (b, r, 0, c)),
            pl.BlockSpec((1, 1, nq, tn), lambda i, b, c, r: (b, r, 0, c + ncb)),
        ],
        out_specs=pl.BlockSpec((1, DFT_RADIX * tm, tn), lambda i, b, c, r: (b, i, c)),
        scratch_shapes=[twiddle, twiddle,
                        pltpu.VMEM((tn // LANES, DFT_RADIX * tm, LANES), F32)],
        compiler_params=_compiler_params(
            ("parallel", "arbitrary", "arbitrary", "arbitrary")),
        name="dft",
    )(ca, sa, cb, sb, w, w)


def _dft_tables(seq):
    nq = seq // DFT_RADIX
    k = (DFT_RADIX * jnp.arange(nq)[None, :] + jnp.arange(DFT_RADIX)[:, None]).reshape(-1)
    coarse = _dft_angles(seq, k, LANES * jnp.arange(nq // LANES))
    fine = _dft_angles(seq, k, jnp.arange(LANES))
    return tuple(f(t).reshape(DFT_RADIX, nq, -1)
                 for t in (coarse, fine) for f in (jnp.cos, jnp.sin))


def _attn_kernel(q_ref, k_ref, v_ref, bias_ref, o_ref,
                 qp, kp, vp, s_buf, p_buf, pv, mb, lb):
    seq, hd = q_ref.shape[2], q_ref.shape[3]
    bq, kw = ATTN_BQ, ATTN_KW
    nblk = seq // bq

    assert kw == 2 * hd and s_buf.shape == (2, seq, hd)
    for src_ref, dst in ((q_ref, qp), (k_ref, kp), (v_ref, vp)):
        def load_prev(start, size, stride, src_ref=src_ref):
            return src_ref[0, 0, pl.ds(start, size, stride=stride), :]
        d_prev = 1
        for bi, d in enumerate(DILATIONS):
            ratio, sub, sub_prev = d // d_prev, seq // d, seq // d_prev
            piece = min(sub, ATTN_COPY_ROWS)
            for jp in range(d_prev):
                for r in range(ratio):
                    for c in range(sub // piece):
                        row0 = (jp + d_prev * r) * sub + c * piece
                        rows = slice(row0, row0 + piece)
                        x = load_prev(jp * sub_prev + r + c * piece * ratio, piece, ratio)
                        dst[bi, rows, :] = x.astype(BF16)
                        if 0 < bi < len(DILATIONS) - 1:
                            s_buf[bi % 2, rows, :] = x
            if d == 1:
                continue

            def load_prev(start, size, stride, half=bi % 2):
                return s_buf[half, pl.ds(start, size, stride=stride), :]
            d_prev = d

    ones = jnp.ones((kw, hd), BF16)

    for bi, d in enumerate(DILATIONS):
        sub = seq // d
        bps = sub // bq

        def place(n, d=d, sub=sub, bps=bps):
            j = n // bps
            t0 = (n % bps) * bq
            ks = jnp.clip(t0 - REACH, 0, sub - kw)
            qrows = pl.ds(pl.multiple_of(j * sub + t0, bq), bq)
            krows = pl.ds(pl.multiple_of(j * sub + ks, REACH), kw)
            if d > 1:
                orows = pl.ds(j + d * t0, bq, stride=d)
            else:
                orows = pl.ds(pl.multiple_of(t0, bq), bq)
            return qrows, krows, orows, (t0 - ks) // REACH

        def scores(n, carry, place=place, bi=bi):
            qrows, krows, _, edge = place(n)
            s = lax.dot_general(qp[bi, qrows, :], kp[bi, krows, :],
                                (((1,), (1,)), ((), ())), preferred_element_type=F32)
            s = s + bias_ref[edge]
            rows = pl.ds(pl.multiple_of(n * bq, bq), bq)
            s_buf[0, rows, :] = s[:, :hd]
            s_buf[1, rows, :] = s[:, hd:]
            return carry

        def softmax(n, carry, place=place, bi=bi):
            _, _, orows, _ = place(n)
            rows = pl.ds(pl.multiple_of(n * bq, bq), bq)
            s = jnp.concatenate([s_buf[0, rows, :], s_buf[1, rows, :]], axis=1)
            m = jnp.max(s, axis=-1, keepdims=True)
            p_buf[n] = jnp.exp2(s - m).astype(BF16)
            mb[bi, orows, :] = jnp.broadcast_to(m, (bq, hd))
            return carry

        def values(n, carry, place=place, bi=bi):
            _, krows, orows, _ = place(n)
            v1 = jnp.concatenate([vp[bi, krows, :], ones], axis=1)
            r = jnp.dot(p_buf[n], v1, preferred_element_type=F32)
            pv[bi, orows, :] = r[:, :hd]
            lb[bi, orows, :] = r[:, hd:]
            return carry

        lax.fori_loop(0, nblk, scores, 0, unroll=ATTN_UNROLL)
        lax.fori_loop(0, nblk, softmax, 0, unroll=ATTN_UNROLL)
        lax.fori_loop(0, nblk, values, 0, unroll=ATTN_UNROLL)

    def merge(c, carry):
        rows = pl.ds(pl.multiple_of(c * bq, bq), bq)
        branches = range(len(DILATIONS))
        ms = [mb[p, rows, :] for p in branches]
        top = functools.reduce(jnp.maximum, ms)
        es = [jnp.exp2(m - top) for m in ms]
        num = sum(es[p] * pv[p, rows, :] for p in branches)
        den = sum(es[p] * lb[p, rows, :] for p in branches)
        o_ref[0, rows, :] = num / den
        return carry

    lax.fori_loop(0, nblk, merge, 0, unroll=2)


def _attn_bias():
    r = jnp.arange(ATTN_BQ)[:, None]
    c = jnp.arange(ATTN_KW)[None, :]
    return jnp.stack([
        jnp.where(jnp.abs(edge * REACH + r - c) <= REACH, 0.0, MASK_VALUE).astype(F32)
        for edge in range(3)])


def _attn(q, k, v):
    batch, heads, seq, hd = q.shape
    nb = len(DILATIONS)
    nblk = seq // ATTN_BQ
    bias = _attn_bias()
    blk = pl.BlockSpec((1, 1, seq, hd), lambda b, h: (b, h, 0, 0))
    stat = pltpu.VMEM((nb, seq, hd), F32)
    gathered = pltpu.VMEM((nb, seq, hd), BF16)
    return pl.pallas_call(
        _attn_kernel,
        out_shape=jax.ShapeDtypeStruct((batch, seq, heads * hd), F32),
        grid=(batch, heads),
        in_specs=[blk, blk, blk, pl.BlockSpec(bias.shape, lambda b, h: (0, 0, 0))],
        out_specs=pl.BlockSpec((1, seq, hd), lambda b, h: (b, 0, h)),
        scratch_shapes=[
            gathered, gathered, gathered,
            pltpu.VMEM((ATTN_KW // hd, seq, hd), F32),
            pltpu.VMEM((nblk, ATTN_BQ, ATTN_KW), BF16),
            stat, stat, stat,
        ],
        compiler_params=_compiler_params(("parallel", "parallel"), ATTN_VMEM_LIMIT_BYTES),
        name="attn",
    )(q, k, v, bias)


def _mixout_kernel(x_ref, yf_ref, ya_ref, gf_ref, ga_ref, gt_ref, w_ref, o_ref):
    df = yf_ref.shape[1]
    nf = (_rms(yf_ref[...]) * gf_ref[...]).astype(BF16)
    na = (_rms(ya_ref[...]) * ga_ref[...]).astype(BF16)
    out = (jnp.dot(nf, w_ref[:df, :], preferred_element_type=F32)
           + jnp.dot(na, w_ref[df:, :], preferred_element_type=F32))
    o_ref[...] = x_ref[...] + gt_ref[0] * out


def _mixout(x, yf, ya, gf, ga, gt, w_out, *, layer, seq, tm):
    t, d = x.shape
    df, da = yf.shape[1], ya.shape[1]
    tpb = seq // tm
    return pl.pallas_call(
        _mixout_kernel,
        out_shape=jax.ShapeDtypeStruct((t, d), F32),
        grid=(t // tm,),
        in_specs=[
            pl.BlockSpec((tm, d), lambda i: (i, 0)),
            pl.BlockSpec((tm, df), lambda i: (i, 0)),
            pl.BlockSpec((tm, da), lambda i: (i, 0)),
            pl.BlockSpec((1, df), lambda i: (0, 0)),
            pl.BlockSpec((1, da), lambda i: (0, 0)),
            pl.BlockSpec((1, 1, d), lambda i: (i // tpb, 0, 0)),
            pl.BlockSpec((None, df + da, d), lambda i: (layer, 0, 0)),
        ],
        out_specs=pl.BlockSpec((tm, d), lambda i: (i, 0)),
        compiler_params=_compiler_params(("parallel",)),
        name="mix_out",
    )(x, yf, ya, gf, ga, gt, w_out)


class _Tiles(NamedTuple):
    ffn_rows: int
    ffn_cols: int
    mix_rows: int
    fold_rows: int
    dft_rows: int
    dft_cols: int
    ada_cols: int


def _tiles(seq, d, d_ff):
    ffn_cols = 512 if d_ff % 512 == 0 else LANES
    return _Tiles(ffn_rows=min(512, seq), ffn_cols=ffn_cols, mix_rows=min(512, seq),
                  fold_rows=min(256, seq // DFT_RADIX),
                  dft_rows=min(512, seq // DFT_RADIX), dft_cols=min(1024, d // 2), ada_cols=min(1024, d))


def kernel(x, c, w_ada, b_ada, w_ffn1_gu, w_ffn1_down, w_mix_in, w_fourier,
           g_fourier_out, g_attn_out, w_mix_out, w_ffn2_gu, w_ffn2_down, g_final):
    batch, seq, d = x.shape
    depth = w_ada.shape[0]
    d_ff = w_ffn1_down.shape[1]
    t = batch * seq
    tl = _tiles(seq, d, d_ff)

    rows = 8 * pl.cdiv(batch, 8)
    c_pad = jnp.zeros((rows, d), F32).at[:batch].set(c)
    mod = _ada(c_pad, w_ada, b_ada, tn=tl.ada_cols)
    mod = mod[:, :batch].reshape(depth, batch, N_MOD, 1, d)

    groups = w_fourier.shape[1]
    ab = _fprep(w_fourier.reshape(depth * groups, HEAD_DIM, HEAD_DIM), seq)
    ab = ab.reshape(depth, groups, HEAD_DIM, 2 * HEAD_DIM)
    rope_cos, rope_sin = _rope_tables(seq)
    dft_tables = _dft_tables(seq)
    g_fin = g_final.reshape(1, d)

    wgu1, wd1, wgu2, wd2, w_in, w_out = (
        w.astype(BF16) for w in (w_ffn1_gu, w_ffn1_down, w_ffn2_gu, w_ffn2_down,
                                 w_mix_in, w_mix_out))

    xt = x.reshape(t, d)
    for l in range(depth):
        sh1, sc1, g1, sh2, sc2, g2, sh3, sc3, g3 = (mod[l, :, i] for i in range(N_MOD))
        xt = _ffn(xt, sh1, sc1, g1, wgu1, wd1, g_fin, layer=l,
                  seq=seq, tm=tl.ffn_rows, tf=tl.ffn_cols, final_norm=False)
        pab, q, k, v = _mixin(xt, sh2, sc2, w_in, ab, rope_cos, rope_sin, layer=l,
                              batch=batch, seq=seq, tm=tl.mix_rows)
        w = _dft_fold(pab, batch=batch, seq=seq, tr=tl.fold_rows)
        yf = _dft(w, dft_tables, batch=batch, seq=seq, tm=tl.dft_rows, tn=tl.dft_cols)
        ya = _attn(q, k, v)
        xt = _mixout(xt, yf.reshape(t, -1), ya.reshape(t, -1),
                     g_fourier_out[l].reshape(1, -1), g_attn_out[l].reshape(1, -1),
                     g2, w_out, layer=l, seq=seq, tm=tl.mix_rows)
        xt = _ffn(xt, sh3, sc3, g3, wgu2, wd2, g_fin, layer=l,
                  seq=seq, tm=tl.ffn_rows, tf=tl.ffn_cols, final_norm=(l == depth - 1))
    return xt.reshape(batch, seq, d)
```

```python
import functools
import math
from typing import NamedTuple

import jax
import jax.numpy as jnp
from jax import lax
from jax.experimental import pallas as pl
from jax.experimental.pallas import tpu as pltpu

F32 = jnp.float32
BF16 = jnp.bfloat16

EPS = 1e-6
HEAD_DIM = 128
ROPE_THETA = 10000.0
DILATED_PATTERNS = ((128, 1), (512, 4), (2048, 16))
DILATIONS = tuple(d for _, d in DILATED_PATTERNS)
REACH = (DILATED_PATTERNS[0][0] // 2) // DILATED_PATTERNS[0][1]
assert all((w // 2) // d == REACH for w, d in DILATED_PATTERNS)
MASK_VALUE = -1e30
N_MOD = 9

LANES = 128
VMEM_LIMIT_BYTES = 56 * 1024 * 1024

FFN_COL_SPLIT = 2

ATTN_BQ = 128
ATTN_KW = ATTN_BQ + 2 * REACH
ATTN_COPY_ROWS = 128
ATTN_UNROLL = True
ATTN_VMEM_LIMIT_BYTES = 60 * 1024 * 1024


def _compiler_params(semantics, vmem_limit_bytes=VMEM_LIMIT_BYTES):
    return pltpu.CompilerParams(dimension_semantics=semantics,
                                vmem_limit_bytes=vmem_limit_bytes)


def _rms(x):
    return x * lax.rsqrt(jnp.mean(x * x, axis=-1, keepdims=True) + EPS)


def _silu(x):
    return x * jax.nn.sigmoid(x)


def _ada_kernel(c_ref, w_ref, b_ref, o_ref):
    ca = _silu(c_ref[...]).astype(BF16)
    o_ref[0] = jnp.dot(ca, w_ref[0].astype(BF16),
                       preferred_element_type=F32) + b_ref[0]


def _ada(c_pad, w_ada, b_ada, *, tn):
    depth, d, n = w_ada.shape
    rows = c_pad.shape[0]
    return pl.pallas_call(
        _ada_kernel,
        out_shape=jax.ShapeDtypeStruct((depth, rows, n), F32),
        grid=(depth, n // tn),
        in_specs=[
            pl.BlockSpec((rows, d), lambda l, j: (0, 0)),
            pl.BlockSpec((1, d, tn), lambda l, j: (l, 0, j)),
            pl.BlockSpec((1, 1, tn), lambda l, j: (l, 0, j)),
        ],
        out_specs=pl.BlockSpec((1, rows, tn), lambda l, j: (l, 0, j)),
        compiler_params=_compiler_params(("parallel", "parallel")),
        name="ada",
    )(c_pad, w_ada, b_ada.reshape(depth, 1, n))


def _ffn_kernel(x_ref, sh_ref, sc_ref, gt_ref, wg0_ref, wu0_ref, wd0_ref, wgu_hbm, wd_hbm,
                gf_ref, o_ref, wg_buf, wu_buf, wd_buf, sems, *, layer, final_norm):
    d, tf = wg0_ref.shape
    f = wd_hbm.shape[1]
    nj = f // tf
    part = tf // FFN_COL_SPLIT

    def fetch(j, slot):
        return (
            pltpu.make_async_copy(wgu_hbm.at[layer, :, pl.ds(j * tf, tf)],
                                  wg_buf.at[slot], sems.at[0, slot]),
            pltpu.make_async_copy(wgu_hbm.at[layer, :, pl.ds(f + j * tf, tf)],
                                  wu_buf.at[slot], sems.at[1, slot]),
            pltpu.make_async_copy(wd_hbm.at[layer, pl.ds(j * tf, tf), :],
                                  wd_buf.at[slot], sems.at[2, slot]),
        )

    if nj > 1:
        for c in fetch(1, 1):
            c.start()
    h = (_rms(x_ref[...]) * (1.0 + sc_ref[0]) + sh_ref[0]).astype(BF16)

    for j in range(nj):
        if j == 0:
            wg, wu, wd = wg0_ref, wu0_ref, wd0_ref
        else:
            slot = j % 2
            for c in fetch(j, slot):
                c.wait()
            if j + 1 < nj:
                for c in fetch(j + 1, 1 - slot):
                    c.start()
            wg, wu, wd = wg_buf.at[slot], wu_buf.at[slot], wd_buf.at[slot]
        gu = []
        for s in range(FFN_COL_SPLIT):
            cols = slice(s * part, (s + 1) * part)
            gu.append((jnp.dot(h, wg[:, cols], preferred_element_type=F32),
                       jnp.dot(h, wu[:, cols], preferred_element_type=F32)))
        for s, (g, u) in enumerate(gu):
            a = (_silu(g) * u).astype(BF16)
            r = jnp.dot(a, wd[s * part:(s + 1) * part, :], preferred_element_type=F32)
            if j == 0 and s == 0:
                o_ref[...] = r
            else:
                o_ref[...] += r

    y = x_ref[...] + (0.5 * gt_ref[0]) * o_ref[...]
    if final_norm:
        y = _rms(y) * gf_ref[...]
    o_ref[...] = y


def _ffn(x, sh, sc, gt, w_gu, w_down, g_final, *, layer, seq, tm, tf, final_norm):
    t, d = x.shape
    f = w_down.shape[1]
    nj = f // tf
    tpb = seq // tm
    vec = pl.BlockSpec((1, 1, d), lambda i: (i // tpb, 0, 0))
    resident = pl.Buffered(1)
    return pl.pallas_call(
        functools.partial(_ffn_kernel, layer=layer, final_norm=final_norm),
        out_shape=jax.ShapeDtypeStruct((t, d), F32),
        grid=(t // tm,),
        in_specs=[
            pl.BlockSpec((tm, d), lambda i: (i, 0)),
            vec, vec, vec,
            pl.BlockSpec((None, d, tf), lambda i: (layer, 0, 0), pipeline_mode=resident),
            pl.BlockSpec((None, d, tf), lambda i: (layer, 0, nj), pipeline_mode=resident),
            pl.BlockSpec((None, tf, d), lambda i: (layer, 0, 0), pipeline_mode=resident),
            pl.BlockSpec(memory_space=pl.ANY),
            pl.BlockSpec(memory_space=pl.ANY),
            pl.BlockSpec((1, d), lambda i: (0, 0)),
        ],
        out_specs=pl.BlockSpec((tm, d), lambda i: (i, 0)),
        scratch_shapes=[
            pltpu.VMEM((2, d, tf), BF16),
            pltpu.VMEM((2, d, tf), BF16),
            pltpu.VMEM((2, tf, d), BF16),
            pltpu.SemaphoreType.DMA((3, 2)),
        ],
        compiler_params=_compiler_params(("parallel",)),
        name="ffn",
    )(x, sh, sc, gt, w_gu, w_gu, w_down, w_gu, w_down, g_final)


def _fprep_kernel(cc_ref, sc_ref, w_ref, ab_ref, *, scale):
    w = w_ref[0]
    hc = w.shape[0]
    a = jnp.dot(cc_ref[...], w, preferred_element_type=F32,
                precision=lax.Precision.HIGHEST)
    b = jnp.dot(sc_ref[...], w, preferred_element_type=F32,
                precision=lax.Precision.HIGHEST)
    ab_ref[0, :, :hc] = (a * scale).astype(BF16)
    ab_ref[0, :, hc:] = (b * scale).astype(BF16)


def _fprep(w_fourier, seq):
    n, hc, _ = w_fourier.shape
    ang = _dft_angles(hc, jnp.arange(hc), jnp.arange(hc))
    scale = 1.0 / math.sqrt(seq * hc)
    sq = pl.BlockSpec((hc, hc), lambda g: (0, 0))
    return pl.pallas_call(
        functools.partial(_fprep_kernel, scale=scale),
        out_shape=jax.ShapeDtypeStruct((n, hc, 2 * hc), BF16),
        grid=(n,),
        in_specs=[sq, sq, pl.BlockSpec((1, hc, hc), lambda g: (g, 0, 0))],
        out_specs=pl.BlockSpec((1, hc, 2 * hc), lambda g: (g, 0, 0)),
        compiler_params=_compiler_params(("parallel",)),
        name="fprep",
    )(jnp.cos(ang), jnp.sin(ang), w_fourier)


def _dft_angles(n, rows, cols):
    idx = (rows[:, None] * cols[None, :]) % n
    return idx.astype(F32) * (2.0 * math.pi / n)


def _mixin_kernel(x_ref, sh_ref, sc_ref, w_ref, ab_ref, cos_ref, sin_ref,
                  pab_ref, q_ref, k_ref, v_ref, *, q_scale):
    hc = HEAD_DIM
    dm = w_ref.shape[1] // 4
    groups = heads = dm // hc
    h = (_rms(x_ref[...]) * (1.0 + sc_ref[0]) + sh_ref[0]).astype(BF16)

    def project(n):
        return jnp.dot(h, w_ref[:, n * dm:(n + 1) * dm], preferred_element_type=F32)

    def rope_to(dst_ref, p, scale):
        cos = cos_ref[...] * scale
        sin = sin_ref[...] * scale
        for hh in range(heads):
            t = p[:, hh * hc:(hh + 1) * hc]
            dst_ref[0, hh] = t * cos + pltpu.roll(t, hc // 2, 1) * sin

    u = project(0).astype(BF16)
    pq = project(1)
    for g in range(groups):
        r = jnp.dot(u[:, g * hc:(g + 1) * hc], ab_ref[g], preferred_element_type=F32)
        pab_ref[:, g * hc:(g + 1) * hc] = r[:, :hc].astype(BF16)
        pab_ref[:, (groups + g) * hc:(groups + g + 1) * hc] = r[:, hc:].astype(BF16)
    rope_to(q_ref, pq, q_scale)
    rope_to(k_ref, project(2), 1.0)
    pv = project(3)
    for hh in range(heads):
        v_ref[0, hh] = pv[:, hh * hc:(hh + 1) * hc]


def _mixin(x, sh, sc, w_in, ab, cos, sin, *, layer, batch, seq, tm):
    t, d = x.shape
    dm = w_in.shape[2] // 4
    heads = dm // HEAD_DIM
    tpb = seq // tm
    vec = pl.BlockSpec((1, 1, d), lambda i: (i // tpb, 0, 0))
    tab = pl.BlockSpec((tm, HEAD_DIM), lambda i: (i % tpb, 0))
    hm = pl.BlockSpec((1, heads, tm, HEAD_DIM), lambda i: (i // tpb, 0, i % tpb, 0))
    hm_shape = jax.ShapeDtypeStruct((batch, heads, seq, HEAD_DIM), F32)
    resident = pl.Buffered(1)
    return pl.pallas_call(
        functools.partial(_mixin_kernel, q_scale=HEAD_DIM ** -0.5 * math.log2(math.e)),
        out_shape=(jax.ShapeDtypeStruct((t, 2 * dm), BF16),
                   hm_shape, hm_shape, hm_shape),
        grid=(t // tm,),
        in_specs=[
            pl.BlockSpec((tm, d), lambda i: (i, 0)),
            vec, vec,
            pl.BlockSpec((None, d, 4 * dm), lambda i: (layer, 0, 0), pipeline_mode=resident),
            pl.BlockSpec((None,) + ab.shape[1:], lambda i: (layer, 0, 0, 0),
                         pipeline_mode=resident),
            tab, tab,
        ],
        out_specs=(pl.BlockSpec((tm, 2 * dm), lambda i: (i, 0)), hm, hm, hm),
        compiler_params=_compiler_params(("parallel",)),
        name="mix_in",
    )(x, sh, sc, w_in, ab, cos, sin)


def _rope_tables(seq):
    half = HEAD_DIM // 2
    inv_freq = ROPE_THETA ** (-jnp.arange(half, dtype=F32) / half)
    ang = jnp.arange(seq, dtype=F32)[:, None] * inv_freq[None, :]
    cos, sin = jnp.cos(ang), jnp.sin(ang)
    return (jnp.concatenate([cos, cos], axis=-1),
            jnp.concatenate([-sin, sin], axis=-1))


DFT_RADIX = 4


def _dft_fold_kernel(z0_ref, z1_ref, z2_ref, z3_ref, w_ref):
    dm = w_ref.shape[3] // 2
    a = [z[0, 0, :, :dm].astype(F32) for z in (z0_ref, z1_ref, z2_ref, z3_ref)]
    p = [z[0, 0, :, dm:].astype(F32) for z in (z0_ref, z1_ref, z2_ref, z3_ref)]
    t0r, t0i = a[0] + a[2], -(p[0] + p[2])
    t1r, t1i = a[0] - a[2], p[2] - p[0]
    t2r, t2i = a[1] + a[3], -(p[1] + p[3])
    t3r, t3i = a[1] - a[3], p[3] - p[1]
    parts = ((t0r + t2r, t0i + t2i),
             (t1r + t3i, t1i - t3r),
             (t0r - t2r, t0i - t2i),
             (t1r - t3i, t1i + t3r))
    for r, (re, im) in enumerate(parts):
        w_ref[0, r, :, :dm] = re.astype(BF16)
        w_ref[0, r, :, dm:] = im.astype(BF16)


def _dft_fold(pab, *, batch, seq, tr):
    width = pab.shape[-1]
    nq = seq // DFT_RADIX
    z = pab.reshape(batch, DFT_RADIX, nq, width)
    quarter = [pl.BlockSpec((1, 1, tr, width), lambda b, i, q=q: (b, q, i, 0))
               for q in range(DFT_RADIX)]
    return pl.pallas_call(
        _dft_fold_kernel,
        out_shape=jax.ShapeDtypeStruct((batch, DFT_RADIX, nq, width), BF16),
        grid=(batch, nq // tr),
        in_specs=quarter,
        out_specs=pl.BlockSpec((1, DFT_RADIX, tr, width), lambda b, i: (b, 0, i, 0)),
        compiler_params=_compiler_params(("parallel", "parallel")),
        name="dft_fold",
    )(z, z, z, z)


def _dft_kernel(ca_ref, sa_ref, cb_ref, sb_ref, wre_ref, wim_ref, o_ref,
                cmat, smat, stage):
    r = pl.program_id(3)
    first = jnp.logical_and(pl.program_id(1) == 0, pl.program_id(2) == 0)

    @pl.when(first)
    def _():
        cb = cb_ref[0]
        sb = sb_ref[0]
        for a in range(ca_ref.shape[2]):
            ca = ca_ref[0, :, a:a + 1]
            sa = sa_ref[0, :, a:a + 1]
            cols = slice(a * LANES, (a + 1) * LANES)
            cmat[r, :, cols] = (ca * cb - sa * sb).astype(BF16)
            smat[r, :, cols] = (sa * cb + ca * sb).astype(BF16)

    y = (jnp.dot(cmat[r], wre_ref[0, 0], preferred_element_type=F32)
         + jnp.dot(smat[r], wim_ref[0, 0], preferred_element_type=F32))
    for cc in range(stage.shape[0]):
        stage[cc, pl.ds(r, y.shape[0], stride=DFT_RADIX), :] = (
            y[:, cc * LANES:(cc + 1) * LANES])

    @pl.when(r == DFT_RADIX - 1)
    def _():
        for cc in range(stage.shape[0]):
            o_ref[0, :, cc * LANES:(cc + 1) * LANES] = stage[cc]


def _dft(w, tables, *, batch, seq, tm, tn):
    dm = w.shape[-1] // 2
    nq = seq // DFT_RADIX
    ncb = dm // tn
    ca, sa, cb, sb = tables
    coarse = pl.BlockSpec((1, tm, ca.shape[2]), lambda i, b, c, r: (r, i, 0))
    fine = pl.BlockSpec((1, tm, LANES), lambda i, b, c, r: (r, i, 0))
    twiddle = pltpu.VMEM((DFT_RADIX, tm, nq), BF16)
    return pl.pallas_call(
        _dft_kernel,
        out_shape=jax.ShapeDtypeStruct((batch, seq, dm), F32),
        grid=(nq // tm, batch, ncb, DFT_RADIX),
        in_specs=[
            coarse, coarse, fine, fine,
            pl.BlockSpec((1, 1, nq, tn), lambda i, b, c, r: (b, r, 0, c)),
            pl.BlockSpec((1, 1, nq, tn), lambda i, b, c, r: (b, r, 0, c + ncb)),
        ],
        out_specs=pl.BlockSpec((1, DFT_RADIX * tm, tn), lambda i, b, c, r: (b, i, c)),
        scratch_shapes=[twiddle, twiddle,
                        pltpu.VMEM((tn // LANES, DFT_RADIX * tm, LANES), F32)],
        compiler_params=_compiler_params(
            ("parallel", "arbitrary", "arbitrary", "arbitrary")),
        name="dft",
    )(ca, sa, cb, sb, w, w)


def _dft_tables(seq):
    nq = seq // DFT_RADIX
    k = (DFT_RADIX * jnp.arange(nq)[None, :] + jnp.arange(DFT_RADIX)[:, None]).reshape(-1)
    coarse = _dft_angles(seq, k, LANES * jnp.arange(nq // LANES))
    fine = _dft_angles(seq, k, jnp.arange(LANES))
    return tuple(f(t).reshape(DFT_RADIX, nq, -1)
                 for t in (coarse, fine) for f in (jnp.cos, jnp.sin))


def _attn_kernel(q_ref, k_ref, v_ref, bias_ref, o_ref,
                 qp, kp, vp, s_buf, p_buf, pv, mb, lb):
    seq, hd = q_ref.shape[2], q_ref.shape[3]
    bq, kw = ATTN_BQ, ATTN_KW
    nblk = seq // bq

    assert kw == 2 * hd and s_buf.shape == (2, seq, hd)
    for src_ref, dst in ((q_ref, qp), (k_ref, kp), (v_ref, vp)):
        def load_prev(start, size, stride, src_ref=src_ref):
            return src_ref[0, 0, pl.ds(start, size, stride=stride), :]
        d_prev = 1
        for bi, d in enumerate(DILATIONS):
            ratio, sub, sub_prev = d // d_prev, seq // d, seq // d_prev
            piece = min(sub, ATTN_COPY_ROWS)
            for jp in range(d_prev):
                for r in range(ratio):
                    for c in range(sub // piece):
                        row0 = (jp + d_prev * r) * sub + c * piece
                        rows = slice(row0, row0 + piece)
                        x = load_prev(jp * sub_prev + r + c * piece * ratio, piece, ratio)
                        dst[bi, rows, :] = x.astype(BF16)
                        if 0 < bi < len(DILATIONS) - 1:
                            s_buf[bi % 2, rows, :] = x
            if d == 1:
                continue

            def load_prev(start, size, stride, half=bi % 2):
                return s_buf[half, pl.ds(start, size, stride=stride), :]
            d_prev = d

    ones = jnp.ones((kw, hd), BF16)

    for bi, d in enumerate(DILATIONS):
        sub = seq // d
        bps = sub // bq

        def place(n, d=d, sub=sub, bps=bps):
            j = n // bps
            t0 = (n % bps) * bq
            ks = jnp.clip(t0 - REACH, 0, sub - kw)
            qrows = pl.ds(pl.multiple_of(j * sub + t0, bq), bq)
            krows = pl.ds(pl.multiple_of(j * sub + ks, REACH), kw)
            if d > 1:
                orows = pl.ds(j + d * t0, bq, stride=d)
            else:
                orows = pl.ds(pl.multiple_of(t0, bq), bq)
            return qrows, krows, orows, (t0 - ks) // REACH

        def scores(n, carry, place=place, bi=bi):
            qrows, krows, _, edge = place(n)
            s = lax.dot_general(qp[bi, qrows, :], kp[bi, krows, :],
                                (((1,), (1,)), ((), ())), preferred_element_type=F32)
            s = s + bias_ref[edge]
            rows = pl.ds(pl.multiple_of(n * bq, bq), bq)
            s_buf[0, rows, :] = s[:, :hd]
            s_buf[1, rows, :] = s[:, hd:]
            return carry

        def softmax(n, carry, place=place, bi=bi):
            _, _, orows, _ = place(n)
            rows = pl.ds(pl.multiple_of(n * bq, bq), bq)
            s = jnp.concatenate([s_buf[0, rows, :], s_buf[1, rows, :]], axis=1)
            m = jnp.max(s, axis=-1, keepdims=True)
            p_buf[n] = jnp.exp2(s - m).astype(BF16)
            mb[bi, orows, :] = jnp.broadcast_to(m, (bq, hd))
            return carry

        def values(n, carry, place=place, bi=bi):
            _, krows, orows, _ = place(n)
            v1 = jnp.concatenate([vp[bi, krows, :], ones], axis=1)
            r = jnp.dot(p_buf[n], v1, preferred_element_type=F32)
            pv[bi, orows, :] = r[:, :hd]
            lb[bi, orows, :] = r[:, hd:]
            return carry

        lax.fori_loop(0, nblk, scores, 0, unroll=ATTN_UNROLL)
        lax.fori_loop(0, nblk, softmax, 0, unroll=ATTN_UNROLL)
        lax.fori_loop(0, nblk, values, 0, unroll=ATTN_UNROLL)

    def merge(c, carry):
        rows = pl.ds(pl.multiple_of(c * bq, bq), bq)
        branches = range(len(DILATIONS))
        ms = [mb[p, rows, :] for p in branches]
        top = functools.reduce(jnp.maximum, ms)
        es = [jnp.exp2(m - top) for m in ms]
        num = sum(es[p] * pv[p, rows, :] for p in branches)
        den = sum(es[p] * lb[p, rows, :] for p in branches)
        o_ref[0, rows, :] = num / den
        return carry

    lax.fori_loop(0, nblk, merge, 0, unroll=2)


def _attn_bias():
    r = jnp.arange(ATTN_BQ)[:, None]
    c = jnp.arange(ATTN_KW)[None, :]
    return jnp.stack([
        jnp.where(jnp.abs(edge * REACH + r - c) <= REACH, 0.0, MASK_VALUE).astype(F32)
        for edge in range(3)])


def _attn(q, k, v):
    batch, heads, seq, hd = q.shape
    nb = len(DILATIONS)
    nblk = seq // ATTN_BQ
    bias = _attn_bias()
    blk = pl.BlockSpec((1, 1, seq, hd), lambda b, h: (b, h, 0, 0))
    stat = pltpu.VMEM((nb, seq, hd), F32)
    gathered = pltpu.VMEM((nb, seq, hd), BF16)
    return pl.pallas_call(
        _attn_kernel,
        out_shape=jax.ShapeDtypeStruct((batch, seq, heads * hd), F32),
        grid=(batch, heads),
        in_specs=[blk, blk, blk, pl.BlockSpec(bias.shape, lambda b, h: (0, 0, 0))],
        out_specs=pl.BlockSpec((1, seq, hd), lambda b, h: (b, 0, h)),
        scratch_shapes=[
            gathered, gathered, gathered,
            pltpu.VMEM((ATTN_KW // hd, seq, hd), F32),
            pltpu.VMEM((nblk, ATTN_BQ, ATTN_KW), BF16),
            stat, stat, stat,
        ],
        compiler_params=_compiler_params(("parallel", "parallel"), ATTN_VMEM_LIMIT_BYTES),
        name="attn",
    )(q, k, v, bias)


def _mixout_kernel(x_ref, yf_ref, ya_ref, gf_ref, ga_ref, gt_ref, w_ref, o_ref):
    df = yf_ref.shape[1]
    nf = (_rms(yf_ref[...]) * gf_ref[...]).astype(BF16)
    na = (_rms(ya_ref[...]) * ga_ref[...]).astype(BF16)
    out = (jnp.dot(nf, w_ref[:df, :], preferred_element_type=F32)
           + jnp.dot(na, w_ref[df:, :], preferred_element_type=F32))
    o_ref[...] = x_ref[...] + gt_ref[0] * out


def _mixout(x, yf, ya, gf, ga, gt, w_out, *, layer, seq, tm):
    t, d = x.shape
    df, da = yf.shape[1], ya.shape[1]
    tpb = seq // tm
    return pl.pallas_call(
        _mixout_kernel,
        out_shape=jax.ShapeDtypeStruct((t, d), F32),
        grid=(t // tm,),
        in_specs=[
            pl.BlockSpec((tm, d), lambda i: (i, 0)),
            pl.BlockSpec((tm, df), lambda i: (i, 0)),
            pl.BlockSpec((tm, da), lambda i: (i, 0)),
            pl.BlockSpec((1, df), lambda i: (0, 0)),
            pl.BlockSpec((1, da), lambda i: (0, 0)),
            pl.BlockSpec((1, 1, d), lambda i: (i // tpb, 0, 0)),
            pl.BlockSpec((None, df + da, d), lambda i: (layer, 0, 0)),
        ],
        out_specs=pl.BlockSpec((tm, d), lambda i: (i, 0)),
        compiler_params=_compiler_params(("parallel",)),
        name="mix_out",
    )(x, yf, ya, gf, ga, gt, w_out)


class _Tiles(NamedTuple):
    ffn_rows: int
    ffn_cols: int
    mix_rows: int
    fold_rows: int
    dft_rows: int
    dft_cols: int
    ada_cols: int


def _tiles(seq, d, d_ff):
    ffn_cols = 512 if d_ff % 512 == 0 else LANES
    return _Tiles(ffn_rows=min(512, seq), ffn_cols=ffn_cols, mix_rows=min(512, seq),
                  fold_rows=min(256, seq // DFT_RADIX),
                  dft_rows=min(512, seq // DFT_RADIX), dft_cols=min(1024, d // 2), ada_cols=min(1024, d))


def kernel(x, c, w_ada, b_ada, w_ffn1_gu, w_ffn1_down, w_mix_in, w_fourier,
           g_fourier_out, g_attn_out, w_mix_out, w_ffn2_gu, w_ffn2_down, g_final):
    batch, seq, d = x.shape
    depth = w_ada.shape[0]
    d_ff = w_ffn1_down.shape[1]
    t = batch * seq
    tl = _tiles(seq, d, d_ff)

    rows = 8 * pl.cdiv(batch, 8)
    c_pad = jnp.zeros((rows, d), F32).at[:batch].set(c)
    mod = _ada(c_pad, w_ada, b_ada, tn=tl.ada_cols)
    mod = mod[:, :batch].reshape(depth, batch, N_MOD, 1, d)

    groups = w_fourier.shape[1]
    ab = _fprep(w_fourier.reshape(depth * groups, HEAD_DIM, HEAD_DIM), seq)
    ab = ab.reshape(depth, groups, HEAD_DIM, 2 * HEAD_DIM)
    rope_cos, rope_sin = _rope_tables(seq)
    dft_tables = _dft_tables(seq)
    g_fin = g_final.reshape(1, d)

    wgu1, wd1, wgu2, wd2, w_in, w_out = (
        w.astype(BF16) for w in (w_ffn1_gu, w_ffn1_down, w_ffn2_gu, w_ffn2_down,
                                 w_mix_in, w_mix_out))

    xt = x.reshape(t, d)
    for l in range(depth):
        sh1, sc1, g1, sh2, sc2, g2, sh3, sc3, g3 = (mod[l, :, i] for i in range(N_MOD))
        xt = _ffn(xt, sh1, sc1, g1, wgu1, wd1, g_fin, layer=l,
                  seq=seq, tm=tl.ffn_rows, tf=tl.ffn_cols, final_norm=False)
        pab, q, k, v = _mixin(xt, sh2, sc2, w_in, ab, rope_cos, rope_sin, layer=l,
                              batch=batch, seq=seq, tm=tl.mix_rows)
        w = _dft_fold(pab, batch=batch, seq=seq, tr=tl.fold_rows)
        yf = _dft(w, dft_tables, batch=batch, seq=seq, tm=tl.dft_rows, tn=tl.dft_cols)
        ya = _attn(q, k, v)
        xt = _mixout(xt, yf.reshape(t, -1), ya.reshape(t, -1),
                     g_fourier_out[l].reshape(1, -1), g_attn_out[l].reshape(1, -1),
                     g2, w_out, layer=l, seq=seq, tm=tl.mix_rows)
        xt = _ffn(xt, sh3, sc3, g3, wgu2, wd2, g_fin, layer=l,
                  seq=seq, tm=tl.ffn_rows, tf=tl.ffn_cols, final_norm=(l == depth - 1))
    return xt.reshape(batch, seq, d)
```

```python
import functools
import math
from typing import NamedTuple

import jax
import jax.numpy as jnp
from jax import lax
from jax.experimental import pallas as pl
from jax.experimental.pallas import tpu as pltpu

F32 = jnp.float32
BF16 = jnp.bfloat16

EPS = 1e-6
HEAD_DIM = 128
ROPE_THETA = 10000.0
DILATED_PATTERNS = ((128, 1), (512, 4), (2048, 16))
DILATIONS = tuple(d for _, d in DILATED_PATTERNS)
REACH = (DILATED_PATTERNS[0][0] // 2) // DILATED_PATTERNS[0][1]
assert all((w // 2) // d == REACH for w, d in DILATED_PATTERNS)
MASK_VALUE = -1e30
N_MOD = 9

LANES = 128
VMEM_LIMIT_BYTES = 56 * 1024 * 1024

FFN_COL_SPLIT = 2
FFN_SLOTS = 3

ATTN_BQ = 128
ATTN_KW = ATTN_BQ + 2 * REACH
ATTN_COPY_ROWS = 128
ATTN_UNROLL = True
ATTN_VMEM_LIMIT_BYTES = 60 * 1024 * 1024


def _compiler_params(semantics, vmem_limit_bytes=VMEM_LIMIT_BYTES):
    return pltpu.CompilerParams(dimension_semantics=semantics,
                                vmem_limit_bytes=vmem_limit_bytes)


def _rms(x):
    return x * lax.rsqrt(jnp.mean(x * x, axis=-1, keepdims=True) + EPS)


def _silu(x):
    return x * jax.nn.sigmoid(x)


def _ada_kernel(c_ref, w_ref, b_ref, o_ref):
    ca = _silu(c_ref[...]).astype(BF16)
    o_ref[0] = jnp.dot(ca, w_ref[0].astype(BF16),
                       preferred_element_type=F32) + b_ref[0]


def _ada(c_pad, w_ada, b_ada, *, tn):
    depth, d, n = w_ada.shape
    rows = c_pad.shape[0]
    return pl.pallas_call(
        _ada_kernel,
        out_shape=jax.ShapeDtypeStruct((depth, rows, n), F32),
        grid=(depth, n // tn),
        in_specs=[
            pl.BlockSpec((rows, d), lambda l, j: (0, 0)),
            pl.BlockSpec((1, d, tn), lambda l, j: (l, 0, j)),
            pl.BlockSpec((1, 1, tn), lambda l, j: (l, 0, j)),
        ],
        out_specs=pl.BlockSpec((1, rows, tn), lambda l, j: (l, 0, j)),
        compiler_params=_compiler_params(("parallel", "parallel")),
        name="ada",
    )(c_pad, w_ada, b_ada.reshape(depth, 1, n))


def _ffn_kernel(x_ref, sh_ref, sc_ref, gt_ref, wg0_ref, wu0_ref, wd0_ref, wgu_hbm, wd_hbm,
                gf_ref, o_ref, wg_buf, wu_buf, wd_buf, sems, *, layer, final_norm):
    d, tf = wg0_ref.shape
    nj = wd_hbm.shape[1] // tf
    part = tf // FFN_COL_SPLIT

    def fetch(j):
        slot = j % FFN_SLOTS
        return (
            pltpu.make_async_copy(wgu_hbm.at[layer, j], wg_buf.at[slot], sems.at[0, slot]),
            pltpu.make_async_copy(wgu_hbm.at[layer, nj + j], wu_buf.at[slot], sems.at[1, slot]),
            pltpu.make_async_copy(wd_hbm.at[layer, pl.ds(j * tf, tf), :],
                                  wd_buf.at[slot], sems.at[2, slot]),
        )

    for j in range(1, min(FFN_SLOTS, nj)):
        for c in fetch(j):
            c.start()
    h = (_rms(x_ref[...]) * (1.0 + sc_ref[0]) + sh_ref[0]).astype(BF16)

    for j in range(nj):
        if j == 0:
            wg, wu, wd = wg0_ref, wu0_ref, wd0_ref
        else:
            slot = j % FFN_SLOTS
            for c in fetch(j):
                c.wait()
            if j + FFN_SLOTS - 1 < nj:
                for c in fetch(j + FFN_SLOTS - 1):
                    c.start()
            wg, wu, wd = wg_buf.at[slot], wu_buf.at[slot], wd_buf.at[slot]
        gu = []
        for s in range(FFN_COL_SPLIT):
            cols = slice(s * part, (s + 1) * part)
            gu.append((jnp.dot(h, wg[:, cols], preferred_element_type=F32),
                       jnp.dot(h, wu[:, cols], preferred_element_type=F32)))
        for s, (g, u) in enumerate(gu):
            a = (_silu(g) * u).astype(BF16)
            r = jnp.dot(a, wd[s * part:(s + 1) * part, :], preferred_element_type=F32)
            if j == 0 and s == 0:
                o_ref[...] = r
            else:
                o_ref[...] += r

    y = x_ref[...] + (0.5 * gt_ref[0]) * o_ref[...]
    if final_norm:
        y = _rms(y) * gf_ref[...]
    o_ref[...] = y


def _ffn(x, sh, sc, gt, w_gu, w_down, g_final, *, layer, seq, tm, final_norm):
    t, d = x.shape
    nj, tf = w_gu.shape[1] // 2, w_gu.shape[3]
    tpb = seq // tm
    vec = pl.BlockSpec((1, 1, d), lambda i: (i // tpb, 0, 0))
    resident = pl.Buffered(1)
    return pl.pallas_call(
        functools.partial(_ffn_kernel, layer=layer, final_norm=final_norm),
        out_shape=jax.ShapeDtypeStruct((t, d), F32),
        grid=(t // tm,),
        in_specs=[
            pl.BlockSpec((tm, d), lambda i: (i, 0)),
            vec, vec, vec,
            pl.BlockSpec((None, None, d, tf), lambda i: (layer, 0, 0, 0), pipeline_mode=resident),
            pl.BlockSpec((None, None, d, tf), lambda i: (layer, nj, 0, 0), pipeline_mode=resident),
            pl.BlockSpec((None, tf, d), lambda i: (layer, 0, 0), pipeline_mode=resident),
            pl.BlockSpec(memory_space=pl.ANY),
            pl.BlockSpec(memory_space=pl.ANY),
            pl.BlockSpec((1, d), lambda i: (0, 0)),
        ],
        out_specs=pl.BlockSpec((tm, d), lambda i: (i, 0)),
        scratch_shapes=[
            pltpu.VMEM((FFN_SLOTS, d, tf), BF16),
            pltpu.VMEM((FFN_SLOTS, d, tf), BF16),
            pltpu.VMEM((FFN_SLOTS, tf, d), BF16),
            pltpu.SemaphoreType.DMA((3, FFN_SLOTS)),
        ],
        compiler_params=_compiler_params(("parallel",)),
        name="ffn",
    )(x, sh, sc, gt, w_gu, w_gu, w_down, w_gu, w_down, g_final)


def _fprep_kernel(cc_ref, sc_ref, w_ref, ab_ref, *, scale):
    w = w_ref[0]
    hc = w.shape[0]
    a = jnp.dot(cc_ref[...], w, preferred_element_type=F32,
                precision=lax.Precision.HIGHEST)
    b = jnp.dot(sc_ref[...], w, preferred_element_type=F32,
                precision=lax.Precision.HIGHEST)
    ab_ref[0, :, :hc] = (a * scale).astype(BF16)
    ab_ref[0, :, hc:] = (b * scale).astype(BF16)


def _fprep(w_fourier, seq):
    n, hc, _ = w_fourier.shape
    ang = _dft_angles(hc, jnp.arange(hc), jnp.arange(hc))
    scale = 1.0 / math.sqrt(seq * hc)
    sq = pl.BlockSpec((hc, hc), lambda g: (0, 0))
    return pl.pallas_call(
        functools.partial(_fprep_kernel, scale=scale),
        out_shape=jax.ShapeDtypeStruct((n, hc, 2 * hc), BF16),
        grid=(n,),
        in_specs=[sq, sq, pl.BlockSpec((1, hc, hc), lambda g: (g, 0, 0))],
        out_specs=pl.BlockSpec((1, hc, 2 * hc), lambda g: (g, 0, 0)),
        compiler_params=_compiler_params(("parallel",)),
        name="fprep",
    )(jnp.cos(ang), jnp.sin(ang), w_fourier)


def _dft_angles(n, rows, cols):
    idx = (rows[:, None] * cols[None, :]) % n
    return idx.astype(F32) * (2.0 * math.pi / n)


def _mixin_kernel(x_ref, sh_ref, sc_ref, w_ref, ab_ref, cos_ref, sin_ref,
                  pab_ref, q_ref, k_ref, v_ref, *, q_scale):
    hc = HEAD_DIM
    dm = w_ref.shape[1] // 4
    groups = heads = dm // hc
    h = (_rms(x_ref[...]) * (1.0 + sc_ref[0]) + sh_ref[0]).astype(BF16)

    def project(n):
        return jnp.dot(h, w_ref[:, n * dm:(n + 1) * dm], preferred_element_type=F32)

    def rope_to(dst_ref, p, scale):
        cos = cos_ref[...] * scale
        sin = sin_ref[...] * scale
        for hh in range(heads):
            t = p[:, hh * hc:(hh + 1) * hc]
            dst_ref[0, hh] = t * cos + pltpu.roll(t, hc // 2, 1) * sin

    u = project(0).astype(BF16)
    pq = project(1)
    for g in range(groups):
        r = jnp.dot(u[:, g * hc:(g + 1) * hc], ab_ref[g], preferred_element_type=F32)
        pab_ref[:, g * hc:(g + 1) * hc] = r[:, :hc].astype(BF16)
        pab_ref[:, (groups + g) * hc:(groups + g + 1) * hc] = r[:, hc:].astype(BF16)
    rope_to(q_ref, pq, q_scale)
    rope_to(k_ref, project(2), 1.0)
    pv = project(3)
    for hh in range(heads):
        v_ref[0, hh] = pv[:, hh * hc:(hh + 1) * hc]


def _mixin(x, sh, sc, w_in, ab, cos, sin, *, layer, batch, seq, tm):
    t, d = x.shape
    dm = w_in.shape[2] // 4
    heads = dm // HEAD_DIM
    tpb = seq // tm
    vec = pl.BlockSpec((1, 1, d), lambda i: (i // tpb, 0, 0))
    tab = pl.BlockSpec((tm, HEAD_DIM), lambda i: (i % tpb, 0))
    hm = pl.BlockSpec((1, heads, tm, HEAD_DIM), lambda i: (i // tpb, 0, i % tpb, 0))
    hm_shape = jax.ShapeDtypeStruct((batch, heads, seq, HEAD_DIM), F32)
    resident = pl.Buffered(1)
    return pl.pallas_call(
        functools.partial(_mixin_kernel, q_scale=HEAD_DIM ** -0.5 * math.log2(math.e)),
        out_shape=(jax.ShapeDtypeStruct((t, 2 * dm), BF16),
                   hm_shape, hm_shape, hm_shape),
        grid=(t // tm,),
        in_specs=[
            pl.BlockSpec((tm, d), lambda i: (i, 0)),
            vec, vec,
            pl.BlockSpec((None, d, 4 * dm), lambda i: (layer, 0, 0), pipeline_mode=resident),
            pl.BlockSpec((None,) + ab.shape[1:], lambda i: (layer, 0, 0, 0),
                         pipeline_mode=resident),
            tab, tab,
        ],
        out_specs=(pl.BlockSpec((tm, 2 * dm), lambda i: (i, 0)), hm, hm, hm),
        compiler_params=_compiler_params(("parallel",)),
        name="mix_in",
    )(x, sh, sc, w_in, ab, cos, sin)


def _rope_tables(seq):
    half = HEAD_DIM // 2
    inv_freq = ROPE_THETA ** (-jnp.arange(half, dtype=F32) / half)
    ang = jnp.arange(seq, dtype=F32)[:, None] * inv_freq[None, :]
    cos, sin = jnp.cos(ang), jnp.sin(ang)
    return (jnp.concatenate([cos, cos], axis=-1),
            jnp.concatenate([-sin, sin], axis=-1))


DFT_RADIX = 4


def _dft_fold_kernel(z0_ref, z1_ref, z2_ref, z3_ref, w_ref):
    dm = w_ref.shape[3] // 2
    a = [z[0, 0, :, :dm].astype(F32) for z in (z0_ref, z1_ref, z2_ref, z3_ref)]
    p = [z[0, 0, :, dm:].astype(F32) for z in (z0_ref, z1_ref, z2_ref, z3_ref)]
    t0r, t0i = a[0] + a[2], -(p[0] + p[2])
    t1r, t1i = a[0] - a[2], p[2] - p[0]
    t2r, t2i = a[1] + a[3], -(p[1] + p[3])
    t3r, t3i = a[1] - a[3], p[3] - p[1]
    parts = ((t0r + t2r, t0i + t2i),
             (t1r + t3i, t1i - t3r),
             (t0r - t2r, t0i - t2i),
             (t1r - t3i, t1i + t3r))
    for r, (re, im) in enumerate(parts):
        w_ref[0, r, :, :dm] = re.astype(BF16)
        w_ref[0, r, :, dm:] = im.astype(BF16)


def _dft_fold(pab, *, batch, seq, tr):
    width = pab.shape[-1]
    nq = seq // DFT_RADIX
    z = pab.reshape(batch, DFT_RADIX, nq, width)
    quarter = [pl.BlockSpec((1, 1, tr, width), lambda b, i, q=q: (b, q, i, 0))
               for q in range(DFT_RADIX)]
    return pl.pallas_call(
        _dft_fold_kernel,
        out_shape=jax.ShapeDtypeStruct((batch, DFT_RADIX, nq, width), BF16),
        grid=(batch, nq // tr),
        in_specs=quarter,
        out_specs=pl.BlockSpec((1, DFT_RADIX, tr, width), lambda b, i: (b, 0, i, 0)),
        compiler_params=_compiler_params(("parallel", "parallel")),
        name="dft_fold",
    )(z, z, z, z)


def _dft_kernel(ca_ref, sa_ref, cb_ref, sb_ref, wre_ref, wim_ref, o_ref,
                cmat, smat, stage):
    r = pl.program_id(3)
    first = jnp.logical_and(pl.program_id(1) == 0, pl.program_id(2) == 0)

    @pl.when(first)
    def _():
        cb = cb_ref[0]
        sb = sb_ref[0]
        for a in range(ca_ref.shape[2]):
            ca = ca_ref[0, :, a:a + 1]
            sa = sa_ref[0, :, a:a + 1]
            cols = slice(a * LANES, (a + 1) * LANES)
            cmat[r, :, cols] = (ca * cb - sa * sb).astype(BF16)
            smat[r, :, cols] = (sa * cb + ca * sb).astype(BF16)

    y = (jnp.dot(cmat[r], wre_ref[0, 0], preferred_element_type=F32)
         + jnp.dot(smat[r], wim_ref[0, 0], preferred_element_type=F32))
    for cc in range(stage.shape[0]):
        stage[cc, pl.ds(r, y.shape[0], stride=DFT_RADIX), :] = (
            y[:, cc * LANES:(cc + 1) * LANES])

    @pl.when(r == DFT_RADIX - 1)
    def _():
        for cc in range(stage.shape[0]):
            o_ref[0, :, cc * LANES:(cc + 1) * LANES] = stage[cc]


def _dft(w, tables, *, batch, seq, tm, tn):
    dm = w.shape[-1] // 2
    nq = seq // DFT_RADIX
    ncb = dm // tn
    ca, sa, cb, sb = tables
    coarse = pl.BlockSpec((1, tm, ca.shape[2]), lambda i, b, c, r: (r, i, 0))
    fine = pl.BlockSpec((1, tm, LANES), lambda i, b, c, r: (r, i, 0))
    twiddle = pltpu.VMEM((DFT_RADIX, tm, nq), BF16)
    return pl.pallas_call(
        _dft_kernel,
        out_shape=jax.ShapeDtypeStruct((batch, seq, dm), F32),
        grid=(nq // tm, batch, ncb, DFT_RADIX),
        in_specs=[
            coarse, coarse, fine, fine,
            pl.BlockSpec((1, 1, nq, tn), lambda i, b, c, r: (b, r, 0, c)),
            pl.BlockSpec((1, 1, nq, tn), lambda i, b, c, r: (b, r, 0, c + ncb)),
        ],
        out_specs=pl.BlockSpec((1, DFT_RADIX * tm, tn), lambda i, b, c, r: (b, i, c)),
        scratch_shapes=[twiddle, twiddle,
                        pltpu.VMEM((tn // LANES, DFT_RADIX * tm, LANES), F32)],
        compiler_params=_compiler_params(
            ("parallel", "arbitrary", "arbitrary", "arbitrary")),
        name="dft",
    )(ca, sa, cb, sb, w, w)


def _dft_tables(seq):
    nq = seq // DFT_RADIX
    k = (DFT_RADIX * jnp.arange(nq)[None, :] + jnp.arange(DFT_RADIX)[:, None]).reshape(-1)
    coarse = _dft_angles(seq, k, LANES * jnp.arange(nq // LANES))
    fine = _dft_angles(seq, k, jnp.arange(LANES))
    return tuple(f(t).reshape(DFT_RADIX, nq, -1)
                 for t in (coarse, fine) for f in (jnp.cos, jnp.sin))


def _attn_kernel(q_ref, k_ref, v_ref, bias_ref, o_ref,
                 qp, kp, vp, s_buf, p_buf, pv, mb, lb):
    seq, hd = q_ref.shape[2], q_ref.shape[3]
    bq, kw = ATTN_BQ, ATTN_KW
    nblk = seq // bq

    assert kw == 2 * hd and s_buf.shape == (2, seq, hd)
    for src_ref, dst in ((q_ref, qp), (k_ref, kp), (v_ref, vp)):
        def load_prev(start, size, stride, src_ref=src_ref):
            return src_ref[0, 0, pl.ds(start, size, stride=stride), :]
        d_prev = 1
        for bi, d in enumerate(DILATIONS):
            ratio, sub, sub_prev = d // d_prev, seq // d, seq // d_prev
            piece = min(sub, ATTN_COPY_ROWS)
            for jp in range(d_prev):
                for r in range(ratio):
                    for c in range(sub // piece):
                        row0 = (jp + d_prev * r) * sub + c * piece
                        rows = slice(row0, row0 + piece)
                        x = load_prev(jp * sub_prev + r + c * piece * ratio, piece, ratio)
                        dst[bi, rows, :] = x.astype(BF16)
                        if 0 < bi < len(DILATIONS) - 1:
                            s_buf[bi % 2, rows, :] = x
            if d == 1:
                continue

            def load_prev(start, size, stride, half=bi % 2):
                return s_buf[half, pl.ds(start, size, stride=stride), :]
            d_prev = d

    ones = jnp.ones((kw, hd), BF16)

    for bi, d in enumerate(DILATIONS):
        sub = seq // d
        bps = sub // bq

        def place(n, d=d, sub=sub, bps=bps):
            j = n // bps
            t0 = (n % bps) * bq
            ks = jnp.clip(t0 - REACH, 0, sub - kw)
            qrows = pl.ds(pl.multiple_of(j * sub + t0, bq), bq)
            krows = pl.ds(pl.multiple_of(j * sub + ks, REACH), kw)
            if d > 1:
                orows = pl.ds(j + d * t0, bq, stride=d)
            else:
                orows = pl.ds(pl.multiple_of(t0, bq), bq)
            return qrows, krows, orows, (t0 - ks) // REACH

        def scores(n, carry, place=place, bi=bi):
            qrows, krows, _, edge = place(n)
            s = lax.dot_general(qp[bi, qrows, :], kp[bi, krows, :],
                                (((1,), (1,)), ((), ())), preferred_element_type=F32)
            s = s + bias_ref[edge]
            rows = pl.ds(pl.multiple_of(n * bq, bq), bq)
            s_buf[0, rows, :] = s[:, :hd]
            s_buf[1, rows, :] = s[:, hd:]
            return carry

        def softmax(n, carry, place=place, bi=bi):
            _, _, orows, _ = place(n)
            rows = pl.ds(pl.multiple_of(n * bq, bq), bq)
            s = jnp.concatenate([s_buf[0, rows, :], s_buf[1, rows, :]], axis=1)
            m = jnp.max(s, axis=-1, keepdims=True)
            p_buf[n] = jnp.exp2(s - m).astype(BF16)
            mb[bi, orows, :] = jnp.broadcast_to(m, (bq, hd))
            return carry

        def values(n, carry, place=place, bi=bi):
            _, krows, orows, _ = place(n)
            v1 = jnp.concatenate([vp[bi, krows, :], ones], axis=1)
            r = jnp.dot(p_buf[n], v1, preferred_element_type=F32)
            pv[bi, orows, :] = r[:, :hd]
            lb[bi, orows, :] = r[:, hd:]
            return carry

        lax.fori_loop(0, nblk, scores, 0, unroll=ATTN_UNROLL)
        lax.fori_loop(0, nblk, softmax, 0, unroll=ATTN_UNROLL)
        lax.fori_loop(0, nblk, values, 0, unroll=ATTN_UNROLL)

    def merge(c, carry):
        rows = pl.ds(pl.multiple_of(c * bq, bq), bq)
        branches = range(len(DILATIONS))
        ms = [mb[p, rows, :] for p in branches]
        top = functools.reduce(jnp.maximum, ms)
        es = [jnp.exp2(m - top) for m in ms]
        num = sum(es[p] * pv[p, rows, :] for p in branches)
        den = sum(es[p] * lb[p, rows, :] for p in branches)
        o_ref[0, rows, :] = num / den
        return carry

    lax.fori_loop(0, nblk, merge, 0, unroll=2)


def _attn_bias():
    r = jnp.arange(ATTN_BQ)[:, None]
    c = jnp.arange(ATTN_KW)[None, :]
    return jnp.stack([
        jnp.where(jnp.abs(edge * REACH + r - c) <= REACH, 0.0, MASK_VALUE).astype(F32)
        for edge in range(3)])


def _attn(q, k, v):
    batch, heads, seq, hd = q.shape
    nb = len(DILATIONS)
    nblk = seq // ATTN_BQ
    bias = _attn_bias()
    blk = pl.BlockSpec((1, 1, seq, hd), lambda b, h: (b, h, 0, 0))
    stat = pltpu.VMEM((nb, seq, hd), F32)
    gathered = pltpu.VMEM((nb, seq, hd), BF16)
    return pl.pallas_call(
        _attn_kernel,
        out_shape=jax.ShapeDtypeStruct((batch, seq, heads * hd), F32),
        grid=(batch, heads),
        in_specs=[blk, blk, blk, pl.BlockSpec(bias.shape, lambda b, h: (0, 0, 0))],
        out_specs=pl.BlockSpec((1, seq, hd), lambda b, h: (b, 0, h)),
        scratch_shapes=[
            gathered, gathered, gathered,
            pltpu.VMEM((ATTN_KW // hd, seq, hd), F32),
            pltpu.VMEM((nblk, ATTN_BQ, ATTN_KW), BF16),
            stat, stat, stat,
        ],
        compiler_params=_compiler_params(("parallel", "parallel"), ATTN_VMEM_LIMIT_BYTES),
        name="attn",
    )(q, k, v, bias)


def _mixout_kernel(x_ref, yf_ref, ya_ref, gf_ref, ga_ref, gt_ref, w_ref, o_ref):
    df = yf_ref.shape[1]
    nf = (_rms(yf_ref[...]) * gf_ref[...]).astype(BF16)
    na = (_rms(ya_ref[...]) * ga_ref[...]).astype(BF16)
    out = (jnp.dot(nf, w_ref[:df, :], preferred_element_type=F32)
           + jnp.dot(na, w_ref[df:, :], preferred_element_type=F32))
    o_ref[...] = x_ref[...] + gt_ref[0] * out


def _mixout(x, yf, ya, gf, ga, gt, w_out, *, layer, seq, tm):
    t, d = x.shape
    df, da = yf.shape[1], ya.shape[1]
    tpb = seq // tm
    return pl.pallas_call(
        _mixout_kernel,
        out_shape=jax.ShapeDtypeStruct((t, d), F32),
        grid=(t // tm,),
        in_specs=[
            pl.BlockSpec((tm, d), lambda i: (i, 0)),
            pl.BlockSpec((tm, df), lambda i: (i, 0)),
            pl.BlockSpec((tm, da), lambda i: (i, 0)),
            pl.BlockSpec((1, df), lambda i: (0, 0)),
            pl.BlockSpec((1, da), lambda i: (0, 0)),
            pl.BlockSpec((1, 1, d), lambda i: (i // tpb, 0, 0)),
            pl.BlockSpec((None, df + da, d), lambda i: (layer, 0, 0)),
        ],
        out_specs=pl.BlockSpec((tm, d), lambda i: (i, 0)),
        compiler_params=_compiler_params(("parallel",)),
        name="mix_out",
    )(x, yf, ya, gf, ga, gt, w_out)


def _column_blocks(w, tn):
    depth, d, n = w.shape
    return w.reshape(depth, d, n // tn, tn).transpose(0, 2, 1, 3)


class _Tiles(NamedTuple):
    ffn_rows: int
    ffn_cols: int
    mix_rows: int
    fold_rows: int
    dft_rows: int
    dft_cols: int
    ada_cols: int


def _tiles(seq, d, d_ff):
    ffn_cols = 512 if d_ff % 512 == 0 else LANES
    return _Tiles(ffn_rows=min(512, seq), ffn_cols=ffn_cols, mix_rows=min(512, seq),
                  fold_rows=min(256, seq // DFT_RADIX),
                  dft_rows=min(512, seq // DFT_RADIX), dft_cols=min(1024, d // 2), ada_cols=min(1024, d))


def kernel(x, c, w_ada, b_ada, w_ffn1_gu, w_ffn1_down, w_mix_in, w_fourier,
           g_fourier_out, g_attn_out, w_mix_out, w_ffn2_gu, w_ffn2_down, g_final):
    batch, seq, d = x.shape
    depth = w_ada.shape[0]
    d_ff = w_ffn1_down.shape[1]
    t = batch * seq
    tl = _tiles(seq, d, d_ff)

    rows = 8 * pl.cdiv(batch, 8)
    c_pad = jnp.zeros((rows, d), F32).at[:batch].set(c)
    mod = _ada(c_pad, w_ada, b_ada, tn=tl.ada_cols)
    mod = mod[:, :batch].reshape(depth, batch, N_MOD, 1, d)

    groups = w_fourier.shape[1]
    ab = _fprep(w_fourier.reshape(depth * groups, HEAD_DIM, HEAD_DIM), seq)
    ab = ab.reshape(depth, groups, HEAD_DIM, 2 * HEAD_DIM)
    rope_cos, rope_sin = _rope_tables(seq)
    dft_tables = _dft_tables(seq)
    g_fin = g_final.reshape(1, d)

    wd1, wd2, w_in, w_out = (
        w.astype(BF16) for w in (w_ffn1_down, w_ffn2_down, w_mix_in, w_mix_out))
    wgu1, wgu2 = (_column_blocks(w.astype(BF16), tl.ffn_cols) for w in (w_ffn1_gu, w_ffn2_gu))

    xt = x.reshape(t, d)
    for l in range(depth):
        sh1, sc1, g1, sh2, sc2, g2, sh3, sc3, g3 = (mod[l, :, i] for i in range(N_MOD))
        xt = _ffn(xt, sh1, sc1, g1, wgu1, wd1, g_fin, layer=l,
                  seq=seq, tm=tl.ffn_rows, final_norm=False)
        pab, q, k, v = _mixin(xt, sh2, sc2, w_in, ab, rope_cos, rope_sin, layer=l,
                              batch=batch, seq=seq, tm=tl.mix_rows)
        w = _dft_fold(pab, batch=batch, seq=seq, tr=tl.fold_rows)
        yf = _dft(w, dft_tables, batch=batch, seq=seq, tm=tl.dft_rows, tn=tl.dft_cols)
        ya = _attn(q, k, v)
        xt = _mixout(xt, yf.reshape(t, -1), ya.reshape(t, -1),
                     g_fourier_out[l].reshape(1, -1), g_attn_out[l].reshape(1, -1),
                     g2, w_out, layer=l, seq=seq, tm=tl.mix_rows)
        xt = _ffn(xt, sh3, sc3, g3, wgu2, wd2, g_fin, layer=l,
                  seq=seq, tm=tl.ffn_rows, final_norm=(l == depth - 1))
    return xt.reshape(batch, seq, d)
```

```python
import functools
import math
from typing import NamedTuple

import jax
import jax.numpy as jnp
from jax import lax
from jax.experimental import pallas as pl
from jax.experimental.pallas import tpu as pltpu

F32 = jnp.float32
BF16 = jnp.bfloat16

EPS = 1e-6
HEAD_DIM = 128
ROPE_THETA = 10000.0
DILATED_PATTERNS = ((128, 1), (512, 4), (2048, 16))
DILATIONS = tuple(d for _, d in DILATED_PATTERNS)
REACH = (DILATED_PATTERNS[0][0] // 2) // DILATED_PATTERNS[0][1]
assert all((w // 2) // d == REACH for w, d in DILATED_PATTERNS)
MASK_VALUE = -1e30
N_MOD = 9

LANES = 128
VMEM_LIMIT_BYTES = 56 * 1024 * 1024

FFN_COL_SPLIT = 2

ATTN_BQ = 128
ATTN_KW = ATTN_BQ + 2 * REACH
ATTN_COPY_ROWS = 128
ATTN_UNROLL = True
ATTN_VMEM_LIMIT_BYTES = 60 * 1024 * 1024


def _compiler_params(semantics, vmem_limit_bytes=VMEM_LIMIT_BYTES):
    return pltpu.CompilerParams(dimension_semantics=semantics,
                                vmem_limit_bytes=vmem_limit_bytes)


def _rms(x):
    return x * lax.rsqrt(jnp.mean(x * x, axis=-1, keepdims=True) + EPS)


def _silu(x):
    return x * jax.nn.sigmoid(x)


def _ada_kernel(c_ref, w_ref, b_ref, o_ref):
    ca = _silu(c_ref[...]).astype(BF16)
    o_ref[0] = jnp.dot(ca, w_ref[0].astype(BF16),
                       preferred_element_type=F32) + b_ref[0]


def _ada(c_pad, w_ada, b_ada, *, tn):
    depth, d, n = w_ada.shape
    rows = c_pad.shape[0]
    return pl.pallas_call(
        _ada_kernel,
        out_shape=jax.ShapeDtypeStruct((depth, rows, n), F32),
        grid=(depth, n // tn),
        in_specs=[
            pl.BlockSpec((rows, d), lambda l, j: (0, 0)),
            pl.BlockSpec((1, d, tn), lambda l, j: (l, 0, j)),
            pl.BlockSpec((1, 1, tn), lambda l, j: (l, 0, j)),
        ],
        out_specs=pl.BlockSpec((1, rows, tn), lambda l, j: (l, 0, j)),
        compiler_params=_compiler_params(("parallel", "parallel")),
        name="ada",
    )(c_pad, w_ada, b_ada.reshape(depth, 1, n))


def _ffn_kernel(x_ref, sh_ref, sc_ref, gt_ref, wg0_ref, wu0_ref, wd0_ref, wgu_hbm, wd_hbm,
                gf_ref, o_ref, wg_buf, wu_buf, wd_buf, sems, h_ref, *, layer, final_norm):
    d, tf = wg0_ref.shape
    f = wd_hbm.shape[1]
    nj = f // tf
    part = tf // FFN_COL_SPLIT

    def fetch(j):
        slot = (j - 1) % 2
        return (
            pltpu.make_async_copy(wgu_hbm.at[layer, :, pl.ds(j * tf, tf)],
                                  wg_buf.at[slot], sems.at[0, slot]),
            pltpu.make_async_copy(wgu_hbm.at[layer, :, pl.ds(f + j * tf, tf)],
                                  wu_buf.at[slot], sems.at[1, slot]),
            pltpu.make_async_copy(wd_hbm.at[layer, pl.ds(j * tf, tf), :],
                                  wd_buf.at[slot], sems.at[2, slot]),
        )

    def hidden_block(h, wg, wu, wd, first):
        gu = []
        for s in range(FFN_COL_SPLIT):
            cols = slice(s * part, (s + 1) * part)
            gu.append((jnp.dot(h, wg[:, cols], preferred_element_type=F32),
                       jnp.dot(h, wu[:, cols], preferred_element_type=F32)))
        for s, (g, u) in enumerate(gu):
            a = (_silu(g) * u).astype(BF16)
            r = jnp.dot(a, wd[s * part:(s + 1) * part, :], preferred_element_type=F32)
            if first and s == 0:
                o_ref[...] = r
            else:
                o_ref[...] += r

    def streamed_block(j, slot, prefetch):
        for c in fetch(j):
            c.wait()
        if prefetch:
            for c in fetch(j + 1):
                c.start()
        hidden_block(h_ref[...], wg_buf.at[slot], wu_buf.at[slot], wd_buf.at[slot], False)

    if nj > 1:
        for c in fetch(1):
            c.start()
    h_ref[...] = (_rms(x_ref[...]) * (1.0 + sc_ref[0]) + sh_ref[0]).astype(BF16)
    hidden_block(h_ref[...], wg0_ref, wu0_ref, wd0_ref, True)

    streamed = nj - 1
    tail = min(streamed, 2 + streamed % 2)
    pairs = (streamed - tail) // 2

    def pair(k, carry):
        streamed_block(1 + 2 * k, 0, True)
        streamed_block(2 + 2 * k, 1, True)
        return carry

    if pairs:
        lax.fori_loop(0, pairs, pair, 0)
    for j in range(nj - tail, nj):
        streamed_block(j, (j - 1) % 2, j + 1 < nj)

    y = x_ref[...] + (0.5 * gt_ref[0]) * o_ref[...]
    if final_norm:
        y = _rms(y) * gf_ref[...]
    o_ref[...] = y


def _ffn(x, sh, sc, gt, w_gu, w_down, g_final, *, layer, seq, tm, tf, final_norm):
    t, d = x.shape
    f = w_down.shape[1]
    nj = f // tf
    tpb = seq // tm
    vec = pl.BlockSpec((1, 1, d), lambda i: (i // tpb, 0, 0))
    resident = pl.Buffered(1)
    return pl.pallas_call(
        functools.partial(_ffn_kernel, layer=layer, final_norm=final_norm),
        out_shape=jax.ShapeDtypeStruct((t, d), F32),
        grid=(t // tm,),
        in_specs=[
            pl.BlockSpec((tm, d), lambda i: (i, 0)),
            vec, vec, vec,
            pl.BlockSpec((None, d, tf), lambda i: (layer, 0, 0), pipeline_mode=resident),
            pl.BlockSpec((None, d, tf), lambda i: (layer, 0, nj), pipeline_mode=resident),
            pl.BlockSpec((None, tf, d), lambda i: (layer, 0, 0), pipeline_mode=resident),
            pl.BlockSpec(memory_space=pl.ANY),
            pl.BlockSpec(memory_space=pl.ANY),
            pl.BlockSpec((1, d), lambda i: (0, 0)),
        ],
        out_specs=pl.BlockSpec((tm, d), lambda i: (i, 0)),
        scratch_shapes=[
            pltpu.VMEM((2, d, tf), BF16),
            pltpu.VMEM((2, d, tf), BF16),
            pltpu.VMEM((2, tf, d), BF16),
            pltpu.SemaphoreType.DMA((3, 2)),
            pltpu.VMEM((tm, d), BF16),
        ],
        compiler_params=_compiler_params(("parallel",)),
        name="ffn",
    )(x, sh, sc, gt, w_gu, w_gu, w_down, w_gu, w_down, g_final)


def _fprep_kernel(cc_ref, sc_ref, w_ref, ab_ref, *, scale):
    w = w_ref[0]
    hc = w.shape[0]
    a = jnp.dot(cc_ref[...], w, preferred_element_type=F32,
                precision=lax.Precision.HIGHEST)
    b = jnp.dot(sc_ref[...], w, preferred_element_type=F32,
                precision=lax.Precision.HIGHEST)
    ab_ref[0, :, :hc] = (a * scale).astype(BF16)
    ab_ref[0, :, hc:] = (b * scale).astype(BF16)


def _fprep(w_fourier, seq):
    n, hc, _ = w_fourier.shape
    ang = _dft_angles(hc, jnp.arange(hc), jnp.arange(hc))
    scale = 1.0 / math.sqrt(seq * hc)
    sq = pl.BlockSpec((hc, hc), lambda g: (0, 0))
    return pl.pallas_call(
        functools.partial(_fprep_kernel, scale=scale),
        out_shape=jax.ShapeDtypeStruct((n, hc, 2 * hc), BF16),
        grid=(n,),
        in_specs=[sq, sq, pl.BlockSpec((1, hc, hc), lambda g: (g, 0, 0))],
        out_specs=pl.BlockSpec((1, hc, 2 * hc), lambda g: (g, 0, 0)),
        compiler_params=_compiler_params(("parallel",)),
        name="fprep",
    )(jnp.cos(ang), jnp.sin(ang), w_fourier)


def _dft_angles(n, rows, cols):
    idx = (rows[:, None] * cols[None, :]) % n
    return idx.astype(F32) * (2.0 * math.pi / n)


def _mixin_kernel(x_ref, sh_ref, sc_ref, w_ref, ab_ref, cos_ref, sin_ref,
                  pab_ref, q_ref, k_ref, v_ref, *, q_scale):
    hc = HEAD_DIM
    dm = w_ref.shape[1] // 4
    groups = heads = dm // hc
    h = (_rms(x_ref[...]) * (1.0 + sc_ref[0]) + sh_ref[0]).astype(BF16)

    def project(n):
        return jnp.dot(h, w_ref[:, n * dm:(n + 1) * dm], preferred_element_type=F32)

    def rope_to(dst_ref, p, scale):
        cos = cos_ref[...] * scale
        sin = sin_ref[...] * scale
        for hh in range(heads):
            t = p[:, hh * hc:(hh + 1) * hc]
            dst_ref[0, hh] = t * cos + pltpu.roll(t, hc // 2, 1) * sin

    u = project(0).astype(BF16)
    pq = project(1)
    for g in range(groups):
        r = jnp.dot(u[:, g * hc:(g + 1) * hc], ab_ref[g], preferred_element_type=F32)
        pab_ref[:, g * hc:(g + 1) * hc] = r[:, :hc].astype(BF16)
        pab_ref[:, (groups + g) * hc:(groups + g + 1) * hc] = r[:, hc:].astype(BF16)
    rope_to(q_ref, pq, q_scale)
    rope_to(k_ref, project(2), 1.0)
    pv = project(3)
    for hh in range(heads):
        v_ref[0, hh] = pv[:, hh * hc:(hh + 1) * hc]


def _mixin(x, sh, sc, w_in, ab, cos, sin, *, layer, batch, seq, tm):
    t, d = x.shape
    dm = w_in.shape[2] // 4
    heads = dm // HEAD_DIM
    tpb = seq // tm
    vec = pl.BlockSpec((1, 1, d), lambda i: (i // tpb, 0, 0))
    tab = pl.BlockSpec((tm, HEAD_DIM), lambda i: (i % tpb, 0))
    hm = pl.BlockSpec((1, heads, tm, HEAD_DIM), lambda i: (i // tpb, 0, i % tpb, 0))
    hm_shape = jax.ShapeDtypeStruct((batch, heads, seq, HEAD_DIM), F32)
    resident = pl.Buffered(1)
    return pl.pallas_call(
        functools.partial(_mixin_kernel, q_scale=HEAD_DIM ** -0.5 * math.log2(math.e)),
        out_shape=(jax.ShapeDtypeStruct((t, 2 * dm), BF16),
                   hm_shape, hm_shape, hm_shape),
        grid=(t // tm,),
        in_specs=[
            pl.BlockSpec((tm, d), lambda i: (i, 0)),
            vec, vec,
            pl.BlockSpec((None, d, 4 * dm), lambda i: (layer, 0, 0), pipeline_mode=resident),
            pl.BlockSpec((None,) + ab.shape[1:], lambda i: (layer, 0, 0, 0),
                         pipeline_mode=resident),
            tab, tab,
        ],
        out_specs=(pl.BlockSpec((tm, 2 * dm), lambda i: (i, 0)), hm, hm, hm),
        compiler_params=_compiler_params(("parallel",)),
        name="mix_in",
    )(x, sh, sc, w_in, ab, cos, sin)


def _rope_tables(seq):
    half = HEAD_DIM // 2
    inv_freq = ROPE_THETA ** (-jnp.arange(half, dtype=F32) / half)
    ang = jnp.arange(seq, dtype=F32)[:, None] * inv_freq[None, :]
    cos, sin = jnp.cos(ang), jnp.sin(ang)
    return (jnp.concatenate([cos, cos], axis=-1),
            jnp.concatenate([-sin, sin], axis=-1))


DFT_RADIX = 4


def _dft_fold_kernel(z0_ref, z1_ref, z2_ref, z3_ref, w_ref):
    dm = w_ref.shape[3] // 2
    a = [z[0, 0, :, :dm].astype(F32) for z in (z0_ref, z1_ref, z2_ref, z3_ref)]
    p = [z[0, 0, :, dm:].astype(F32) for z in (z0_ref, z1_ref, z2_ref, z3_ref)]
    t0r, t0i = a[0] + a[2], -(p[0] + p[2])
    t1r, t1i = a[0] - a[2], p[2] - p[0]
    t2r, t2i = a[1] + a[3], -(p[1] + p[3])
    t3r, t3i = a[1] - a[3], p[3] - p[1]
    parts = ((t0r + t2r, t0i + t2i),
             (t1r + t3i, t1i - t3r),
             (t0r - t2r, t0i - t2i),
             (t1r - t3i, t1i + t3r))
    for r, (re, im) in enumerate(parts):
        w_ref[0, r, :, :dm] = re.astype(BF16)
        w_ref[0, r, :, dm:] = im.astype(BF16)


def _dft_fold(pab, *, batch, seq, tr):
    width = pab.shape[-1]
    nq = seq // DFT_RADIX
    z = pab.reshape(batch, DFT_RADIX, nq, width)
    quarter = [pl.BlockSpec((1, 1, tr, width), lambda b, i, q=q: (b, q, i, 0))
               for q in range(DFT_RADIX)]
    return pl.pallas_call(
        _dft_fold_kernel,
        out_shape=jax.ShapeDtypeStruct((batch, DFT_RADIX, nq, width), BF16),
        grid=(batch, nq // tr),
        in_specs=quarter,
        out_specs=pl.BlockSpec((1, DFT_RADIX, tr, width), lambda b, i: (b, 0, i, 0)),
        compiler_params=_compiler_params(("parallel", "parallel")),
        name="dft_fold",
    )(z, z, z, z)


def _dft_kernel(ca_ref, sa_ref, cb_ref, sb_ref, wre_ref, wim_ref, o_ref,
                cmat, smat, stage):
    r = pl.program_id(3)
    first = jnp.logical_and(pl.program_id(1) == 0, pl.program_id(2) == 0)

    @pl.when(first)
    def _():
        cb = cb_ref[0]
        sb = sb_ref[0]
        for a in range(ca_ref.shape[2]):
            ca = ca_ref[0, :, a:a + 1]
            sa = sa_ref[0, :, a:a + 1]
            cols = slice(a * LANES, (a + 1) * LANES)
            cmat[r, :, cols] = (ca * cb - sa * sb).astype(BF16)
            smat[r, :, cols] = (sa * cb + ca * sb).astype(BF16)

    y = (jnp.dot(cmat[r], wre_ref[0, 0], preferred_element_type=F32)
         + jnp.dot(smat[r], wim_ref[0, 0], preferred_element_type=F32))
    for cc in range(stage.shape[0]):
        stage[cc, pl.ds(r, y.shape[0], stride=DFT_RADIX), :] = (
            y[:, cc * LANES:(cc + 1) * LANES])

    @pl.when(r == DFT_RADIX - 1)
    def _():
        for cc in range(stage.shape[0]):
            o_ref[0, :, cc * LANES:(cc + 1) * LANES] = stage[cc]


def _dft(w, tables, *, batch, seq, tm, tn):
    dm = w.shape[-1] // 2
    nq = seq // DFT_RADIX
    ncb = dm // tn
    ca, sa, cb, sb = tables
    coarse = pl.BlockSpec((1, tm, ca.shape[2]), lambda i, b, c, r: (r, i, 0))
    fine = pl.BlockSpec((1, tm, LANES), lambda i, b, c, r: (r, i, 0))
    twiddle = pltpu.VMEM((DFT_RADIX, tm, nq), BF16)
    return pl.pallas_call(
        _dft_kernel,
        out_shape=jax.ShapeDtypeStruct((batch, seq, dm), F32),
        grid=(nq // tm, batch, ncb, DFT_RADIX),
        in_specs=[
            coarse, coarse, fine, fine,
            pl.BlockSpec((1, 1, nq, tn), lambda i, b, c, r: (b, r, 0, c)),
            pl.BlockSpec((1, 1, nq, tn), lambda i, b, c, r: (b, r, 0, c + ncb)),
        ],
        out_specs=pl.BlockSpec((1, DFT_RADIX * tm, tn), lambda i, b, c, r: (b, i, c)),
        scratch_shapes=[twiddle, twiddle,
                        pltpu.VMEM((tn // LANES, DFT_RADIX * tm, LANES), F32)],
        compiler_params=_compiler_params(
            ("parallel", "arbitrary", "arbitrary", "arbitrary")),
        name="dft",
    )(ca, sa, cb, sb, w, w)


def _dft_tables(seq):
    nq = seq // DFT_RADIX
    k = (DFT_RADIX * jnp.arange(nq)[None, :] + jnp.arange(DFT_RADIX)[:, None]).reshape(-1)
    coarse = _dft_angles(seq, k, LANES * jnp.arange(nq // LANES))
    fine = _dft_angles(seq, k, jnp.arange(LANES))
    return tuple(f(t).reshape(DFT_RADIX, nq, -1)
                 for t in (coarse, fine) for f in (jnp.cos, jnp.sin))


def _attn_kernel(q_ref, k_ref, v_ref, bias_ref, o_ref,
                 qp, kp, vp, s_buf, p_buf, pv, mb, lb):
    seq, hd = q_ref.shape[2], q_ref.shape[3]
    bq, kw = ATTN_BQ, ATTN_KW
    nblk = seq // bq

    assert kw == 2 * hd and s_buf.shape == (2, seq, hd)
    for src_ref, dst in ((q_ref, qp), (k_ref, kp), (v_ref, vp)):
        def load_prev(start, size, stride, src_ref=src_ref):
            return src_ref[0, 0, pl.ds(start, size, stride=stride), :]
        d_prev = 1
        for bi, d in enumerate(DILATIONS):
            ratio, sub, sub_prev = d // d_prev, seq // d, seq // d_prev
            piece = min(sub, ATTN_COPY_ROWS)
            for jp in range(d_prev):
                for r in range(ratio):
                    for c in range(sub // piece):
                        row0 = (jp + d_prev * r) * sub + c * piece
                        rows = slice(row0, row0 + piece)
                        x = load_prev(jp * sub_prev + r + c * piece * ratio, piece, ratio)
                        dst[bi, rows, :] = x.astype(BF16)
                        if 0 < bi < len(DILATIONS) - 1:
                            s_buf[bi % 2, rows, :] = x
            if d == 1:
                continue

            def load_prev(start, size, stride, half=bi % 2):
                return s_buf[half, pl.ds(start, size, stride=stride), :]
            d_prev = d

    ones = jnp.ones((kw, hd), BF16)

    for bi, d in enumerate(DILATIONS):
        sub = seq // d
        bps = sub // bq

        def place(n, d=d, sub=sub, bps=bps):
            j = n // bps
            t0 = (n % bps) * bq
            ks = jnp.clip(t0 - REACH, 0, sub - kw)
            qrows = pl.ds(pl.multiple_of(j * sub + t0, bq), bq)
            krows = pl.ds(pl.multiple_of(j * sub + ks, REACH), kw)
            if d > 1:
                orows = pl.ds(j + d * t0, bq, stride=d)
            else:
                orows = pl.ds(pl.multiple_of(t0, bq), bq)
            return qrows, krows, orows, (t0 - ks) // REACH

        def scores(n, carry, place=place, bi=bi):
            qrows, krows, _, edge = place(n)
            s = lax.dot_general(qp[bi, qrows, :], kp[bi, krows, :],
                                (((1,), (1,)), ((), ())), preferred_element_type=F32)
            s = s + bias_ref[edge]
            rows = pl.ds(pl.multiple_of(n * bq, bq), bq)
            s_buf[0, rows, :] = s[:, :hd]
            s_buf[1, rows, :] = s[:, hd:]
            return carry

        def softmax(n, carry, place=place, bi=bi):
            _, _, orows, _ = place(n)
            rows = pl.ds(pl.multiple_of(n * bq, bq), bq)
            s = jnp.concatenate([s_buf[0, rows, :], s_buf[1, rows, :]], axis=1)
            m = jnp.max(s, axis=-1, keepdims=True)
            p_buf[n] = jnp.exp2(s - m).astype(BF16)
            mb[bi, orows, :] = jnp.broadcast_to(m, (bq, hd))
            return carry

        def values(n, carry, place=place, bi=bi):
            _, krows, orows, _ = place(n)
            v1 = jnp.concatenate([vp[bi, krows, :], ones], axis=1)
            r = jnp.dot(p_buf[n], v1, preferred_element_type=F32)
            pv[bi, orows, :] = r[:, :hd]
            lb[bi, orows, :] = r[:, hd:]
            return carry

        lax.fori_loop(0, nblk, scores, 0, unroll=ATTN_UNROLL)
        lax.fori_loop(0, nblk, softmax, 0, unroll=ATTN_UNROLL)
        lax.fori_loop(0, nblk, values, 0, unroll=ATTN_UNROLL)

    def merge(c, carry):
        rows = pl.ds(pl.multiple_of(c * bq, bq), bq)
        branches = range(len(DILATIONS))
        ms = [mb[p, rows, :] for p in branches]
        top = functools.reduce(jnp.maximum, ms)
        es = [jnp.exp2(m - top) for m in ms]
        num = sum(es[p] * pv[p, rows, :] for p in branches)
        den = sum(es[p] * lb[p, rows, :] for p in branches)
        o_ref[0, rows, :] = num / den
        return carry

    lax.fori_loop(0, nblk, merge, 0, unroll=2)


def _attn_bias():
    r = jnp.arange(ATTN_BQ)[:, None]
    c = jnp.arange(ATTN_KW)[None, :]
    return jnp.stack([
        jnp.where(jnp.abs(edge * REACH + r - c) <= REACH, 0.0, MASK_VALUE).astype(F32)
        for edge in range(3)])


def _attn(q, k, v):
    batch, heads, seq, hd = q.shape
    nb = len(DILATIONS)
    nblk = seq // ATTN_BQ
    bias = _attn_bias()
    blk = pl.BlockSpec((1, 1, seq, hd), lambda b, h: (b, h, 0, 0))
    stat = pltpu.VMEM((nb, seq, hd), F32)
    gathered = pltpu.VMEM((nb, seq, hd), BF16)
    return pl.pallas_call(
        _attn_kernel,
        out_shape=jax.ShapeDtypeStruct((batch, seq, heads * hd), F32),
        grid=(batch, heads),
        in_specs=[blk, blk, blk, pl.BlockSpec(bias.shape, lambda b, h: (0, 0, 0))],
        out_specs=pl.BlockSpec((1, seq, hd), lambda b, h: (b, 0, h)),
        scratch_shapes=[
            gathered, gathered, gathered,
            pltpu.VMEM((ATTN_KW // hd, seq, hd), F32),
            pltpu.VMEM((nblk, ATTN_BQ, ATTN_KW), BF16),
            stat, stat, stat,
        ],
        compiler_params=_compiler_params(("parallel", "parallel"), ATTN_VMEM_LIMIT_BYTES),
        name="attn",
    )(q, k, v, bias)


def _mixout_kernel(x_ref, yf_ref, ya_ref, gf_ref, ga_ref, gt_ref, w_ref, o_ref):
    df = yf_ref.shape[1]
    nf = (_rms(yf_ref[...]) * gf_ref[...]).astype(BF16)
    na = (_rms(ya_ref[...]) * ga_ref[...]).astype(BF16)
    out = (jnp.dot(nf, w_ref[:df, :], preferred_element_type=F32)
           + jnp.dot(na, w_ref[df:, :], preferred_element_type=F32))
    o_ref[...] = x_ref[...] + gt_ref[0] * out


def _mixout(x, yf, ya, gf, ga, gt, w_out, *, layer, seq, tm):
    t, d = x.shape
    df, da = yf.shape[1], ya.shape[1]
    tpb = seq // tm
    return pl.pallas_call(
        _mixout_kernel,
        out_shape=jax.ShapeDtypeStruct((t, d), F32),
        grid=(t // tm,),
        in_specs=[
            pl.BlockSpec((tm, d), lambda i: (i, 0)),
            pl.BlockSpec((tm, df), lambda i: (i, 0)),
            pl.BlockSpec((tm, da), lambda i: (i, 0)),
            pl.BlockSpec((1, df), lambda i: (0, 0)),
            pl.BlockSpec((1, da), lambda i: (0, 0)),
            pl.BlockSpec((1, 1, d), lambda i: (i // tpb, 0, 0)),
            pl.BlockSpec((None, df + da, d), lambda i: (layer, 0, 0)),
        ],
        out_specs=pl.BlockSpec((tm, d), lambda i: (i, 0)),
        compiler_params=_compiler_params(("parallel",)),
        name="mix_out",
    )(x, yf, ya, gf, ga, gt, w_out)


class _Tiles(NamedTuple):
    ffn_rows: int
    ffn_cols: int
    mix_rows: int
    fold_rows: int
    dft_rows: int
    dft_cols: int
    ada_cols: int


def _tiles(seq, d, d_ff):
    ffn_cols = 512 if d_ff % 512 == 0 else LANES
    return _Tiles(ffn_rows=min(512, seq), ffn_cols=ffn_cols, mix_rows=min(512, seq),
                  fold_rows=min(256, seq // DFT_RADIX),
                  dft_rows=min(512, seq // DFT_RADIX), dft_cols=min(1024, d // 2), ada_cols=min(1024, d))


def kernel(x, c, w_ada, b_ada, w_ffn1_gu, w_ffn1_down, w_mix_in, w_fourier,
           g_fourier_out, g_attn_out, w_mix_out, w_ffn2_gu, w_ffn2_down, g_final):
    batch, seq, d = x.shape
    depth = w_ada.shape[0]
    d_ff = w_ffn1_down.shape[1]
    t = batch * seq
    tl = _tiles(seq, d, d_ff)

    rows = 8 * pl.cdiv(batch, 8)
    c_pad = jnp.zeros((rows, d), F32).at[:batch].set(c)
    mod = _ada(c_pad, w_ada, b_ada, tn=tl.ada_cols)
    mod = mod[:, :batch].reshape(depth, batch, N_MOD, 1, d)

    groups = w_fourier.shape[1]
    ab = _fprep(w_fourier.reshape(depth * groups, HEAD_DIM, HEAD_DIM), seq)
    ab = ab.reshape(depth, groups, HEAD_DIM, 2 * HEAD_DIM)
    rope_cos, rope_sin = _rope_tables(seq)
    dft_tables = _dft_tables(seq)
    g_fin = g_final.reshape(1, d)

    wgu1, wd1, wgu2, wd2, w_in, w_out = (
        w.astype(BF16) for w in (w_ffn1_gu, w_ffn1_down, w_ffn2_gu, w_ffn2_down,
                                 w_mix_in, w_mix_out))

    xt = x.reshape(t, d)
    for l in range(depth):
        sh1, sc1, g1, sh2, sc2, g2, sh3, sc3, g3 = (mod[l, :, i] for i in range(N_MOD))
        xt = _ffn(xt, sh1, sc1, g1, wgu1, wd1, g_fin, layer=l,
                  seq=seq, tm=tl.ffn_rows, tf=tl.ffn_cols, final_norm=False)
        pab, q, k, v = _mixin(xt, sh2, sc2, w_in, ab, rope_cos, rope_sin, layer=l,
                              batch=batch, seq=seq, tm=tl.mix_rows)
        w = _dft_fold(pab, batch=batch, seq=seq, tr=tl.fold_rows)
        yf = _dft(w, dft_tables, batch=batch, seq=seq, tm=tl.dft_rows, tn=tl.dft_cols)
        ya = _attn(q, k, v)
        xt = _mixout(xt, yf.reshape(t, -1), ya.reshape(t, -1),
                     g_fourier_out[l].reshape(1, -1), g_attn_out[l].reshape(1, -1),
                     g2, w_out, layer=l, seq=seq, tm=tl.mix_rows)
        xt = _ffn(xt, sh3, sc3, g3, wgu2, wd2, g_fin, layer=l,
                  seq=seq, tm=tl.ffn_rows, tf=tl.ffn_cols, final_norm=(l == depth - 1))
    return xt.reshape(batch, seq, d)
```

```python
import functools
import math
from typing import NamedTuple

import jax
import jax.numpy as jnp
from jax import lax
from jax.experimental import pallas as pl
from jax.experimental.pallas import tpu as pltpu

F32 = jnp.float32
BF16 = jnp.bfloat16

EPS = 1e-6
HEAD_DIM = 128
ROPE_THETA = 10000.0
DILATED_PATTERNS = ((128, 1), (512, 4), (2048, 16))
DILATIONS = tuple(d for _, d in DILATED_PATTERNS)
REACH = (DILATED_PATTERNS[0][0] // 2) // DILATED_PATTERNS[0][1]
assert all((w // 2) // d == REACH for w, d in DILATED_PATTERNS)
MASK_VALUE = -1e30
N_MOD = 9

LANES = 128
VMEM_LIMIT_BYTES = 56 * 1024 * 1024

FFN_COL_SPLIT = 2
FFN_RESIDENT_BLOCKS = 3

ATTN_BQ = 128
ATTN_KW = ATTN_BQ + 2 * REACH
ATTN_COPY_ROWS = 128
ATTN_UNROLL = True
ATTN_VMEM_LIMIT_BYTES = 60 * 1024 * 1024


def _compiler_params(semantics, vmem_limit_bytes=VMEM_LIMIT_BYTES):
    return pltpu.CompilerParams(dimension_semantics=semantics,
                                vmem_limit_bytes=vmem_limit_bytes)


def _rms(x):
    return x * lax.rsqrt(jnp.mean(x * x, axis=-1, keepdims=True) + EPS)


def _silu(x):
    return x * jax.nn.sigmoid(x)


def _ada_kernel(c_ref, w_ref, b_ref, o_ref):
    ca = _silu(c_ref[...]).astype(BF16)
    o_ref[0] = jnp.dot(ca, w_ref[0].astype(BF16),
                       preferred_element_type=F32) + b_ref[0]


def _ada(c_pad, w_ada, b_ada, *, tn):
    depth, d, n = w_ada.shape
    rows = c_pad.shape[0]
    return pl.pallas_call(
        _ada_kernel,
        out_shape=jax.ShapeDtypeStruct((depth, rows, n), F32),
        grid=(depth, n // tn),
        in_specs=[
            pl.BlockSpec((rows, d), lambda l, j: (0, 0)),
            pl.BlockSpec((1, d, tn), lambda l, j: (l, 0, j)),
            pl.BlockSpec((1, 1, tn), lambda l, j: (l, 0, j)),
        ],
        out_specs=pl.BlockSpec((1, rows, tn), lambda l, j: (l, 0, j)),
        compiler_params=_compiler_params(("parallel", "parallel")),
        name="ada",
    )(c_pad, w_ada, b_ada.reshape(depth, 1, n))


def _ffn_kernel(x_ref, sh_ref, sc_ref, gt_ref, *refs, layer, final_norm, resident):
    wg_res, wu_res, wd_res = (refs[k * resident:(k + 1) * resident] for k in range(3))
    (wgu_hbm, wd_hbm, gf_ref, o_ref,
     wg_buf, wu_buf, wd_buf, sems, h_ref) = refs[3 * resident:]
    d, tf = wg_res[0].shape
    f = wd_hbm.shape[1]
    nj = f // tf
    part = tf // FFN_COL_SPLIT

    def fetch(j):
        slot = (j - resident) % 2
        return (
            pltpu.make_async_copy(wgu_hbm.at[layer, :, pl.ds(j * tf, tf)],
                                  wg_buf.at[slot], sems.at[0, slot]),
            pltpu.make_async_copy(wgu_hbm.at[layer, :, pl.ds(f + j * tf, tf)],
                                  wu_buf.at[slot], sems.at[1, slot]),
            pltpu.make_async_copy(wd_hbm.at[layer, pl.ds(j * tf, tf), :],
                                  wd_buf.at[slot], sems.at[2, slot]),
        )

    def hidden_block(h, wg, wu, wd, first):
        gu = []
        for s in range(FFN_COL_SPLIT):
            cols = slice(s * part, (s + 1) * part)
            gu.append((jnp.dot(h, wg[:, cols], preferred_element_type=F32),
                       jnp.dot(h, wu[:, cols], preferred_element_type=F32)))
        for s, (g, u) in enumerate(gu):
            a = (_silu(g) * u).astype(BF16)
            r = jnp.dot(a, wd[s * part:(s + 1) * part, :], preferred_element_type=F32)
            if first and s == 0:
                o_ref[...] = r
            else:
                o_ref[...] += r

    def streamed_block(j, slot, prefetch):
        for c in fetch(j):
            c.wait()
        if prefetch:
            for c in fetch(j + 1):
                c.start()
        hidden_block(h_ref[...], wg_buf.at[slot], wu_buf.at[slot], wd_buf.at[slot], False)

    streamed = nj - resident
    if streamed:
        for c in fetch(resident):
            c.start()
    h_ref[...] = (_rms(x_ref[...]) * (1.0 + sc_ref[0]) + sh_ref[0]).astype(BF16)
    for j in range(resident):
        hidden_block(h_ref[...], wg_res[j], wu_res[j], wd_res[j], j == 0)

    tail = min(streamed, 2 + streamed % 2)
    pairs = (streamed - tail) // 2

    def pair(k, carry):
        streamed_block(resident + 2 * k, 0, True)
        streamed_block(resident + 2 * k + 1, 1, True)
        return carry

    if pairs:
        lax.fori_loop(0, pairs, pair, 0)
    for j in range(nj - tail, nj):
        streamed_block(j, (j - resident) % 2, j + 1 < nj)

    y = x_ref[...] + (0.5 * gt_ref[0]) * o_ref[...]
    if final_norm:
        y = _rms(y) * gf_ref[...]
    o_ref[...] = y


def _ffn(x, sh, sc, gt, w_gu, w_down, g_final, *, layer, seq, tm, tf, final_norm):
    t, d = x.shape
    f = w_down.shape[1]
    nj = f // tf
    tpb = seq // tm
    resident = min(FFN_RESIDENT_BLOCKS, nj)
    vec = pl.BlockSpec((1, 1, d), lambda i: (i // tpb, 0, 0))
    once = pl.Buffered(1)
    wg_res = [pl.BlockSpec((None, d, tf), lambda i, j=j: (layer, 0, j), pipeline_mode=once)
              for j in range(resident)]
    wu_res = [pl.BlockSpec((None, d, tf), lambda i, j=j: (layer, 0, nj + j), pipeline_mode=once)
              for j in range(resident)]
    wd_res = [pl.BlockSpec((None, tf, d), lambda i, j=j: (layer, j, 0), pipeline_mode=once)
              for j in range(resident)]
    return pl.pallas_call(
        functools.partial(_ffn_kernel, layer=layer, final_norm=final_norm, resident=resident),
        out_shape=jax.ShapeDtypeStruct((t, d), F32),
        grid=(t // tm,),
        in_specs=[
            pl.BlockSpec((tm, d), lambda i: (i, 0)),
            vec, vec, vec,
            *wg_res, *wu_res, *wd_res,
            pl.BlockSpec(memory_space=pl.ANY),
            pl.BlockSpec(memory_space=pl.ANY),
            pl.BlockSpec((1, d), lambda i: (0, 0)),
        ],
        out_specs=pl.BlockSpec((tm, d), lambda i: (i, 0)),
        scratch_shapes=[
            pltpu.VMEM((2, d, tf), BF16),
            pltpu.VMEM((2, d, tf), BF16),
            pltpu.VMEM((2, tf, d), BF16),
            pltpu.SemaphoreType.DMA((3, 2)),
            pltpu.VMEM((tm, d), BF16),
        ],
        compiler_params=_compiler_params(("parallel",)),
        name="ffn",
    )(x, sh, sc, gt, *([w_gu] * (2 * resident)), *([w_down] * resident), w_gu, w_down, g_final)


def _fprep_kernel(cc_ref, sc_ref, w_ref, ab_ref, *, scale):
    w = w_ref[0]
    hc = w.shape[0]
    a = jnp.dot(cc_ref[...], w, preferred_element_type=F32,
                precision=lax.Precision.HIGHEST)
    b = jnp.dot(sc_ref[...], w, preferred_element_type=F32,
                precision=lax.Precision.HIGHEST)
    ab_ref[0, :, :hc] = (a * scale).astype(BF16)
    ab_ref[0, :, hc:] = (b * scale).astype(BF16)


def _fprep(w_fourier, seq):
    n, hc, _ = w_fourier.shape
    ang = _dft_angles(hc, jnp.arange(hc), jnp.arange(hc))
    scale = 1.0 / math.sqrt(seq * hc)
    sq = pl.BlockSpec((hc, hc), lambda g: (0, 0))
    return pl.pallas_call(
        functools.partial(_fprep_kernel, scale=scale),
        out_shape=jax.ShapeDtypeStruct((n, hc, 2 * hc), BF16),
        grid=(n,),
        in_specs=[sq, sq, pl.BlockSpec((1, hc, hc), lambda g: (g, 0, 0))],
        out_specs=pl.BlockSpec((1, hc, 2 * hc), lambda g: (g, 0, 0)),
        compiler_params=_compiler_params(("parallel",)),
        name="fprep",
    )(jnp.cos(ang), jnp.sin(ang), w_fourier)


def _dft_angles(n, rows, cols):
    idx = (rows[:, None] * cols[None, :]) % n
    return idx.astype(F32) * (2.0 * math.pi / n)


def _mixin_kernel(x_ref, sh_ref, sc_ref, w_ref, ab_ref, cos_ref, sin_ref,
                  pab_ref, q_ref, k_ref, v_ref, *, q_scale):
    hc = HEAD_DIM
    dm = w_ref.shape[1] // 4
    groups = heads = dm // hc
    h = (_rms(x_ref[...]) * (1.0 + sc_ref[0]) + sh_ref[0]).astype(BF16)

    def project(n):
        return jnp.dot(h, w_ref[:, n * dm:(n + 1) * dm], preferred_element_type=F32)

    def rope_to(dst_ref, p, scale):
        cos = cos_ref[...] * scale
        sin = sin_ref[...] * scale
        for hh in range(heads):
            t = p[:, hh * hc:(hh + 1) * hc]
            dst_ref[0, hh] = t * cos + pltpu.roll(t, hc // 2, 1) * sin

    u = project(0).astype(BF16)
    pq = project(1)
    for g in range(groups):
        r = jnp.dot(u[:, g * hc:(g + 1) * hc], ab_ref[g], preferred_element_type=F32)
        pab_ref[:, g * hc:(g + 1) * hc] = r[:, :hc].astype(BF16)
        pab_ref[:, (groups + g) * hc:(groups + g + 1) * hc] = r[:, hc:].astype(BF16)
    rope_to(q_ref, pq, q_scale)
    rope_to(k_ref, project(2), 1.0)
    pv = project(3)
    for hh in range(heads):
        v_ref[0, hh] = pv[:, hh * hc:(hh + 1) * hc]


def _mixin(x, sh, sc, w_in, ab, cos, sin, *, layer, batch, seq, tm):
    t, d = x.shape
    dm = w_in.shape[2] // 4
    heads = dm // HEAD_DIM
    tpb = seq // tm
    vec = pl.BlockSpec((1, 1, d), lambda i: (i // tpb, 0, 0))
    tab = pl.BlockSpec((tm, HEAD_DIM), lambda i: (i % tpb, 0))
    hm = pl.BlockSpec((1, heads, tm, HEAD_DIM), lambda i: (i // tpb, 0, i % tpb, 0))
    hm_shape = jax.ShapeDtypeStruct((batch, heads, seq, HEAD_DIM), F32)
    resident = pl.Buffered(1)
    return pl.pallas_call(
        functools.partial(_mixin_kernel, q_scale=HEAD_DIM ** -0.5 * math.log2(math.e)),
        out_shape=(jax.ShapeDtypeStruct((t, 2 * dm), BF16),
                   hm_shape, hm_shape, hm_shape),
        grid=(t // tm,),
        in_specs=[
            pl.BlockSpec((tm, d), lambda i: (i, 0)),
            vec, vec,
            pl.BlockSpec((None, d, 4 * dm), lambda i: (layer, 0, 0), pipeline_mode=resident),
            pl.BlockSpec((None,) + ab.shape[1:], lambda i: (layer, 0, 0, 0),
                         pipeline_mode=resident),
            tab, tab,
        ],
        out_specs=(pl.BlockSpec((tm, 2 * dm), lambda i: (i, 0)), hm, hm, hm),
        compiler_params=_compiler_params(("parallel",)),
        name="mix_in",
    )(x, sh, sc, w_in, ab, cos, sin)


def _rope_tables(seq):
    half = HEAD_DIM // 2
    inv_freq = ROPE_THETA ** (-jnp.arange(half, dtype=F32) / half)
    ang = jnp.arange(seq, dtype=F32)[:, None] * inv_freq[None, :]
    cos, sin = jnp.cos(ang), jnp.sin(ang)
    return (jnp.concatenate([cos, cos], axis=-1),
            jnp.concatenate([-sin, sin], axis=-1))


DFT_RADIX = 4


def _dft_fold_kernel(z0_ref, z1_ref, z2_ref, z3_ref, w_ref):
    dm = w_ref.shape[3] // 2
    a = [z[0, 0, :, :dm].astype(F32) for z in (z0_ref, z1_ref, z2_ref, z3_ref)]
    p = [z[0, 0, :, dm:].astype(F32) for z in (z0_ref, z1_ref, z2_ref, z3_ref)]
    t0r, t0i = a[0] + a[2], -(p[0] + p[2])
    t1r, t1i = a[0] - a[2], p[2] - p[0]
    t2r, t2i = a[1] + a[3], -(p[1] + p[3])
    t3r, t3i = a[1] - a[3], p[3] - p[1]
    parts = ((t0r + t2r, t0i + t2i),
             (t1r + t3i, t1i - t3r),
             (t0r - t2r, t0i - t2i),
             (t1r - t3i, t1i + t3r))
    for r, (re, im) in enumerate(parts):
        w_ref[0, r, :, :dm] = re.astype(BF16)
        w_ref[0, r, :, dm:] = im.astype(BF16)


def _dft_fold(pab, *, batch, seq, tr):
    width = pab.shape[-1]
    nq = seq // DFT_RADIX
    z = pab.reshape(batch, DFT_RADIX, nq, width)
    quarter = [pl.BlockSpec((1, 1, tr, width), lambda b, i, q=q: (b, q, i, 0))
               for q in range(DFT_RADIX)]
    return pl.pallas_call(
        _dft_fold_kernel,
        out_shape=jax.ShapeDtypeStruct((batch, DFT_RADIX, nq, width), BF16),
        grid=(batch, nq // tr),
        in_specs=quarter,
        out_specs=pl.BlockSpec((1, DFT_RADIX, tr, width), lambda b, i: (b, 0, i, 0)),
        compiler_params=_compiler_params(("parallel", "parallel")),
        name="dft_fold",
    )(z, z, z, z)


def _dft_kernel(ca_ref, sa_ref, cb_ref, sb_ref, wre_ref, wim_ref, o_ref,
                cmat, smat, stage):
    r = pl.program_id(3)
    first = jnp.logical_and(pl.program_id(1) == 0, pl.program_id(2) == 0)

    @pl.when(first)
    def _():
        cb = cb_ref[0]
        sb = sb_ref[0]
        for a in range(ca_ref.shape[2]):
            ca = ca_ref[0, :, a:a + 1]
            sa = sa_ref[0, :, a:a + 1]
            cols = slice(a * LANES, (a + 1) * LANES)
            cmat[r, :, cols] = (ca * cb - sa * sb).astype(BF16)
            smat[r, :, cols] = (sa * cb + ca * sb).astype(BF16)

    y = (jnp.dot(cmat[r], wre_ref[0, 0], preferred_element_type=F32)
         + jnp.dot(smat[r], wim_ref[0, 0], preferred_element_type=F32))
    for cc in range(stage.shape[0]):
        stage[cc, pl.ds(r, y.shape[0], stride=DFT_RADIX), :] = (
            y[:, cc * LANES:(cc + 1) * LANES])

    @pl.when(r == DFT_RADIX - 1)
    def _():
        for cc in range(stage.shape[0]):
            o_ref[0, :, cc * LANES:(cc + 1) * LANES] = stage[cc]


def _dft(w, tables, *, batch, seq, tm, tn):
    dm = w.shape[-1] // 2
    nq = seq // DFT_RADIX
    ncb = dm // tn
    ca, sa, cb, sb = tables
    coarse = pl.BlockSpec((1, tm, ca.shape[2]), lambda i, b, c, r: (r, i, 0))
    fine = pl.BlockSpec((1, tm, LANES), lambda i, b, c, r: (r, i, 0))
    twiddle = pltpu.VMEM((DFT_RADIX, tm, nq), BF16)
    return pl.pallas_call(
        _dft_kernel,
        out_shape=jax.ShapeDtypeStruct((batch, seq, dm), F32),
        grid=(nq // tm, batch, ncb, DFT_RADIX),
        in_specs=[
            coarse, coarse, fine, fine,
            pl.BlockSpec((1, 1, nq, tn), lambda i, b, c, r: (b, r, 0, c)),
            pl.BlockSpec((1, 1, nq, tn), lambda i, b, c, r: (b, r, 0, c + ncb)),
        ],
        out_specs=pl.BlockSpec((1, DFT_RADIX * tm, tn), lambda i, b, c, r: (b, i, c)),
        scratch_shapes=[twiddle, twiddle,
                        pltpu.VMEM((tn // LANES, DFT_RADIX * tm, LANES), F32)],
        compiler_params=_compiler_params(
            ("parallel", "arbitrary", "arbitrary", "arbitrary")),
        name="dft",
    )(ca, sa, cb, sb, w, w)


def _dft_tables(seq):
    nq = seq // DFT_RADIX
    k = (DFT_RADIX * jnp.arange(nq)[None, :] + jnp.arange(DFT_RADIX)[:, None]).reshape(-1)
    coarse = _dft_angles(seq, k, LANES * jnp.arange(nq // LANES))
    fine = _dft_angles(seq, k, jnp.arange(LANES))
    return tuple(f(t).reshape(DFT_RADIX, nq, -1)
                 for t in (coarse, fine) for f in (jnp.cos, jnp.sin))


def _attn_kernel(q_ref, k_ref, v_ref, bias_ref, o_ref,
                 qp, kp, vp, s_buf, p_buf, pv, mb, lb):
    seq, hd = q_ref.shape[2], q_ref.shape[3]
    bq, kw = ATTN_BQ, ATTN_KW
    nblk = seq // bq

    assert kw == 2 * hd and s_buf.shape == (2, seq, hd)
    for src_ref, dst in ((q_ref, qp), (k_ref, kp), (v_ref, vp)):
        def load_prev(start, size, stride, src_ref=src_ref):
            return src_ref[0, 0, pl.ds(start, size, stride=stride), :]
        d_prev = 1
        for bi, d in enumerate(DILATIONS):
            ratio, sub, sub_prev = d // d_prev, seq // d, seq // d_prev
            piece = min(sub, ATTN_COPY_ROWS)
            for jp in range(d_prev):
                for r in range(ratio):
                    for c in range(sub // piece):
                        row0 = (jp + d_prev * r) * sub + c * piece
                        rows = slice(row0, row0 + piece)
                        x = load_prev(jp * sub_prev + r + c * piece * ratio, piece, ratio)
                        dst[bi, rows, :] = x.astype(BF16)
                        if 0 < bi < len(DILATIONS) - 1:
                            s_buf[bi % 2, rows, :] = x
            if d == 1:
                continue

            def load_prev(start, size, stride, half=bi % 2):
                return s_buf[half, pl.ds(start, size, stride=stride), :]
            d_prev = d

    ones = jnp.ones((kw, hd), BF16)

    for bi, d in enumerate(DILATIONS):
        sub = seq // d
        bps = sub // bq

        def place(n, d=d, sub=sub, bps=bps):
            j = n // bps
            t0 = (n % bps) * bq
            ks = jnp.clip(t0 - REACH, 0, sub - kw)
            qrows = pl.ds(pl.multiple_of(j * sub + t0, bq), bq)
            krows = pl.ds(pl.multiple_of(j * sub + ks, REACH), kw)
            if d > 1:
                orows = pl.ds(j + d * t0, bq, stride=d)
            else:
                orows = pl.ds(pl.multiple_of(t0, bq), bq)
            return qrows, krows, orows, (t0 - ks) // REACH

        def scores(n, carry, place=place, bi=bi):
            qrows, krows, _, edge = place(n)
            s = lax.dot_general(qp[bi, qrows, :], kp[bi, krows, :],
                                (((1,), (1,)), ((), ())), preferred_element_type=F32)
            s = s + bias_ref[edge]
            rows = pl.ds(pl.multiple_of(n * bq, bq), bq)
            s_buf[0, rows, :] = s[:, :hd]
            s_buf[1, rows, :] = s[:, hd:]
            return carry

        def softmax(n, carry, place=place, bi=bi):
            _, _, orows, _ = place(n)
            rows = pl.ds(pl.multiple_of(n * bq, bq), bq)
            s = jnp.concatenate([s_buf[0, rows, :], s_buf[1, rows, :]], axis=1)
            m = jnp.max(s, axis=-1, keepdims=True)
            p_buf[n] = jnp.exp2(s - m).astype(BF16)
            mb[bi, orows, :] = jnp.broadcast_to(m, (bq, hd))
            return carry

        def values(n, carry, place=place, bi=bi):
            _, krows, orows, _ = place(n)
            v1 = jnp.concatenate([vp[bi, krows, :], ones], axis=1)
            r = jnp.dot(p_buf[n], v1, preferred_element_type=F32)
            pv[bi, orows, :] = r[:, :hd]
            lb[bi, orows, :] = r[:, hd:]
            return carry

        lax.fori_loop(0, nblk, scores, 0, unroll=ATTN_UNROLL)
        lax.fori_loop(0, nblk, softmax, 0, unroll=ATTN_UNROLL)
        lax.fori_loop(0, nblk, values, 0, unroll=ATTN_UNROLL)

    def merge(c, carry):
        rows = pl.ds(pl.multiple_of(c * bq, bq), bq)
        branches = range(len(DILATIONS))
        ms = [mb[p, rows, :] for p in branches]
        top = functools.reduce(jnp.maximum, ms)
        es = [jnp.exp2(m - top) for m in ms]
        num = sum(es[p] * pv[p, rows, :] for p in branches)
        den = sum(es[p] * lb[p, rows, :] for p in branches)
        o_ref[0, rows, :] = num / den
        return carry

    lax.fori_loop(0, nblk, merge, 0, unroll=2)


def _attn_bias():
    r = jnp.arange(ATTN_BQ)[:, None]
    c = jnp.arange(ATTN_KW)[None, :]
    return jnp.stack([
        jnp.where(jnp.abs(edge * REACH + r - c) <= REACH, 0.0, MASK_VALUE).astype(F32)
        for edge in range(3)])


def _attn(q, k, v):
    batch, heads, seq, hd = q.shape
    nb = len(DILATIONS)
    nblk = seq // ATTN_BQ
    bias = _attn_bias()
    blk = pl.BlockSpec((1, 1, seq, hd), lambda b, h: (b, h, 0, 0))
    stat = pltpu.VMEM((nb, seq, hd), F32)
    gathered = pltpu.VMEM((nb, seq, hd), BF16)
    return pl.pallas_call(
        _attn_kernel,
        out_shape=jax.ShapeDtypeStruct((batch, seq, heads * hd), F32),
        grid=(batch, heads),
        in_specs=[blk, blk, blk, pl.BlockSpec(bias.shape, lambda b, h: (0, 0, 0))],
        out_specs=pl.BlockSpec((1, seq, hd), lambda b, h: (b, 0, h)),
        scratch_shapes=[
            gathered, gathered, gathered,
            pltpu.VMEM((ATTN_KW // hd, seq, hd), F32),
            pltpu.VMEM((nblk, ATTN_BQ, ATTN_KW), BF16),
            stat, stat, stat,
        ],
        compiler_params=_compiler_params(("parallel", "parallel"), ATTN_VMEM_LIMIT_BYTES),
        name="attn",
    )(q, k, v, bias)


def _mixout_kernel(x_ref, yf_ref, ya_ref, gf_ref, ga_ref, gt_ref, w_ref, o_ref):
    df = yf_ref.shape[1]
    nf = (_rms(yf_ref[...]) * gf_ref[...]).astype(BF16)
    na = (_rms(ya_ref[...]) * ga_ref[...]).astype(BF16)
    out = (jnp.dot(nf, w_ref[:df, :], preferred_element_type=F32)
           + jnp.dot(na, w_ref[df:, :], preferred_element_type=F32))
    o_ref[...] = x_ref[...] + gt_ref[0] * out


def _mixout(x, yf, ya, gf, ga, gt, w_out, *, layer, seq, tm):
    t, d = x.shape
    df, da = yf.shape[1], ya.shape[1]
    tpb = seq // tm
    return pl.pallas_call(
        _mixout_kernel,
        out_shape=jax.ShapeDtypeStruct((t, d), F32),
        grid=(t // tm,),
        in_specs=[
            pl.BlockSpec((tm, d), lambda i: (i, 0)),
            pl.BlockSpec((tm, df), lambda i: (i, 0)),
            pl.BlockSpec((tm, da), lambda i: (i, 0)),
            pl.BlockSpec((1, df), lambda i: (0, 0)),
            pl.BlockSpec((1, da), lambda i: (0, 0)),
            pl.BlockSpec((1, 1, d), lambda i: (i // tpb, 0, 0)),
            pl.BlockSpec((None, df + da, d), lambda i: (layer, 0, 0)),
        ],
        out_specs=pl.BlockSpec((tm, d), lambda i: (i, 0)),
        compiler_params=_compiler_params(("parallel",)),
        name="mix_out",
    )(x, yf, ya, gf, ga, gt, w_out)


class _Tiles(NamedTuple):
    ffn_rows: int
    ffn_cols: int
    mix_rows: int
    fold_rows: int
    dft_rows: int
    dft_cols: int
    ada_cols: int


def _tiles(seq, d, d_ff):
    ffn_cols = 512 if d_ff % 512 == 0 else LANES
    return _Tiles(ffn_rows=min(512, seq), ffn_cols=ffn_cols, mix_rows=min(512, seq),
                  fold_rows=min(256, seq // DFT_RADIX),
                  dft_rows=min(512, seq // DFT_RADIX), dft_cols=min(1024, d // 2), ada_cols=min(1024, d))


def kernel(x, c, w_ada, b_ada, w_ffn1_gu, w_ffn1_down, w_mix_in, w_fourier,
           g_fourier_out, g_attn_out, w_mix_out, w_ffn2_gu, w_ffn2_down, g_final):
    batch, seq, d = x.shape
    depth = w_ada.shape[0]
    d_ff = w_ffn1_down.shape[1]
    t = batch * seq
    tl = _tiles(seq, d, d_ff)

    rows = 8 * pl.cdiv(batch, 8)
    c_pad = jnp.zeros((rows, d), F32).at[:batch].set(c)
    mod = _ada(c_pad, w_ada, b_ada, tn=tl.ada_cols)
    mod = mod[:, :batch].reshape(depth, batch, N_MOD, 1, d)

    groups = w_fourier.shape[1]
    ab = _fprep(w_fourier.reshape(depth * groups, HEAD_DIM, HEAD_DIM), seq)
    ab = ab.reshape(depth, groups, HEAD_DIM, 2 * HEAD_DIM)
    rope_cos, rope_sin = _rope_tables(seq)
    dft_tables = _dft_tables(seq)
    g_fin = g_final.reshape(1, d)

    wgu1, wd1, wgu2, wd2, w_in, w_out = (
        w.astype(BF16) for w in (w_ffn1_gu, w_ffn1_down, w_ffn2_gu, w_ffn2_down,
                                 w_mix_in, w_mix_out))

    xt = x.reshape(t, d)
    for l in range(depth):
        sh1, sc1, g1, sh2, sc2, g2, sh3, sc3, g3 = (mod[l, :, i] for i in range(N_MOD))
        xt = _ffn(xt, sh1, sc1, g1, wgu1, wd1, g_fin, layer=l,
                  seq=seq, tm=tl.ffn_rows, tf=tl.ffn_cols, final_norm=False)
        pab, q, k, v = _mixin(xt, sh2, sc2, w_in, ab, rope_cos, rope_sin, layer=l,
                              batch=batch, seq=seq, tm=tl.mix_rows)
        w = _dft_fold(pab, batch=batch, seq=seq, tr=tl.fold_rows)
        yf = _dft(w, dft_tables, batch=batch, seq=seq, tm=tl.dft_rows, tn=tl.dft_cols)
        ya = _attn(q, k, v)
        xt = _mixout(xt, yf.reshape(t, -1), ya.reshape(t, -1),
                     g_fourier_out[l].reshape(1, -1), g_attn_out[l].reshape(1, -1),
                     g2, w_out, layer=l, seq=seq, tm=tl.mix_rows)
        xt = _ffn(xt, sh3, sc3, g3, wgu2, wd2, g_fin, layer=l,
                  seq=seq, tm=tl.ffn_rows, tf=tl.ffn_cols, final_norm=(l == depth - 1))
    return xt.reshape(batch, seq, d)
```

```python
import functools
import math
from typing import NamedTuple

import jax
import jax.numpy as jnp
from jax import lax
from jax.experimental import pallas as pl
from jax.experimental.pallas import tpu as pltpu

F32 = jnp.float32
BF16 = jnp.bfloat16

EPS = 1e-6
HEAD_DIM = 128
ROPE_THETA = 10000.0
DILATED_PATTERNS = ((128, 1), (512, 4), (2048, 16))
DILATIONS = tuple(d for _, d in DILATED_PATTERNS)
REACH = (DILATED_PATTERNS[0][0] // 2) // DILATED_PATTERNS[0][1]
assert all((w // 2) // d == REACH for w, d in DILATED_PATTERNS)
MASK_VALUE = -1e30
N_MOD = 9

LANES = 128
VMEM_LIMIT_BYTES = 56 * 1024 * 1024
BIG_VMEM_LIMIT_BYTES = 62 * 1024 * 1024

FFN_COL_SPLIT = 2
FFN_RESIDENT_BLOCKS = 4

ATTN_BQ = 128
ATTN_KW = ATTN_BQ + 2 * REACH
ATTN_COPY_ROWS = 128
ATTN_UNROLL = True


def _compiler_params(semantics, vmem_limit_bytes=VMEM_LIMIT_BYTES):
    return pltpu.CompilerParams(dimension_semantics=semantics,
                                vmem_limit_bytes=vmem_limit_bytes)


def _rms(x):
    return x * lax.rsqrt(jnp.mean(x * x, axis=-1, keepdims=True) + EPS)


def _silu(x):
    return x * jax.nn.sigmoid(x)


def _ada_kernel(c_ref, w_ref, b_ref, o_ref):
    ca = _silu(c_ref[...]).astype(BF16)
    o_ref[0] = jnp.dot(ca, w_ref[0].astype(BF16),
                       preferred_element_type=F32) + b_ref[0]


def _ada(c_pad, w_ada, b_ada, *, tn):
    depth, d, n = w_ada.shape
    rows = c_pad.shape[0]
    return pl.pallas_call(
        _ada_kernel,
        out_shape=jax.ShapeDtypeStruct((depth, rows, n), F32),
        grid=(depth, n // tn),
        in_specs=[
            pl.BlockSpec((rows, d), lambda l, j: (0, 0)),
            pl.BlockSpec((1, d, tn), lambda l, j: (l, 0, j)),
            pl.BlockSpec((1, 1, tn), lambda l, j: (l, 0, j)),
        ],
        out_specs=pl.BlockSpec((1, rows, tn), lambda l, j: (l, 0, j)),
        compiler_params=_compiler_params(("parallel", "parallel")),
        name="ada",
    )(c_pad, w_ada, b_ada.reshape(depth, 1, n))


def _ffn_kernel(x_ref, sh_ref, sc_ref, gt_ref, *refs, layer, final_norm, resident):
    wg_res, wu_res, wd_res = (refs[k * resident:(k + 1) * resident] for k in range(3))
    (wgu_hbm, wd_hbm, gf_ref, o_ref,
     wg_buf, wu_buf, wd_buf, sems, h_ref) = refs[3 * resident:]
    d, tf = wg_res[0].shape
    f = wd_hbm.shape[1]
    nj = f // tf
    part = tf // FFN_COL_SPLIT

    def fetch(j):
        slot = (j - resident) % 2
        return (
            pltpu.make_async_copy(wgu_hbm.at[layer, :, pl.ds(j * tf, tf)],
                                  wg_buf.at[slot], sems.at[0, slot]),
            pltpu.make_async_copy(wgu_hbm.at[layer, :, pl.ds(f + j * tf, tf)],
                                  wu_buf.at[slot], sems.at[1, slot]),
            pltpu.make_async_copy(wd_hbm.at[layer, pl.ds(j * tf, tf), :],
                                  wd_buf.at[slot], sems.at[2, slot]),
        )

    def hidden_block(h, wg, wu, wd, first):
        gu = []
        for s in range(FFN_COL_SPLIT):
            cols = slice(s * part, (s + 1) * part)
            gu.append((jnp.dot(h, wg[:, cols], preferred_element_type=F32),
                       jnp.dot(h, wu[:, cols], preferred_element_type=F32)))
        for s, (g, u) in enumerate(gu):
            a = (_silu(g) * u).astype(BF16)
            r = jnp.dot(a, wd[s * part:(s + 1) * part, :], preferred_element_type=F32)
            if first and s == 0:
                o_ref[...] = r
            else:
                o_ref[...] += r

    def streamed_block(j, slot, prefetch):
        for c in fetch(j):
            c.wait()
        if prefetch:
            for c in fetch(j + 1):
                c.start()
        hidden_block(h_ref[...], wg_buf.at[slot], wu_buf.at[slot], wd_buf.at[slot], False)

    streamed = nj - resident
    if streamed:
        for c in fetch(resident):
            c.start()
    h_ref[...] = (_rms(x_ref[...]) * (1.0 + sc_ref[0]) + sh_ref[0]).astype(BF16)
    for j in range(resident):
        hidden_block(h_ref[...], wg_res[j], wu_res[j], wd_res[j], j == 0)

    tail = min(streamed, 2 + streamed % 2)
    pairs = (streamed - tail) // 2

    def pair(k, carry):
        streamed_block(resident + 2 * k, 0, True)
        streamed_block(resident + 2 * k + 1, 1, True)
        return carry

    if pairs:
        lax.fori_loop(0, pairs, pair, 0)
    for j in range(nj - tail, nj):
        streamed_block(j, (j - resident) % 2, j + 1 < nj)

    y = x_ref[...] + (0.5 * gt_ref[0]) * o_ref[...]
    if final_norm:
        y = _rms(y) * gf_ref[...]
    o_ref[...] = y


def _ffn(x, sh, sc, gt, w_gu, w_down, g_final, *, layer, seq, tm, tf, final_norm):
    t, d = x.shape
    f = w_down.shape[1]
    nj = f // tf
    tpb = seq // tm
    resident = min(FFN_RESIDENT_BLOCKS, nj)
    vec = pl.BlockSpec((1, 1, d), lambda i: (i // tpb, 0, 0))
    once = pl.Buffered(1)
    wg_res = [pl.BlockSpec((None, d, tf), lambda i, j=j: (layer, 0, j), pipeline_mode=once)
              for j in range(resident)]
    wu_res = [pl.BlockSpec((None, d, tf), lambda i, j=j: (layer, 0, nj + j), pipeline_mode=once)
              for j in range(resident)]
    wd_res = [pl.BlockSpec((None, tf, d), lambda i, j=j: (layer, j, 0), pipeline_mode=once)
              for j in range(resident)]
    return pl.pallas_call(
        functools.partial(_ffn_kernel, layer=layer, final_norm=final_norm, resident=resident),
        out_shape=jax.ShapeDtypeStruct((t, d), F32),
        grid=(t // tm,),
        in_specs=[
            pl.BlockSpec((tm, d), lambda i: (i, 0)),
            vec, vec, vec,
            *wg_res, *wu_res, *wd_res,
            pl.BlockSpec(memory_space=pl.ANY),
            pl.BlockSpec(memory_space=pl.ANY),
            pl.BlockSpec((1, d), lambda i: (0, 0)),
        ],
        out_specs=pl.BlockSpec((tm, d), lambda i: (i, 0)),
        scratch_shapes=[
            pltpu.VMEM((2, d, tf), BF16),
            pltpu.VMEM((2, d, tf), BF16),
            pltpu.VMEM((2, tf, d), BF16),
            pltpu.SemaphoreType.DMA((3, 2)),
            pltpu.VMEM((tm, d), BF16),
        ],
        compiler_params=_compiler_params(("parallel",), BIG_VMEM_LIMIT_BYTES),
        name="ffn",
    )(x, sh, sc, gt, *([w_gu] * (2 * resident)), *([w_down] * resident), w_gu, w_down, g_final)


def _fprep_kernel(cc_ref, sc_ref, w_ref, ab_ref, *, scale):
    w = w_ref[0]
    hc = w.shape[0]
    a = jnp.dot(cc_ref[...], w, preferred_element_type=F32,
                precision=lax.Precision.HIGHEST)
    b = jnp.dot(sc_ref[...], w, preferred_element_type=F32,
                precision=lax.Precision.HIGHEST)
    ab_ref[0, :, :hc] = (a * scale).astype(BF16)
    ab_ref[0, :, hc:] = (b * scale).astype(BF16)


def _fprep(w_fourier, seq):
    n, hc, _ = w_fourier.shape
    ang = _dft_angles(hc, jnp.arange(hc), jnp.arange(hc))
    scale = 1.0 / math.sqrt(seq * hc)
    sq = pl.BlockSpec((hc, hc), lambda g: (0, 0))
    return pl.pallas_call(
        functools.partial(_fprep_kernel, scale=scale),
        out_shape=jax.ShapeDtypeStruct((n, hc, 2 * hc), BF16),
        grid=(n,),
        in_specs=[sq, sq, pl.BlockSpec((1, hc, hc), lambda g: (g, 0, 0))],
        out_specs=pl.BlockSpec((1, hc, 2 * hc), lambda g: (g, 0, 0)),
        compiler_params=_compiler_params(("parallel",)),
        name="fprep",
    )(jnp.cos(ang), jnp.sin(ang), w_fourier)


def _dft_angles(n, rows, cols):
    idx = (rows[:, None] * cols[None, :]) % n
    return idx.astype(F32) * (2.0 * math.pi / n)


def _mixin_kernel(x_ref, sh_ref, sc_ref, w_ref, ab_ref, cos_ref, sin_ref,
                  pab_ref, q_ref, k_ref, v_ref, *, q_scale):
    hc = HEAD_DIM
    dm = w_ref.shape[1] // 4
    groups = heads = dm // hc
    h = (_rms(x_ref[...]) * (1.0 + sc_ref[0]) + sh_ref[0]).astype(BF16)

    def project(n):
        return jnp.dot(h, w_ref[:, n * dm:(n + 1) * dm], preferred_element_type=F32)

    def rope_to(dst_ref, p, scale):
        cos = cos_ref[...] * scale
        sin = sin_ref[...] * scale
        for hh in range(heads):
            t = p[:, hh * hc:(hh + 1) * hc]
            dst_ref[0, hh] = t * cos + pltpu.roll(t, hc // 2, 1) * sin

    u = project(0).astype(BF16)
    pq = project(1)
    for g in range(groups):
        r = jnp.dot(u[:, g * hc:(g + 1) * hc], ab_ref[g], preferred_element_type=F32)
        pab_ref[:, g * hc:(g + 1) * hc] = r[:, :hc].astype(BF16)
        pab_ref[:, (groups + g) * hc:(groups + g + 1) * hc] = r[:, hc:].astype(BF16)
    rope_to(q_ref, pq, q_scale)
    rope_to(k_ref, project(2), 1.0)
    pv = project(3)
    for hh in range(heads):
        v_ref[0, hh] = pv[:, hh * hc:(hh + 1) * hc]


def _mixin(x, sh, sc, w_in, ab, cos, sin, *, layer, batch, seq, tm):
    t, d = x.shape
    dm = w_in.shape[2] // 4
    heads = dm // HEAD_DIM
    tpb = seq // tm
    vec = pl.BlockSpec((1, 1, d), lambda i: (i // tpb, 0, 0))
    tab = pl.BlockSpec((tm, HEAD_DIM), lambda i: (i % tpb, 0))
    hm = pl.BlockSpec((1, heads, tm, HEAD_DIM), lambda i: (i // tpb, 0, i % tpb, 0))
    hm_shape = jax.ShapeDtypeStruct((batch, heads, seq, HEAD_DIM), F32)
    resident = pl.Buffered(1)
    return pl.pallas_call(
        functools.partial(_mixin_kernel, q_scale=HEAD_DIM ** -0.5 * math.log2(math.e)),
        out_shape=(jax.ShapeDtypeStruct((t, 2 * dm), BF16),
                   hm_shape, hm_shape, hm_shape),
        grid=(t // tm,),
        in_specs=[
            pl.BlockSpec((tm, d), lambda i: (i, 0)),
            vec, vec,
            pl.BlockSpec((None, d, 4 * dm), lambda i: (layer, 0, 0), pipeline_mode=resident),
            pl.BlockSpec((None,) + ab.shape[1:], lambda i: (layer, 0, 0, 0),
                         pipeline_mode=resident),
            tab, tab,
        ],
        out_specs=(pl.BlockSpec((tm, 2 * dm), lambda i: (i, 0)), hm, hm, hm),
        compiler_params=_compiler_params(("parallel",)),
        name="mix_in",
    )(x, sh, sc, w_in, ab, cos, sin)


def _rope_tables(seq):
    half = HEAD_DIM // 2
    inv_freq = ROPE_THETA ** (-jnp.arange(half, dtype=F32) / half)
    ang = jnp.arange(seq, dtype=F32)[:, None] * inv_freq[None, :]
    cos, sin = jnp.cos(ang), jnp.sin(ang)
    return (jnp.concatenate([cos, cos], axis=-1),
            jnp.concatenate([-sin, sin], axis=-1))


DFT_RADIX = 4


def _dft_fold_kernel(z0_ref, z1_ref, z2_ref, z3_ref, w_ref):
    dm = w_ref.shape[3] // 2
    a = [z[0, 0, :, :dm].astype(F32) for z in (z0_ref, z1_ref, z2_ref, z3_ref)]
    p = [z[0, 0, :, dm:].astype(F32) for z in (z0_ref, z1_ref, z2_ref, z3_ref)]
    t0r, t0i = a[0] + a[2], -(p[0] + p[2])
    t1r, t1i = a[0] - a[2], p[2] - p[0]
    t2r, t2i = a[1] + a[3], -(p[1] + p[3])
    t3r, t3i = a[1] - a[3], p[3] - p[1]
    parts = ((t0r + t2r, t0i + t2i),
             (t1r + t3i, t1i - t3r),
             (t0r - t2r, t0i - t2i),
             (t1r - t3i, t1i + t3r))
    for r, (re, im) in enumerate(parts):
        w_ref[0, r, :, :dm] = re.astype(BF16)
        w_ref[0, r, :, dm:] = im.astype(BF16)


def _dft_fold(pab, *, batch, seq, tr):
    width = pab.shape[-1]
    nq = seq // DFT_RADIX
    z = pab.reshape(batch, DFT_RADIX, nq, width)
    quarter = [pl.BlockSpec((1, 1, tr, width), lambda b, i, q=q: (b, q, i, 0))
               for q in range(DFT_RADIX)]
    return pl.pallas_call(
        _dft_fold_kernel,
        out_shape=jax.ShapeDtypeStruct((batch, DFT_RADIX, nq, width), BF16),
        grid=(batch, nq // tr),
        in_specs=quarter,
        out_specs=pl.BlockSpec((1, DFT_RADIX, tr, width), lambda b, i: (b, 0, i, 0)),
        compiler_params=_compiler_params(("parallel", "parallel")),
        name="dft_fold",
    )(z, z, z, z)


def _dft_kernel(ca_ref, sa_ref, cb_ref, sb_ref, wre_ref, wim_ref, o_ref,
                cmat, smat, stage):
    r = pl.program_id(3)
    first = jnp.logical_and(pl.program_id(1) == 0, pl.program_id(2) == 0)

    @pl.when(first)
    def _():
        cb = cb_ref[0]
        sb = sb_ref[0]
        for a in range(ca_ref.shape[2]):
            ca = ca_ref[0, :, a:a + 1]
            sa = sa_ref[0, :, a:a + 1]
            cols = slice(a * LANES, (a + 1) * LANES)
            cmat[r, :, cols] = (ca * cb - sa * sb).astype(BF16)
            smat[r, :, cols] = (sa * cb + ca * sb).astype(BF16)

    y = (jnp.dot(cmat[r], wre_ref[0, 0], preferred_element_type=F32)
         + jnp.dot(smat[r], wim_ref[0, 0], preferred_element_type=F32))
    for cc in range(stage.shape[0]):
        stage[cc, pl.ds(r, y.shape[0], stride=DFT_RADIX), :] = (
            y[:, cc * LANES:(cc + 1) * LANES])

    @pl.when(r == DFT_RADIX - 1)
    def _():
        for cc in range(stage.shape[0]):
            o_ref[0, :, cc * LANES:(cc + 1) * LANES] = stage[cc]


def _dft(w, tables, *, batch, seq, tm, tn):
    dm = w.shape[-1] // 2
    nq = seq // DFT_RADIX
    ncb = dm // tn
    ca, sa, cb, sb = tables
    coarse = pl.BlockSpec((1, tm, ca.shape[2]), lambda i, b, c, r: (r, i, 0))
    fine = pl.BlockSpec((1, tm, LANES), lambda i, b, c, r: (r, i, 0))
    twiddle = pltpu.VMEM((DFT_RADIX, tm, nq), BF16)
    return pl.pallas_call(
        _dft_kernel,
        out_shape=jax.ShapeDtypeStruct((batch, seq, dm), F32),
        grid=(nq // tm, batch, ncb, DFT_RADIX),
        in_specs=[
            coarse, coarse, fine, fine,
            pl.BlockSpec((1, 1, nq, tn), lambda i, b, c, r: (b, r, 0, c)),
            pl.BlockSpec((1, 1, nq, tn), lambda i, b, c, r: (b, r, 0, c + ncb)),
        ],
        out_specs=pl.BlockSpec((1, DFT_RADIX * tm, tn), lambda i, b, c, r: (b, i, c)),
        scratch_shapes=[twiddle, twiddle,
                        pltpu.VMEM((tn // LANES, DFT_RADIX * tm, LANES), F32)],
        compiler_params=_compiler_params(
            ("parallel", "arbitrary", "arbitrary", "arbitrary")),
        name="dft",
    )(ca, sa, cb, sb, w, w)


def _dft_tables(seq):
    nq = seq // DFT_RADIX
    k = (DFT_RADIX * jnp.arange(nq)[None, :] + jnp.arange(DFT_RADIX)[:, None]).reshape(-1)
    coarse = _dft_angles(seq, k, LANES * jnp.arange(nq // LANES))
    fine = _dft_angles(seq, k, jnp.arange(LANES))
    return tuple(f(t).reshape(DFT_RADIX, nq, -1)
                 for t in (coarse, fine) for f in (jnp.cos, jnp.sin))


def _attn_kernel(q_ref, k_ref, v_ref, bias_ref, o_ref,
                 qp, kp, vp, s_buf, p_buf, pv, mb, lb):
    seq, hd = q_ref.shape[2], q_ref.shape[3]
    bq, kw = ATTN_BQ, ATTN_KW
    nblk = seq // bq

    assert kw == 2 * hd and s_buf.shape == (2, seq, hd)
    for src_ref, dst in ((q_ref, qp), (k_ref, kp), (v_ref, vp)):
        def load_prev(start, size, stride, src_ref=src_ref):
            return src_ref[0, 0, pl.ds(start, size, stride=stride), :]
        d_prev = 1
        for bi, d in enumerate(DILATIONS):
            ratio, sub, sub_prev = d // d_prev, seq // d, seq // d_prev
            piece = min(sub, ATTN_COPY_ROWS)
            for jp in range(d_prev):
                for r in range(ratio):
                    for c in range(sub // piece):
                        row0 = (jp + d_prev * r) * sub + c * piece
                        rows = slice(row0, row0 + piece)
                        x = load_prev(jp * sub_prev + r + c * piece * ratio, piece, ratio)
                        dst[bi, rows, :] = x.astype(BF16)
                        if 0 < bi < len(DILATIONS) - 1:
                            s_buf[bi % 2, rows, :] = x
            if d == 1:
                continue

            def load_prev(start, size, stride, half=bi % 2):
                return s_buf[half, pl.ds(start, size, stride=stride), :]
            d_prev = d

    ones = jnp.ones((kw, hd), BF16)

    for bi, d in enumerate(DILATIONS):
        sub = seq // d
        bps = sub // bq

        def place(n, d=d, sub=sub, bps=bps):
            j = n // bps
            t0 = (n % bps) * bq
            ks = jnp.clip(t0 - REACH, 0, sub - kw)
            qrows = pl.ds(pl.multiple_of(j * sub + t0, bq), bq)
            krows = pl.ds(pl.multiple_of(j * sub + ks, REACH), kw)
            if d > 1:
                orows = pl.ds(j + d * t0, bq, stride=d)
            else:
                orows = pl.ds(pl.multiple_of(t0, bq), bq)
            return qrows, krows, orows, (t0 - ks) // REACH

        def scores(n, carry, place=place, bi=bi):
            qrows, krows, _, edge = place(n)
            s = lax.dot_general(qp[bi, qrows, :], kp[bi, krows, :],
                                (((1,), (1,)), ((), ())), preferred_element_type=F32)
            s = s + bias_ref[edge]
            rows = pl.ds(pl.multiple_of(n * bq, bq), bq)
            s_buf[0, rows, :] = s[:, :hd]
            s_buf[1, rows, :] = s[:, hd:]
            return carry

        def softmax(n, carry, place=place, bi=bi):
            _, _, orows, _ = place(n)
            rows = pl.ds(pl.multiple_of(n * bq, bq), bq)
            s = jnp.concatenate([s_buf[0, rows, :], s_buf[1, rows, :]], axis=1)
            m = jnp.max(s, axis=-1, keepdims=True)
            p_buf[n] = jnp.exp2(s - m).astype(BF16)
            mb[bi, orows, :] = jnp.broadcast_to(m, (bq, hd))
            return carry

        def values(n, carry, place=place, bi=bi):
            _, krows, orows, _ = place(n)
            v1 = jnp.concatenate([vp[bi, krows, :], ones], axis=1)
            r = jnp.dot(p_buf[n], v1, preferred_element_type=F32)
            pv[bi, orows, :] = r[:, :hd]
            lb[bi, orows, :] = r[:, hd:]
            return carry

        lax.fori_loop(0, nblk, scores, 0, unroll=ATTN_UNROLL)
        lax.fori_loop(0, nblk, softmax, 0, unroll=ATTN_UNROLL)
        lax.fori_loop(0, nblk, values, 0, unroll=ATTN_UNROLL)

    def merge(c, carry):
        rows = pl.ds(pl.multiple_of(c * bq, bq), bq)
        branches = range(len(DILATIONS))
        ms = [mb[p, rows, :] for p in branches]
        top = functools.reduce(jnp.maximum, ms)
        es = [jnp.exp2(m - top) for m in ms]
        num = sum(es[p] * pv[p, rows, :] for p in branches)
        den = sum(es[p] * lb[p, rows, :] for p in branches)
        o_ref[0, rows, :] = num / den
        return carry

    lax.fori_loop(0, nblk, merge, 0, unroll=2)


def _attn_bias():
    r = jnp.arange(ATTN_BQ)[:, None]
    c = jnp.arange(ATTN_KW)[None, :]
    return jnp.stack([
        jnp.where(jnp.abs(edge * REACH + r - c) <= REACH, 0.0, MASK_VALUE).astype(F32)
        for edge in range(3)])


def _attn(q, k, v):
    batch, heads, seq, hd = q.shape
    nb = len(DILATIONS)
    nblk = seq // ATTN_BQ
    bias = _attn_bias()
    blk = pl.BlockSpec((1, 1, seq, hd), lambda b, h: (b, h, 0, 0))
    stat = pltpu.VMEM((nb, seq, hd), F32)
    gathered = pltpu.VMEM((nb, seq, hd), BF16)
    return pl.pallas_call(
        _attn_kernel,
        out_shape=jax.ShapeDtypeStruct((batch, seq, heads * hd), F32),
        grid=(batch, heads),
        in_specs=[blk, blk, blk, pl.BlockSpec(bias.shape, lambda b, h: (0, 0, 0))],
        out_specs=pl.BlockSpec((1, seq, hd), lambda b, h: (b, 0, h)),
        scratch_shapes=[
            gathered, gathered, gathered,
            pltpu.VMEM((ATTN_KW // hd, seq, hd), F32),
            pltpu.VMEM((nblk, ATTN_BQ, ATTN_KW), BF16),
            stat, stat, stat,
        ],
        compiler_params=_compiler_params(("parallel", "parallel"), BIG_VMEM_LIMIT_BYTES),
        name="attn",
    )(q, k, v, bias)


def _mixout_kernel(x_ref, yf_ref, ya_ref, gf_ref, ga_ref, gt_ref, w_ref, o_ref):
    df = yf_ref.shape[1]
    nf = (_rms(yf_ref[...]) * gf_ref[...]).astype(BF16)
    na = (_rms(ya_ref[...]) * ga_ref[...]).astype(BF16)
    out = (jnp.dot(nf, w_ref[:df, :], preferred_element_type=F32)
           + jnp.dot(na, w_ref[df:, :], preferred_element_type=F32))
    o_ref[...] = x_ref[...] + gt_ref[0] * out


def _mixout(x, yf, ya, gf, ga, gt, w_out, *, layer, seq, tm):
    t, d = x.shape
    df, da = yf.shape[1], ya.shape[1]
    tpb = seq // tm
    return pl.pallas_call(
        _mixout_kernel,
        out_shape=jax.ShapeDtypeStruct((t, d), F32),
        grid=(t // tm,),
        in_specs=[
            pl.BlockSpec((tm, d), lambda i: (i, 0)),
            pl.BlockSpec((tm, df), lambda i: (i, 0)),
            pl.BlockSpec((tm, da), lambda i: (i, 0)),
            pl.BlockSpec((1, df), lambda i: (0, 0)),
            pl.BlockSpec((1, da), lambda i: (0, 0)),
            pl.BlockSpec((1, 1, d), lambda i: (i // tpb, 0, 0)),
            pl.BlockSpec((None, df + da, d), lambda i: (layer, 0, 0)),
        ],
        out_specs=pl.BlockSpec((tm, d), lambda i: (i, 0)),
        compiler_params=_compiler_params(("parallel",)),
        name="mix_out",
    )(x, yf, ya, gf, ga, gt, w_out)


class _Tiles(NamedTuple):
    ffn_rows: int
    ffn_cols: int
    mix_rows: int
    fold_rows: int
    dft_rows: int
    dft_cols: int
    ada_cols: int


def _tiles(seq, d, d_ff):
    ffn_cols = 512 if d_ff % 512 == 0 else LANES
    return _Tiles(ffn_rows=min(512, seq), ffn_cols=ffn_cols, mix_rows=min(512, seq),
                  fold_rows=min(256, seq // DFT_RADIX),
                  dft_rows=min(512, seq // DFT_RADIX), dft_cols=min(1024, d // 2), ada_cols=min(1024, d))


def kernel(x, c, w_ada, b_ada, w_ffn1_gu, w_ffn1_down, w_mix_in, w_fourier,
           g_fourier_out, g_attn_out, w_mix_out, w_ffn2_gu, w_ffn2_down, g_final):
    batch, seq, d = x.shape
    depth = w_ada.shape[0]
    d_ff = w_ffn1_down.shape[1]
    t = batch * seq
    tl = _tiles(seq, d, d_ff)

    rows = 8 * pl.cdiv(batch, 8)
    c_pad = jnp.zeros((rows, d), F32).at[:batch].set(c)
    mod = _ada(c_pad, w_ada, b_ada, tn=tl.ada_cols)
    mod = mod[:, :batch].reshape(depth, batch, N_MOD, 1, d)

    groups = w_fourier.shape[1]
    ab = _fprep(w_fourier.reshape(depth * groups, HEAD_DIM, HEAD_DIM), seq)
    ab = ab.reshape(depth, groups, HEAD_DIM, 2 * HEAD_DIM)
    rope_cos, rope_sin = _rope_tables(seq)
    dft_tables = _dft_tables(seq)
    g_fin = g_final.reshape(1, d)

    wgu1, wd1, wgu2, wd2, w_in, w_out = (
        w.astype(BF16) for w in (w_ffn1_gu, w_ffn1_down, w_ffn2_gu, w_ffn2_down,
                                 w_mix_in, w_mix_out))

    xt = x.reshape(t, d)
    for l in range(depth):
        sh1, sc1, g1, sh2, sc2, g2, sh3, sc3, g3 = (mod[l, :, i] for i in range(N_MOD))
        xt = _ffn(xt, sh1, sc1, g1, wgu1, wd1, g_fin, layer=l,
                  seq=seq, tm=tl.ffn_rows, tf=tl.ffn_cols, final_norm=False)
        pab, q, k, v = _mixin(xt, sh2, sc2, w_in, ab, rope_cos, rope_sin, layer=l,
                              batch=batch, seq=seq, tm=tl.mix_rows)
        w = _dft_fold(pab, batch=batch, seq=seq, tr=tl.fold_rows)
        yf = _dft(w, dft_tables, batch=batch, seq=seq, tm=tl.dft_rows, tn=tl.dft_cols)
        ya = _attn(q, k, v)
        xt = _mixout(xt, yf.reshape(t, -1), ya.reshape(t, -1),
                     g_fourier_out[l].reshape(1, -1), g_attn_out[l].reshape(1, -1),
                     g2, w_out, layer=l, seq=seq, tm=tl.mix_rows)
        xt = _ffn(xt, sh3, sc3, g3, wgu2, wd2, g_fin, layer=l,
                  seq=seq, tm=tl.ffn_rows, tf=tl.ffn_cols, final_norm=(l == depth - 1))
    return xt.reshape(batch, seq, d)
```

```python
import functools
import math
from typing import NamedTuple

import jax
import jax.numpy as jnp
from jax import lax
from jax.experimental import pallas as pl
from jax.experimental.pallas import tpu as pltpu

F32 = jnp.float32
BF16 = jnp.bfloat16

EPS = 1e-6
HEAD_DIM = 128
ROPE_THETA = 10000.0
DILATED_PATTERNS = ((128, 1), (512, 4), (2048, 16))
DILATIONS = tuple(d for _, d in DILATED_PATTERNS)
REACH = (DILATED_PATTERNS[0][0] // 2) // DILATED_PATTERNS[0][1]
assert all((w // 2) // d == REACH for w, d in DILATED_PATTERNS)
MASK_VALUE = -1e30
N_MOD = 9

LANES = 128
SUBLANES = 8
VMEM_LIMIT_BYTES = 56 * 1024 * 1024
BIG_VMEM_LIMIT_BYTES = 62 * 1024 * 1024

DFT_RADIX = 4

FFN_COL_SPLIT = 2
FFN_RESIDENT_BLOCKS = 4

ATTN_BQ = 128
ATTN_KW = ATTN_BQ + 2 * REACH
ATTN_COPY_ROWS = 128
ATTN_UNROLL = True


def _compiler_params(semantics, vmem_limit_bytes=VMEM_LIMIT_BYTES):
    return pltpu.CompilerParams(dimension_semantics=semantics,
                                vmem_limit_bytes=vmem_limit_bytes)


def _rms(x):
    return x * lax.rsqrt(jnp.mean(x * x, axis=-1, keepdims=True) + EPS)


def _silu(x):
    return x * jax.nn.sigmoid(x)


def _ada_kernel(c_ref, w_ref, b_ref, o_ref):
    ca = _silu(c_ref[...]).astype(BF16)
    o_ref[0] = jnp.dot(ca, w_ref[0].astype(BF16),
                       preferred_element_type=F32) + b_ref[0]


def _ada(c_pad, w_ada, b_ada, *, tn):
    depth, d, n = w_ada.shape
    rows = c_pad.shape[0]
    return pl.pallas_call(
        _ada_kernel,
        out_shape=jax.ShapeDtypeStruct((depth, rows, n), F32),
        grid=(depth, n // tn),
        in_specs=[
            pl.BlockSpec((rows, d), lambda l, j: (0, 0)),
            pl.BlockSpec((1, d, tn), lambda l, j: (l, 0, j)),
            pl.BlockSpec((1, 1, tn), lambda l, j: (l, 0, j)),
        ],
        out_specs=pl.BlockSpec((1, rows, tn), lambda l, j: (l, 0, j)),
        compiler_params=_compiler_params(("parallel", "parallel")),
        name="ada",
    )(c_pad, w_ada, b_ada.reshape(depth, 1, n))


def _ffn_kernel(x_ref, sh_ref, sc_ref, gt_ref, *refs, layer, final_norm, resident):
    wg_res, wu_res, wd_res = (refs[k * resident:(k + 1) * resident] for k in range(3))
    (wgu_hbm, wd_hbm, gf_ref, o_ref,
     wg_buf, wu_buf, wd_buf, sems, h_ref) = refs[3 * resident:]
    d, tf = wg_res[0].shape
    f = wd_hbm.shape[1]
    nj = f // tf
    part = tf // FFN_COL_SPLIT

    def fetch(j):
        slot = (j - resident) % 2
        return (
            pltpu.make_async_copy(wgu_hbm.at[layer, :, pl.ds(j * tf, tf)],
                                  wg_buf.at[slot], sems.at[0, slot]),
            pltpu.make_async_copy(wgu_hbm.at[layer, :, pl.ds(f + j * tf, tf)],
                                  wu_buf.at[slot], sems.at[1, slot]),
            pltpu.make_async_copy(wd_hbm.at[layer, pl.ds(j * tf, tf), :],
                                  wd_buf.at[slot], sems.at[2, slot]),
        )

    def hidden_block(h, wg, wu, wd, first, last=False):
        gu = []
        for s in range(FFN_COL_SPLIT):
            cols = slice(s * part, (s + 1) * part)
            gu.append((jnp.dot(h, wg[:, cols], preferred_element_type=F32),
                       jnp.dot(h, wu[:, cols], preferred_element_type=F32)))
        for s, (g, u) in enumerate(gu):
            a = (_silu(g) * u).astype(BF16)
            r = jnp.dot(a, wd[s * part:(s + 1) * part, :], preferred_element_type=F32)
            if last and s == FFN_COL_SPLIT - 1:
                return r
            if first and s == 0:
                o_ref[...] = r
            else:
                o_ref[...] += r
        return None

    def streamed_block(j, slot, prefetch, last=False):
        for c in fetch(j):
            c.wait()
        if prefetch:
            for c in fetch(j + 1):
                c.start()
        return hidden_block(h_ref[...], wg_buf.at[slot], wu_buf.at[slot], wd_buf.at[slot],
                            False, last)

    assert nj * FFN_COL_SPLIT > 1
    streamed = nj - resident
    if streamed:
        for c in fetch(resident):
            c.start()
    h_ref[...] = (_rms(x_ref[...]) * (1.0 + sc_ref[0]) + sh_ref[0]).astype(BF16)
    rest = None
    for j in range(resident):
        rest = hidden_block(h_ref[...], wg_res[j], wu_res[j], wd_res[j], j == 0, j == nj - 1)

    tail = min(streamed, 2 + streamed % 2)
    pairs = (streamed - tail) // 2

    def pair(k, carry):
        streamed_block(resident + 2 * k, 0, True)
        streamed_block(resident + 2 * k + 1, 1, True)
        return carry

    if pairs:
        lax.fori_loop(0, pairs, pair, 0)
    for j in range(nj - tail, nj):
        rest = streamed_block(j, (j - resident) % 2, j + 1 < nj, j == nj - 1)

    y = x_ref[...] + (0.5 * gt_ref[0]) * (o_ref[...] + rest)
    if final_norm:
        y = _rms(y) * gf_ref[...]
    o_ref[...] = y


def _ffn(x, sh, sc, gt, w_gu, w_down, g_final, *, layer, seq, tm, tf, final_norm):
    t, d = x.shape
    f = w_down.shape[1]
    nj = f // tf
    tpb = seq // tm
    resident = min(FFN_RESIDENT_BLOCKS, nj)
    vec = pl.BlockSpec((1, 1, d), lambda i: (i // tpb, 0, 0))
    once = pl.Buffered(1)
    wg_res = [pl.BlockSpec((None, d, tf), lambda i, j=j: (layer, 0, j), pipeline_mode=once)
              for j in range(resident)]
    wu_res = [pl.BlockSpec((None, d, tf), lambda i, j=j: (layer, 0, nj + j), pipeline_mode=once)
              for j in range(resident)]
    wd_res = [pl.BlockSpec((None, tf, d), lambda i, j=j: (layer, j, 0), pipeline_mode=once)
              for j in range(resident)]
    return pl.pallas_call(
        functools.partial(_ffn_kernel, layer=layer, final_norm=final_norm, resident=resident),
        out_shape=jax.ShapeDtypeStruct((t, d), F32),
        grid=(t // tm,),
        in_specs=[
            pl.BlockSpec((tm, d), lambda i: (i, 0)),
            vec, vec, vec,
            *wg_res, *wu_res, *wd_res,
            pl.BlockSpec(memory_space=pl.ANY),
            pl.BlockSpec(memory_space=pl.ANY),
            pl.BlockSpec((1, d), lambda i: (0, 0)),
        ],
        out_specs=pl.BlockSpec((tm, d), lambda i: (i, 0)),
        scratch_shapes=[
            pltpu.VMEM((2, d, tf), BF16),
            pltpu.VMEM((2, d, tf), BF16),
            pltpu.VMEM((2, tf, d), BF16),
            pltpu.SemaphoreType.DMA((3, 2)),
            pltpu.VMEM((tm, d), BF16),
        ],
        compiler_params=_compiler_params(("parallel",), BIG_VMEM_LIMIT_BYTES),
        name="ffn",
    )(x, sh, sc, gt, *([w_gu] * (2 * resident)), *([w_down] * resident), w_gu, w_down, g_final)


def _fprep_kernel(cc_ref, sc_ref, w_ref, ab_ref, *, scale):
    w = w_ref[0]
    hc = w.shape[0]
    a = jnp.dot(cc_ref[...], w, preferred_element_type=F32,
                precision=lax.Precision.HIGHEST)
    b = jnp.dot(sc_ref[...], w, preferred_element_type=F32,
                precision=lax.Precision.HIGHEST)
    ab_ref[0, :, :hc] = (a * scale).astype(BF16)
    ab_ref[0, :, hc:] = (b * scale).astype(BF16)


def _fprep(w_fourier, seq):
    n, hc, _ = w_fourier.shape
    ang = _dft_angles(hc, jnp.arange(hc), jnp.arange(hc))
    scale = 1.0 / math.sqrt(seq * hc)
    sq = pl.BlockSpec((hc, hc), lambda g: (0, 0))
    return pl.pallas_call(
        functools.partial(_fprep_kernel, scale=scale),
        out_shape=jax.ShapeDtypeStruct((n, hc, 2 * hc), BF16),
        grid=(n,),
        in_specs=[sq, sq, pl.BlockSpec((1, hc, hc), lambda g: (g, 0, 0))],
        out_specs=pl.BlockSpec((1, hc, 2 * hc), lambda g: (g, 0, 0)),
        compiler_params=_compiler_params(("parallel",)),
        name="fprep",
    )(jnp.cos(ang), jnp.sin(ang), w_fourier)


def _dft_angles(n, rows, cols):
    idx = (rows[:, None] * cols[None, :]) % n
    return idx.astype(F32) * (2.0 * math.pi / n)


def _mixin_kernel(x_ref, sh_ref, sc_ref, w_ref, ab_ref, cos_ref, sin_ref,
                  wf_ref, q_ref, k_ref, v_ref, *, q_scale):
    hc = HEAD_DIM
    dm = w_ref.shape[1] // 4
    groups = heads = dm // hc
    nqt, tq = x_ref.shape[1], x_ref.shape[2]
    tm = nqt * tq
    x = x_ref[0].reshape(tm, x_ref.shape[3])
    h = (_rms(x) * (1.0 + sc_ref[0]) + sh_ref[0]).astype(BF16)

    def project(n):
        return jnp.dot(h, w_ref[:, n * dm:(n + 1) * dm], preferred_element_type=F32)

    def rope_to(dst_ref, p, scale):
        cos = cos_ref[...].reshape(tm, hc) * scale
        sin = sin_ref[...].reshape(tm, hc) * scale
        for hh in range(heads):
            t = p[:, hh * hc:(hh + 1) * hc]
            dst_ref[0, hh] = (t * cos + pltpu.roll(t, hc // 2, 1) * sin).reshape(nqt, tq, hc)

    u = project(0).astype(BF16)
    pq = project(1)
    for g in range(groups):
        r = jnp.dot(u[:, g * hc:(g + 1) * hc], ab_ref[g], preferred_element_type=F32)
        a = [r[q * tq:(q + 1) * tq, :hc] for q in range(nqt)]
        p = [r[q * tq:(q + 1) * tq, hc:] for q in range(nqt)]
        t0r, t0i = a[0] + a[2], -(p[0] + p[2])
        t1r, t1i = a[0] - a[2], p[2] - p[0]
        t2r, t2i = a[1] + a[3], -(p[1] + p[3])
        t3r, t3i = a[1] - a[3], p[3] - p[1]
        parts = ((t0r + t2r, t0i + t2i),
                 (t1r + t3i, t1i - t3r),
                 (t0r - t2r, t0i - t2i),
                 (t1r - t3i, t1i + t3r))
        for res, (re, im) in enumerate(parts):
            wf_ref[0, res, :, g * hc:(g + 1) * hc] = re.astype(BF16)
            wf_ref[0, res, :, dm + g * hc:dm + (g + 1) * hc] = im.astype(BF16)
    rope_to(q_ref, pq, q_scale)
    rope_to(k_ref, project(2), 1.0)
    pv = project(3)
    for hh in range(heads):
        v_ref[0, hh] = pv[:, hh * hc:(hh + 1) * hc].reshape(nqt, tq, hc)


def _mixin(x, sh, sc, w_in, ab, cos, sin, *, layer, batch, seq, tm):
    t, d = x.shape
    dm = w_in.shape[2] // 4
    heads = dm // HEAD_DIM
    nq, tq = seq // DFT_RADIX, tm // DFT_RADIX
    tpb = nq // tq
    vec = pl.BlockSpec((1, 1, d), lambda i: (i // tpb, 0, 0))
    tab = pl.BlockSpec((DFT_RADIX, tq, HEAD_DIM), lambda i: (0, i % tpb, 0))
    hm = pl.BlockSpec((1, heads, DFT_RADIX, tq, HEAD_DIM),
                      lambda i: (i // tpb, 0, 0, i % tpb, 0))
    hm_shape = jax.ShapeDtypeStruct((batch, heads, DFT_RADIX, nq, HEAD_DIM), F32)
    resident = pl.Buffered(1)
    wf, q, k, v = pl.pallas_call(
        functools.partial(_mixin_kernel, q_scale=HEAD_DIM ** -0.5 * math.log2(math.e)),
        out_shape=(jax.ShapeDtypeStruct((batch, DFT_RADIX, nq, 2 * dm), BF16),
                   hm_shape, hm_shape, hm_shape),
        grid=(t // tm,),
        in_specs=[
            pl.BlockSpec((1, DFT_RADIX, tq, d), lambda i: (i // tpb, 0, i % tpb, 0)),
            vec, vec,
            pl.BlockSpec((None, d, 4 * dm), lambda i: (layer, 0, 0), pipeline_mode=resident),
            pl.BlockSpec((None,) + ab.shape[1:], lambda i: (layer, 0, 0, 0),
                         pipeline_mode=resident),
            tab, tab,
        ],
        out_specs=(pl.BlockSpec((1, DFT_RADIX, tq, 2 * dm), lambda i: (i // tpb, 0, i % tpb, 0)),
                   hm, hm, hm),
        compiler_params=_compiler_params(("parallel",)),
        name="mix_in",
    )(x.reshape(batch, DFT_RADIX, nq, d), sh, sc, w_in, ab,
      cos.reshape(DFT_RADIX, nq, HEAD_DIM), sin.reshape(DFT_RADIX, nq, HEAD_DIM))
    return wf, *(a.reshape(batch, heads, seq, HEAD_DIM) for a in (q, k, v))


def _rope_tables(seq):
    half = HEAD_DIM // 2
    inv_freq = ROPE_THETA ** (-jnp.arange(half, dtype=F32) / half)
    ang = jnp.arange(seq, dtype=F32)[:, None] * inv_freq[None, :]
    cos, sin = jnp.cos(ang), jnp.sin(ang)
    return (jnp.concatenate([cos, cos], axis=-1),
            jnp.concatenate([-sin, sin], axis=-1))


def _dft_kernel(ca_ref, sa_ref, cb_ref, sb_ref, wre_ref, wim_ref, o_ref,
                cmat, smat, stage):
    r = pl.program_id(3)
    first = jnp.logical_and(pl.program_id(1) == 0, pl.program_id(2) == 0)

    @pl.when(first)
    def _():
        cb = cb_ref[0]
        sb = sb_ref[0]
        for a in range(ca_ref.shape[2]):
            ca = ca_ref[0, :, a:a + 1]
            sa = sa_ref[0, :, a:a + 1]
            cols = slice(a * LANES, (a + 1) * LANES)
            cmat[r, :, cols] = (ca * cb - sa * sb).astype(BF16)
            smat[r, :, cols] = (sa * cb + ca * sb).astype(BF16)

    y = (jnp.dot(cmat[r], wre_ref[0, 0], preferred_element_type=F32)
         + jnp.dot(smat[r], wim_ref[0, 0], preferred_element_type=F32))
    for cc in range(stage.shape[0]):
        stage[cc, pl.ds(r, y.shape[0], stride=DFT_RADIX), :] = (
            y[:, cc * LANES:(cc + 1) * LANES])

    @pl.when(r == DFT_RADIX - 1)
    def _():
        for cc in range(stage.shape[0]):
            o_ref[0, :, cc * LANES:(cc + 1) * LANES] = stage[cc]


def _dft(w, tables, *, batch, seq, tm, tn):
    dm = w.shape[-1] // 2
    nq = seq // DFT_RADIX
    ncb = dm // tn
    ca, sa, cb, sb = tables
    coarse = pl.BlockSpec((1, tm, ca.shape[2]), lambda i, b, c, r: (r, i, 0))
    fine = pl.BlockSpec((1, tm, LANES), lambda i, b, c, r: (r, i, 0))
    twiddle = pltpu.VMEM((DFT_RADIX, tm, nq), BF16)
    return pl.pallas_call(
        _dft_kernel,
        out_shape=jax.ShapeDtypeStruct((batch, seq, dm), F32),
        grid=(nq // tm, batch, ncb, DFT_RADIX),
        in_specs=[
            coarse, coarse, fine, fine,
            pl.BlockSpec((1, 1, nq, tn), lambda i, b, c, r: (b, r, 0, c)),
            pl.BlockSpec((1, 1, nq, tn), lambda i, b, c, r: (b, r, 0, c + ncb)),
        ],
        out_specs=pl.BlockSpec((1, DFT_RADIX * tm, tn), lambda i, b, c, r: (b, i, c)),
        scratch_shapes=[twiddle, twiddle,
                        pltpu.VMEM((tn // LANES, DFT_RADIX * tm, LANES), F32)],
        compiler_params=_compiler_params(
            ("parallel", "arbitrary", "arbitrary", "arbitrary")),
        name="dft",
    )(ca, sa, cb, sb, w, w)


def _dft_tables(seq):
    nq = seq // DFT_RADIX
    k = (DFT_RADIX * jnp.arange(nq)[None, :] + jnp.arange(DFT_RADIX)[:, None]).reshape(-1)
    coarse = _dft_angles(seq, k, LANES * jnp.arange(nq // LANES))
    fine = _dft_angles(seq, k, jnp.arange(LANES))
    return tuple(f(t).reshape(DFT_RADIX, nq, -1)
                 for t in (coarse, fine) for f in (jnp.cos, jnp.sin))


def _attn_kernel(q_ref, k_ref, v_ref, bias_ref, o_ref,
                 qp, kp, vp, s_buf, p_buf, pv, mb, lb):
    seq, hd = q_ref.shape[2], q_ref.shape[3]
    bq, kw = ATTN_BQ, ATTN_KW
    nblk = seq // bq

    assert kw == 2 * hd and s_buf.shape == (2, seq, hd)
    for src_ref, dst in ((q_ref, qp), (k_ref, kp), (v_ref, vp)):
        def load_prev(start, size, stride, src_ref=src_ref):
            return src_ref[0, 0, pl.ds(start, size, stride=stride), :]
        d_prev = 1
        for bi, d in enumerate(DILATIONS):
            ratio, sub, sub_prev = d // d_prev, seq // d, seq // d_prev
            piece = min(sub, ATTN_COPY_ROWS)
            for jp in range(d_prev):
                for r in range(ratio):
                    for c in range(sub // piece):
                        row0 = (jp + d_prev * r) * sub + c * piece
                        rows = slice(row0, row0 + piece)
                        x = load_prev(jp * sub_prev + r + c * piece * ratio, piece, ratio)
                        dst[bi, rows, :] = x.astype(BF16)
                        if 0 < bi < len(DILATIONS) - 1:
                            s_buf[bi % 2, rows, :] = x
            if d == 1:
                continue

            def load_prev(start, size, stride, half=bi % 2):
                return s_buf[half, pl.ds(start, size, stride=stride), :]
            d_prev = d

    ones = jnp.ones((kw, hd), BF16)

    for bi, d in enumerate(DILATIONS):
        sub = seq // d
        bps = sub // bq

        def place(n, d=d, sub=sub, bps=bps):
            j = n // bps
            t0 = (n % bps) * bq
            ks = jnp.clip(t0 - REACH, 0, sub - kw)
            qrows = pl.ds(pl.multiple_of(j * sub + t0, bq), bq)
            krows = pl.ds(pl.multiple_of(j * sub + ks, REACH), kw)
            if d > 1:
                orows = pl.ds(j + d * t0, bq, stride=d)
            else:
                orows = pl.ds(pl.multiple_of(t0, bq), bq)
            return qrows, krows, orows, (t0 - ks) // REACH

        def scores(n, carry, place=place, bi=bi):
            qrows, krows, _, edge = place(n)
            s = lax.dot_general(qp[bi, qrows, :], kp[bi, krows, :],
                                (((1,), (1,)), ((), ())), preferred_element_type=F32)
            s = s + bias_ref[edge]
            rows = pl.ds(pl.multiple_of(n * bq, bq), bq)
            s_buf[0, rows, :] = s[:, :hd]
            s_buf[1, rows, :] = s[:, hd:]
            return carry

        def softmax(n, carry, place=place, bi=bi):
            _, _, orows, _ = place(n)
            rows = pl.ds(pl.multiple_of(n * bq, bq), bq)
            s = jnp.concatenate([s_buf[0, rows, :], s_buf[1, rows, :]], axis=1)
            m = jnp.max(s, axis=-1, keepdims=True)
            p_buf[n] = jnp.exp2(s - m).astype(BF16)
            mb[bi, orows, :] = jnp.broadcast_to(m, (bq, hd))
            return carry

        def values(n, carry, place=place, bi=bi):
            _, krows, orows, _ = place(n)
            v1 = jnp.concatenate([vp[bi, krows, :], ones], axis=1)
            r = jnp.dot(p_buf[n], v1, preferred_element_type=F32)
            pv[bi, orows, :] = r[:, :hd]
            lb[bi, orows, :] = r[:, hd:]
            return carry

        lax.fori_loop(0, nblk, scores, 0, unroll=ATTN_UNROLL)
        lax.fori_loop(0, nblk, softmax, 0, unroll=ATTN_UNROLL)
        lax.fori_loop(0, nblk, values, 0, unroll=ATTN_UNROLL)

    def merge(c, carry):
        rows = pl.ds(pl.multiple_of(c * bq, bq), bq)
        branches = range(len(DILATIONS))
        ms = [mb[p, rows, :] for p in branches]
        top = functools.reduce(jnp.maximum, ms)
        es = [jnp.exp2(m - top) for m in ms]
        num = sum(es[p] * pv[p, rows, :] for p in branches)
        den = sum(es[p] * lb[p, rows, :] for p in branches)
        o_ref[0, rows, :] = num / den
        return carry

    lax.fori_loop(0, nblk, merge, 0, unroll=2)


def _attn_bias():
    r = jnp.arange(ATTN_BQ)[:, None]
    c = jnp.arange(ATTN_KW)[None, :]
    return jnp.stack([
        jnp.where(jnp.abs(edge * REACH + r - c) <= REACH, 0.0, MASK_VALUE).astype(F32)
        for edge in range(3)])


def _attn(q, k, v):
    batch, heads, seq, hd = q.shape
    nb = len(DILATIONS)
    nblk = seq // ATTN_BQ
    bias = _attn_bias()
    blk = pl.BlockSpec((1, 1, seq, hd), lambda b, h: (b, h, 0, 0))
    stat = pltpu.VMEM((nb, seq, hd), F32)
    gathered = pltpu.VMEM((nb, seq, hd), BF16)
    return pl.pallas_call(
        _attn_kernel,
        out_shape=jax.ShapeDtypeStruct((batch, seq, heads * hd), F32),
        grid=(batch, heads),
        in_specs=[blk, blk, blk, pl.BlockSpec(bias.shape, lambda b, h: (0, 0, 0))],
        out_specs=pl.BlockSpec((1, seq, hd), lambda b, h: (b, 0, h)),
        scratch_shapes=[
            gathered, gathered, gathered,
            pltpu.VMEM((ATTN_KW // hd, seq, hd), F32),
            pltpu.VMEM((nblk, ATTN_BQ, ATTN_KW), BF16),
            stat, stat, stat,
        ],
        compiler_params=_compiler_params(("parallel", "parallel"), BIG_VMEM_LIMIT_BYTES),
        name="attn",
    )(q, k, v, bias)


def _mixout_kernel(x_ref, yf_ref, ya_ref, gf_ref, ga_ref, gt_ref, w_ref, o_ref):
    df = yf_ref.shape[1]
    nf = (_rms(yf_ref[...]) * gf_ref[...]).astype(BF16)
    na = (_rms(ya_ref[...]) * ga_ref[...]).astype(BF16)
    out = (jnp.dot(nf, w_ref[:df, :], preferred_element_type=F32)
           + jnp.dot(na, w_ref[df:, :], preferred_element_type=F32))
    o_ref[...] = x_ref[...] + gt_ref[0] * out


def _mixout(x, yf, ya, gf, ga, gt, w_out, *, layer, seq, tm):
    t, d = x.shape
    df, da = yf.shape[1], ya.shape[1]
    tpb = seq // tm
    return pl.pallas_call(
        _mixout_kernel,
        out_shape=jax.ShapeDtypeStruct((t, d), F32),
        grid=(t // tm,),
        in_specs=[
            pl.BlockSpec((tm, d), lambda i: (i, 0)),
            pl.BlockSpec((tm, df), lambda i: (i, 0)),
            pl.BlockSpec((tm, da), lambda i: (i, 0)),
            pl.BlockSpec((1, df), lambda i: (0, 0)),
            pl.BlockSpec((1, da), lambda i: (0, 0)),
            pl.BlockSpec((1, 1, d), lambda i: (i // tpb, 0, 0)),
            pl.BlockSpec((None, df + da, d), lambda i: (layer, 0, 0)),
        ],
        out_specs=pl.BlockSpec((tm, d), lambda i: (i, 0)),
        compiler_params=_compiler_params(("parallel",)),
        name="mix_out",
    )(x, yf, ya, gf, ga, gt, w_out)


class _Tiles(NamedTuple):
    ffn_rows: int
    ffn_cols: int
    mix_rows: int
    dft_rows: int
    dft_cols: int
    ada_cols: int


def _tiles(seq, d, d_ff):
    ffn_cols = 512 if d_ff % 512 == 0 else LANES
    return _Tiles(ffn_rows=min(512, seq), ffn_cols=ffn_cols, mix_rows=min(512, seq),
                  dft_rows=min(512, seq // DFT_RADIX), dft_cols=min(1024, d // 2), ada_cols=min(1024, d))


def kernel(x, c, w_ada, b_ada, w_ffn1_gu, w_ffn1_down, w_mix_in, w_fourier,
           g_fourier_out, g_attn_out, w_mix_out, w_ffn2_gu, w_ffn2_down, g_final):
    batch, seq, d = x.shape
    depth = w_ada.shape[0]
    d_ff = w_ffn1_down.shape[1]
    t = batch * seq
    tl = _tiles(seq, d, d_ff)

    rows = SUBLANES * pl.cdiv(batch, SUBLANES)
    c_pad = jnp.zeros((rows, d), F32).at[:batch].set(c)
    mod = _ada(c_pad, w_ada, b_ada, tn=tl.ada_cols)
    mod = mod[:, :batch].reshape(depth, batch, N_MOD, 1, d)

    groups = w_fourier.shape[1]
    ab = _fprep(w_fourier.reshape(depth * groups, HEAD_DIM, HEAD_DIM), seq)
    ab = ab.reshape(depth, groups, HEAD_DIM, 2 * HEAD_DIM)
    rope_cos, rope_sin = _rope_tables(seq)
    dft_tables = _dft_tables(seq)
    g_fin = g_final.reshape(1, d)

    wgu1, wd1, wgu2, wd2, w_in, w_out = (
        w.astype(BF16) for w in (w_ffn1_gu, w_ffn1_down, w_ffn2_gu, w_ffn2_down,
                                 w_mix_in, w_mix_out))

    xt = x.reshape(t, d)
    for l in range(depth):
        sh1, sc1, g1, sh2, sc2, g2, sh3, sc3, g3 = (mod[l, :, i] for i in range(N_MOD))
        xt = _ffn(xt, sh1, sc1, g1, wgu1, wd1, g_fin, layer=l,
                  seq=seq, tm=tl.ffn_rows, tf=tl.ffn_cols, final_norm=False)
        wf, q, k, v = _mixin(xt, sh2, sc2, w_in, ab, rope_cos, rope_sin, layer=l,
                             batch=batch, seq=seq, tm=tl.mix_rows)
        yf = _dft(wf, dft_tables, batch=batch, seq=seq, tm=tl.dft_rows, tn=tl.dft_cols)
        ya = _attn(q, k, v)
        xt = _mixout(xt, yf.reshape(t, -1), ya.reshape(t, -1),
                     g_fourier_out[l].reshape(1, -1), g_attn_out[l].reshape(1, -1),
                     g2, w_out, layer=l, seq=seq, tm=tl.mix_rows)
        xt = _ffn(xt, sh3, sc3, g3, wgu2, wd2, g_fin, layer=l,
                  seq=seq, tm=tl.ffn_rows, tf=tl.ffn_cols, final_norm=(l == depth - 1))
    return xt.reshape(batch, seq, d)
```

```python
import functools
import math
from typing import NamedTuple

import jax
import jax.numpy as jnp
from jax import lax
from jax.experimental import pallas as pl
from jax.experimental.pallas import tpu as pltpu

F32 = jnp.float32
BF16 = jnp.bfloat16

EPS = 1e-6
HEAD_DIM = 128
ROPE_THETA = 10000.0
DILATED_PATTERNS = ((128, 1), (512, 4), (2048, 16))
DILATIONS = tuple(d for _, d in DILATED_PATTERNS)
REACH = (DILATED_PATTERNS[0][0] // 2) // DILATED_PATTERNS[0][1]
assert all((w // 2) // d == REACH for w, d in DILATED_PATTERNS)
MASK_VALUE = -1e30
N_MOD = 9

LANES = 128
SUBLANES = 8
VMEM_LIMIT_BYTES = 56 * 1024 * 1024
BIG_VMEM_LIMIT_BYTES = 62 * 1024 * 1024

DFT_RADIX = 4

FFN_COL_SPLIT = 2
FFN_RESIDENT_BLOCKS = 4

ATTN_BQ = 128
ATTN_KW = ATTN_BQ + 2 * REACH
ATTN_COPY_ROWS = 128
ATTN_GATHER_ROWS = 256
ATTN_UNROLL = True


def _compiler_params(semantics, vmem_limit_bytes=VMEM_LIMIT_BYTES):
    return pltpu.CompilerParams(dimension_semantics=semantics,
                                vmem_limit_bytes=vmem_limit_bytes)


def _rms(x):
    return x * lax.rsqrt(jnp.mean(x * x, axis=-1, keepdims=True) + EPS)


def _silu(x):
    return x * jax.nn.sigmoid(x)


def _ada_kernel(c_ref, w_ref, b_ref, o_ref):
    ca = _silu(c_ref[...]).astype(BF16)
    o_ref[0] = jnp.dot(ca, w_ref[0].astype(BF16),
                       preferred_element_type=F32) + b_ref[0]


def _ada(c_pad, w_ada, b_ada, *, tn):
    depth, d, n = w_ada.shape
    rows = c_pad.shape[0]
    return pl.pallas_call(
        _ada_kernel,
        out_shape=jax.ShapeDtypeStruct((depth, rows, n), F32),
        grid=(depth, n // tn),
        in_specs=[
            pl.BlockSpec((rows, d), lambda l, j: (0, 0)),
            pl.BlockSpec((1, d, tn), lambda l, j: (l, 0, j)),
            pl.BlockSpec((1, 1, tn), lambda l, j: (l, 0, j)),
        ],
        out_specs=pl.BlockSpec((1, rows, tn), lambda l, j: (l, 0, j)),
        compiler_params=_compiler_params(("parallel", "parallel")),
        name="ada",
    )(c_pad, w_ada, b_ada.reshape(depth, 1, n))


def _ffn_kernel(x_ref, sh_ref, sc_ref, gt_ref, *refs, layer, final_norm, resident):
    wg_res, wu_res, wd_res = (refs[k * resident:(k + 1) * resident] for k in range(3))
    (wgu_hbm, wd_hbm, gf_ref, o_ref,
     wg_buf, wu_buf, wd_buf, sems, h_ref) = refs[3 * resident:]
    d, tf = wg_res[0].shape
    f = wd_hbm.shape[1]
    nj = f // tf
    part = tf // FFN_COL_SPLIT

    def fetch(j):
        slot = (j - resident) % 2
        return (
            pltpu.make_async_copy(wgu_hbm.at[layer, :, pl.ds(j * tf, tf)],
                                  wg_buf.at[slot], sems.at[0, slot]),
            pltpu.make_async_copy(wgu_hbm.at[layer, :, pl.ds(f + j * tf, tf)],
                                  wu_buf.at[slot], sems.at[1, slot]),
            pltpu.make_async_copy(wd_hbm.at[layer, pl.ds(j * tf, tf), :],
                                  wd_buf.at[slot], sems.at[2, slot]),
        )

    def hidden_block(h, wg, wu, wd, first):
        gu = []
        for s in range(FFN_COL_SPLIT):
            cols = slice(s * part, (s + 1) * part)
            gu.append((jnp.dot(h, wg[:, cols], preferred_element_type=F32),
                       jnp.dot(h, wu[:, cols], preferred_element_type=F32)))
        for s, (g, u) in enumerate(gu):
            a = (_silu(g) * u).astype(BF16)
            r = jnp.dot(a, wd[s * part:(s + 1) * part, :], preferred_element_type=F32)
            if first and s == 0:
                o_ref[...] = r
            else:
                o_ref[...] += r

    def streamed_block(j, slot, prefetch):
        for c in fetch(j):
            c.wait()
        if prefetch:
            for c in fetch(j + 1):
                c.start()
        hidden_block(h_ref[...], wg_buf.at[slot], wu_buf.at[slot], wd_buf.at[slot], False)

    streamed = nj - resident
    if streamed:
        for c in fetch(resident):
            c.start()
    h_ref[...] = (_rms(x_ref[...]) * (1.0 + sc_ref[0]) + sh_ref[0]).astype(BF16)
    for j in range(resident):
        hidden_block(h_ref[...], wg_res[j], wu_res[j], wd_res[j], j == 0)

    tail = min(streamed, 2 + streamed % 2)
    pairs = (streamed - tail) // 2

    def pair(k, carry):
        streamed_block(resident + 2 * k, 0, True)
        streamed_block(resident + 2 * k + 1, 1, True)
        return carry

    if pairs:
        lax.fori_loop(0, pairs, pair, 0)
    for j in range(nj - tail, nj):
        streamed_block(j, (j - resident) % 2, j + 1 < nj)

    y = x_ref[...] + (0.5 * gt_ref[0]) * o_ref[...]
    if final_norm:
        y = _rms(y) * gf_ref[...]
    o_ref[...] = y


def _ffn(x, sh, sc, gt, w_gu, w_down, g_final, *, layer, seq, tm, tf, final_norm):
    t, d = x.shape
    f = w_down.shape[1]
    nj = f // tf
    tpb = seq // tm
    resident = min(FFN_RESIDENT_BLOCKS, nj)
    vec = pl.BlockSpec((1, 1, d), lambda i: (i // tpb, 0, 0))
    once = pl.Buffered(1)
    wg_res = [pl.BlockSpec((None, d, tf), lambda i, j=j: (layer, 0, j), pipeline_mode=once)
              for j in range(resident)]
    wu_res = [pl.BlockSpec((None, d, tf), lambda i, j=j: (layer, 0, nj + j), pipeline_mode=once)
              for j in range(resident)]
    wd_res = [pl.BlockSpec((None, tf, d), lambda i, j=j: (layer, j, 0), pipeline_mode=once)
              for j in range(resident)]
    return pl.pallas_call(
        functools.partial(_ffn_kernel, layer=layer, final_norm=final_norm, resident=resident),
        out_shape=jax.ShapeDtypeStruct((t, d), F32),
        grid=(t // tm,),
        in_specs=[
            pl.BlockSpec((tm, d), lambda i: (i, 0)),
            vec, vec, vec,
            *wg_res, *wu_res, *wd_res,
            pl.BlockSpec(memory_space=pl.ANY),
            pl.BlockSpec(memory_space=pl.ANY),
            pl.BlockSpec((1, d), lambda i: (0, 0)),
        ],
        out_specs=pl.BlockSpec((tm, d), lambda i: (i, 0)),
        scratch_shapes=[
            pltpu.VMEM((2, d, tf), BF16),
            pltpu.VMEM((2, d, tf), BF16),
            pltpu.VMEM((2, tf, d), BF16),
            pltpu.SemaphoreType.DMA((3, 2)),
            pltpu.VMEM((tm, d), BF16),
        ],
        compiler_params=_compiler_params(("parallel",), BIG_VMEM_LIMIT_BYTES),
        name="ffn",
    )(x, sh, sc, gt, *([w_gu] * (2 * resident)), *([w_down] * resident), w_gu, w_down, g_final)


def _fprep_kernel(cc_ref, sc_ref, w_ref, ab_ref, *, scale):
    w = w_ref[0]
    hc = w.shape[0]
    a = jnp.dot(cc_ref[...], w, preferred_element_type=F32,
                precision=lax.Precision.HIGHEST)
    b = jnp.dot(sc_ref[...], w, preferred_element_type=F32,
                precision=lax.Precision.HIGHEST)
    ab_ref[0, :, :hc] = (a * scale).astype(BF16)
    ab_ref[0, :, hc:] = (b * scale).astype(BF16)


def _fprep(w_fourier, seq):
    n, hc, _ = w_fourier.shape
    ang = _dft_angles(hc, jnp.arange(hc), jnp.arange(hc))
    scale = 1.0 / math.sqrt(seq * hc)
    sq = pl.BlockSpec((hc, hc), lambda g: (0, 0))
    return pl.pallas_call(
        functools.partial(_fprep_kernel, scale=scale),
        out_shape=jax.ShapeDtypeStruct((n, hc, 2 * hc), BF16),
        grid=(n,),
        in_specs=[sq, sq, pl.BlockSpec((1, hc, hc), lambda g: (g, 0, 0))],
        out_specs=pl.BlockSpec((1, hc, 2 * hc), lambda g: (g, 0, 0)),
        compiler_params=_compiler_params(("parallel",)),
        name="fprep",
    )(jnp.cos(ang), jnp.sin(ang), w_fourier)


def _dft_angles(n, rows, cols):
    idx = (rows[:, None] * cols[None, :]) % n
    return idx.astype(F32) * (2.0 * math.pi / n)


def _mixin_kernel(x_ref, sh_ref, sc_ref, w_ref, ab_ref, cos_ref, sin_ref,
                  wf_ref, q_ref, k_ref, v_ref, *, q_scale):
    hc = HEAD_DIM
    dm = w_ref.shape[1] // 4
    groups = heads = dm // hc
    nqt, tq = x_ref.shape[1], x_ref.shape[2]
    tm = nqt * tq
    x = x_ref[0].reshape(tm, x_ref.shape[3])
    h = (_rms(x) * (1.0 + sc_ref[0]) + sh_ref[0]).astype(BF16)

    def project(n):
        return jnp.dot(h, w_ref[:, n * dm:(n + 1) * dm], preferred_element_type=F32)

    def rope_to(dst_ref, p, scale):
        cos = cos_ref[...].reshape(tm, hc) * scale
        sin = sin_ref[...].reshape(tm, hc) * scale
        for hh in range(heads):
            t = p[:, hh * hc:(hh + 1) * hc]
            dst_ref[0, hh] = (t * cos + pltpu.roll(t, hc // 2, 1) * sin).reshape(nqt, tq, hc)

    u = project(0).astype(BF16)
    pq = project(1)
    for g in range(groups):
        r = jnp.dot(u[:, g * hc:(g + 1) * hc], ab_ref[g], preferred_element_type=F32)
        a = [r[q * tq:(q + 1) * tq, :hc] for q in range(nqt)]
        p = [r[q * tq:(q + 1) * tq, hc:] for q in range(nqt)]
        t0r, t0i = a[0] + a[2], -(p[0] + p[2])
        t1r, t1i = a[0] - a[2], p[2] - p[0]
        t2r, t2i = a[1] + a[3], -(p[1] + p[3])
        t3r, t3i = a[1] - a[3], p[3] - p[1]
        parts = ((t0r + t2r, t0i + t2i),
                 (t1r + t3i, t1i - t3r),
                 (t0r - t2r, t0i - t2i),
                 (t1r - t3i, t1i + t3r))
        for res, (re, im) in enumerate(parts):
            wf_ref[0, res, :, g * hc:(g + 1) * hc] = re.astype(BF16)
            wf_ref[0, res, :, dm + g * hc:dm + (g + 1) * hc] = im.astype(BF16)
    rope_to(q_ref, pq, q_scale)
    rope_to(k_ref, project(2), 1.0)
    pv = project(3)
    for hh in range(heads):
        v_ref[0, hh] = pv[:, hh * hc:(hh + 1) * hc].reshape(nqt, tq, hc)


def _mixin(x, sh, sc, w_in, ab, cos, sin, *, layer, batch, seq, tm):
    t, d = x.shape
    dm = w_in.shape[2] // 4
    heads = dm // HEAD_DIM
    nq, tq = seq // DFT_RADIX, tm // DFT_RADIX
    tpb = nq // tq
    vec = pl.BlockSpec((1, 1, d), lambda i: (i // tpb, 0, 0))
    tab = pl.BlockSpec((DFT_RADIX, tq, HEAD_DIM), lambda i: (0, i % tpb, 0))
    hm = pl.BlockSpec((1, heads, DFT_RADIX, tq, HEAD_DIM),
                      lambda i: (i // tpb, 0, 0, i % tpb, 0))
    hm_shape = jax.ShapeDtypeStruct((batch, heads, DFT_RADIX, nq, HEAD_DIM), F32)
    resident = pl.Buffered(1)
    wf, q, k, v = pl.pallas_call(
        functools.partial(_mixin_kernel, q_scale=HEAD_DIM ** -0.5 * math.log2(math.e)),
        out_shape=(jax.ShapeDtypeStruct((batch, DFT_RADIX, nq, 2 * dm), BF16),
                   hm_shape, hm_shape, hm_shape),
        grid=(t // tm,),
        in_specs=[
            pl.BlockSpec((1, DFT_RADIX, tq, d), lambda i: (i // tpb, 0, i % tpb, 0)),
            vec, vec,
            pl.BlockSpec((None, d, 4 * dm), lambda i: (layer, 0, 0), pipeline_mode=resident),
            pl.BlockSpec((None,) + ab.shape[1:], lambda i: (layer, 0, 0, 0),
                         pipeline_mode=resident),
            tab, tab,
        ],
        out_specs=(pl.BlockSpec((1, DFT_RADIX, tq, 2 * dm), lambda i: (i // tpb, 0, i % tpb, 0)),
                   hm, hm, hm),
        compiler_params=_compiler_params(("parallel",)),
        name="mix_in",
    )(x.reshape(batch, DFT_RADIX, nq, d), sh, sc, w_in, ab,
      cos.reshape(DFT_RADIX, nq, HEAD_DIM), sin.reshape(DFT_RADIX, nq, HEAD_DIM))
    return wf, *(a.reshape(batch, heads, seq, HEAD_DIM) for a in (q, k, v))


def _rope_tables(seq):
    half = HEAD_DIM // 2
    inv_freq = ROPE_THETA ** (-jnp.arange(half, dtype=F32) / half)
    ang = jnp.arange(seq, dtype=F32)[:, None] * inv_freq[None, :]
    cos, sin = jnp.cos(ang), jnp.sin(ang)
    return (jnp.concatenate([cos, cos], axis=-1),
            jnp.concatenate([-sin, sin], axis=-1))


def _dft_kernel(ca_ref, sa_ref, cb_ref, sb_ref, wre_ref, wim_ref, o_ref,
                cmat, smat, stage):
    r = pl.program_id(3)
    first = jnp.logical_and(pl.program_id(1) == 0, pl.program_id(2) == 0)

    @pl.when(first)
    def _():
        cb = cb_ref[0]
        sb = sb_ref[0]
        for a in range(ca_ref.shape[2]):
            ca = ca_ref[0, :, a:a + 1]
            sa = sa_ref[0, :, a:a + 1]
            cols = slice(a * LANES, (a + 1) * LANES)
            cmat[r, :, cols] = (ca * cb - sa * sb).astype(BF16)
            smat[r, :, cols] = (sa * cb + ca * sb).astype(BF16)

    y = (jnp.dot(cmat[r], wre_ref[0, 0], preferred_element_type=F32)
         + jnp.dot(smat[r], wim_ref[0, 0], preferred_element_type=F32))
    for cc in range(stage.shape[0]):
        stage[cc, pl.ds(r, y.shape[0], stride=DFT_RADIX), :] = (
            y[:, cc * LANES:(cc + 1) * LANES])

    @pl.when(r == DFT_RADIX - 1)
    def _():
        for cc in range(stage.shape[0]):
            o_ref[0, :, cc * LANES:(cc + 1) * LANES] = stage[cc]


def _dft(w, tables, *, batch, seq, tm, tn):
    dm = w.shape[-1] // 2
    nq = seq // DFT_RADIX
    ncb = dm // tn
    ca, sa, cb, sb = tables
    coarse = pl.BlockSpec((1, tm, ca.shape[2]), lambda i, b, c, r: (r, i, 0))
    fine = pl.BlockSpec((1, tm, LANES), lambda i, b, c, r: (r, i, 0))
    twiddle = pltpu.VMEM((DFT_RADIX, tm, nq), BF16)
    return pl.pallas_call(
        _dft_kernel,
        out_shape=jax.ShapeDtypeStruct((batch, seq, dm), F32),
        grid=(nq // tm, batch, ncb, DFT_RADIX),
        in_specs=[
            coarse, coarse, fine, fine,
            pl.BlockSpec((1, 1, nq, tn), lambda i, b, c, r: (b, r, 0, c)),
            pl.BlockSpec((1, 1, nq, tn), lambda i, b, c, r: (b, r, 0, c + ncb)),
        ],
        out_specs=pl.BlockSpec((1, DFT_RADIX * tm, tn), lambda i, b, c, r: (b, i, c)),
        scratch_shapes=[twiddle, twiddle,
                        pltpu.VMEM((tn // LANES, DFT_RADIX * tm, LANES), F32)],
        compiler_params=_compiler_params(
            ("parallel", "arbitrary", "arbitrary", "arbitrary")),
        name="dft",
    )(ca, sa, cb, sb, w, w)


def _dft_tables(seq):
    nq = seq // DFT_RADIX
    k = (DFT_RADIX * jnp.arange(nq)[None, :] + jnp.arange(DFT_RADIX)[:, None]).reshape(-1)
    coarse = _dft_angles(seq, k, LANES * jnp.arange(nq // LANES))
    fine = _dft_angles(seq, k, jnp.arange(LANES))
    return tuple(f(t).reshape(DFT_RADIX, nq, -1)
                 for t in (coarse, fine) for f in (jnp.cos, jnp.sin))


def _attn_kernel(q_ref, k_ref, v_ref, bias_ref, perm_ref, o_ref,
                 qp, kp, vp, s_buf, p_buf, pv, mb, lb):
    seq, hd = q_ref.shape[2], q_ref.shape[3]
    bq, kw = ATTN_BQ, ATTN_KW
    nblk = seq // bq

    assert kw == 2 * hd
    piece = ATTN_COPY_ROWS
    for src_ref, dst in ((q_ref, qp), (k_ref, kp), (v_ref, vp)):
        for c in range(seq // piece):
            rows = slice(c * piece, (c + 1) * piece)
            dst[0, rows, :] = src_ref[0, 0, rows, :].astype(BF16)
    ch = ATTN_GATHER_ROWS
    for c in range(seq // ch):
        rows = slice(c * ch, (c + 1) * ch)
        for group in ((qp, kp), (vp,)):
            x = jnp.concatenate([g[0, rows, :] for g in group], axis=1)
            y = jnp.dot(perm_ref[...], x, preferred_element_type=F32)
            for gi, g in enumerate(group):
                yc = y[:, gi * hd:(gi + 1) * hd].astype(BF16)
                off = 0
                for bi, d in enumerate(DILATIONS):
                    if d == 1:
                        continue
                    run = ch // d
                    for j in range(d):
                        start = j * (seq // d) + c * run
                        g[bi, start:start + run, :] = yc[off + j * run:off + (j + 1) * run, :]
                    off += ch

    ones = jnp.ones((kw, hd), BF16)

    for bi, d in enumerate(DILATIONS):
        sub = seq // d
        bps = sub // bq

        def place(n, d=d, sub=sub, bps=bps):
            j = n // bps
            t0 = (n % bps) * bq
            ks = jnp.clip(t0 - REACH, 0, sub - kw)
            qrows = pl.ds(pl.multiple_of(j * sub + t0, bq), bq)
            krows = pl.ds(pl.multiple_of(j * sub + ks, REACH), kw)
            if d > 1:
                orows = pl.ds(j + d * t0, bq, stride=d)
            else:
                orows = pl.ds(pl.multiple_of(t0, bq), bq)
            return qrows, krows, orows, (t0 - ks) // REACH

        def scores(n, carry, place=place, bi=bi):
            qrows, krows, _, edge = place(n)
            s = lax.dot_general(qp[bi, qrows, :], kp[bi, krows, :],
                                (((1,), (1,)), ((), ())), preferred_element_type=F32)
            s = s + bias_ref[edge]
            rows = pl.ds(pl.multiple_of(n * bq, bq), bq)
            s_buf[0, rows, :] = s[:, :hd]
            s_buf[1, rows, :] = s[:, hd:]
            return carry

        def softmax(n, carry, place=place, bi=bi):
            _, _, orows, _ = place(n)
            rows = pl.ds(pl.multiple_of(n * bq, bq), bq)
            s = jnp.concatenate([s_buf[0, rows, :], s_buf[1, rows, :]], axis=1)
            m = jnp.max(s, axis=-1, keepdims=True)
            p_buf[n] = jnp.exp2(s - m).astype(BF16)
            mb[bi, orows, :] = jnp.broadcast_to(m, (bq, hd))
            return carry

        def values(n, carry, place=place, bi=bi):
            _, krows, orows, _ = place(n)
            v1 = jnp.concatenate([vp[bi, krows, :], ones], axis=1)
            r = jnp.dot(p_buf[n], v1, preferred_element_type=F32)
            pv[bi, orows, :] = r[:, :hd]
            lb[bi, orows, :] = r[:, hd:]
            return carry

        lax.fori_loop(0, nblk, scores, 0, unroll=ATTN_UNROLL)
        lax.fori_loop(0, nblk, softmax, 0, unroll=ATTN_UNROLL)
        lax.fori_loop(0, nblk, values, 0, unroll=ATTN_UNROLL)

    def merge(c, carry):
        rows = pl.ds(pl.multiple_of(c * bq, bq), bq)
        branches = range(len(DILATIONS))
        ms = [mb[p, rows, :] for p in branches]
        top = functools.reduce(jnp.maximum, ms)
        es = [jnp.exp2(m - top) for m in ms]
        num = sum(es[p] * pv[p, rows, :] for p in branches)
        den = sum(es[p] * lb[p, rows, :] for p in branches)
        o_ref[0, rows, :] = num / den
        return carry

    lax.fori_loop(0, nblk, merge, 0, unroll=2)


def _attn_gather_perm():
    rows = ATTN_GATHER_ROWS
    src_rows = []
    for d in DILATIONS:
        if d > 1:
            r = jnp.arange(rows)
            src_rows.append(d * (r % (rows // d)) + r // (rows // d))
    src_rows = jnp.concatenate(src_rows)
    return (src_rows[:, None] == jnp.arange(rows)[None, :]).astype(BF16)


def _attn_bias():
    r = jnp.arange(ATTN_BQ)[:, None]
    c = jnp.arange(ATTN_KW)[None, :]
    return jnp.stack([
        jnp.where(jnp.abs(edge * REACH + r - c) <= REACH, 0.0, MASK_VALUE).astype(F32)
        for edge in range(3)])


def _attn(q, k, v):
    batch, heads, seq, hd = q.shape
    nb = len(DILATIONS)
    nblk = seq // ATTN_BQ
    bias = _attn_bias()
    perm = _attn_gather_perm()
    blk = pl.BlockSpec((1, 1, seq, hd), lambda b, h: (b, h, 0, 0))
    stat = pltpu.VMEM((nb, seq, hd), F32)
    gathered = pltpu.VMEM((nb, seq, hd), BF16)
    return pl.pallas_call(
        _attn_kernel,
        out_shape=jax.ShapeDtypeStruct((batch, seq, heads * hd), F32),
        grid=(batch, heads),
        in_specs=[blk, blk, blk, pl.BlockSpec(bias.shape, lambda b, h: (0, 0, 0)),
                  pl.BlockSpec(perm.shape, lambda b, h: (0, 0))],
        out_specs=pl.BlockSpec((1, seq, hd), lambda b, h: (b, 0, h)),
        scratch_shapes=[
            gathered, gathered, gathered,
            pltpu.VMEM((ATTN_KW // hd, seq, hd), F32),
            pltpu.VMEM((nblk, ATTN_BQ, ATTN_KW), BF16),
            stat, stat, stat,
        ],
        compiler_params=_compiler_params(("parallel", "parallel"), BIG_VMEM_LIMIT_BYTES),
        name="attn",
    )(q, k, v, bias, perm)


def _mixout_kernel(x_ref, yf_ref, ya_ref, gf_ref, ga_ref, gt_ref, w_ref, o_ref):
    df = yf_ref.shape[1]
    nf = (_rms(yf_ref[...]) * gf_ref[...]).astype(BF16)
    na = (_rms(ya_ref[...]) * ga_ref[...]).astype(BF16)
    out = (jnp.dot(nf, w_ref[:df, :], preferred_element_type=F32)
           + jnp.dot(na, w_ref[df:, :], preferred_element_type=F32))
    o_ref[...] = x_ref[...] + gt_ref[0] * out


def _mixout(x, yf, ya, gf, ga, gt, w_out, *, layer, seq, tm):
    t, d = x.shape
    df, da = yf.shape[1], ya.shape[1]
    tpb = seq // tm
    return pl.pallas_call(
        _mixout_kernel,
        out_shape=jax.ShapeDtypeStruct((t, d), F32),
        grid=(t // tm,),
        in_specs=[
            pl.BlockSpec((tm, d), lambda i: (i, 0)),
            pl.BlockSpec((tm, df), lambda i: (i, 0)),
            pl.BlockSpec((tm, da), lambda i: (i, 0)),
            pl.BlockSpec((1, df), lambda i: (0, 0)),
            pl.BlockSpec((1, da), lambda i: (0, 0)),
            pl.BlockSpec((1, 1, d), lambda i: (i // tpb, 0, 0)),
            pl.BlockSpec((None, df + da, d), lambda i: (layer, 0, 0)),
        ],
        out_specs=pl.BlockSpec((tm, d), lambda i: (i, 0)),
        compiler_params=_compiler_params(("parallel",)),
        name="mix_out",
    )(x, yf, ya, gf, ga, gt, w_out)


class _Tiles(NamedTuple):
    ffn_rows: int
    ffn_cols: int
    mix_rows: int
    dft_rows: int
    dft_cols: int
    ada_cols: int


def _tiles(seq, d, d_ff):
    ffn_cols = 512 if d_ff % 512 == 0 else LANES
    return _Tiles(ffn_rows=min(512, seq), ffn_cols=ffn_cols, mix_rows=min(512, seq),
                  dft_rows=min(512, seq // DFT_RADIX), dft_cols=min(1024, d // 2), ada_cols=min(1024, d))


def kernel(x, c, w_ada, b_ada, w_ffn1_gu, w_ffn1_down, w_mix_in, w_fourier,
           g_fourier_out, g_attn_out, w_mix_out, w_ffn2_gu, w_ffn2_down, g_final):
    batch, seq, d = x.shape
    depth = w_ada.shape[0]
    d_ff = w_ffn1_down.shape[1]
    t = batch * seq
    tl = _tiles(seq, d, d_ff)

    rows = SUBLANES * pl.cdiv(batch, SUBLANES)
    c_pad = jnp.zeros((rows, d), F32).at[:batch].set(c)
    mod = _ada(c_pad, w_ada, b_ada, tn=tl.ada_cols)
    mod = mod[:, :batch].reshape(depth, batch, N_MOD, 1, d)

    groups = w_fourier.shape[1]
    ab = _fprep(w_fourier.reshape(depth * groups, HEAD_DIM, HEAD_DIM), seq)
    ab = ab.reshape(depth, groups, HEAD_DIM, 2 * HEAD_DIM)
    rope_cos, rope_sin = _rope_tables(seq)
    dft_tables = _dft_tables(seq)
    g_fin = g_final.reshape(1, d)

    wgu1, wd1, wgu2, wd2, w_in, w_out = (
        w.astype(BF16) for w in (w_ffn1_gu, w_ffn1_down, w_ffn2_gu, w_ffn2_down,
                                 w_mix_in, w_mix_out))

    xt = x.reshape(t, d)
    for l in range(depth):
        sh1, sc1, g1, sh2, sc2, g2, sh3, sc3, g3 = (mod[l, :, i] for i in range(N_MOD))
        xt = _ffn(xt, sh1, sc1, g1, wgu1, wd1, g_fin, layer=l,
                  seq=seq, tm=tl.ffn_rows, tf=tl.ffn_cols, final_norm=False)
        wf, q, k, v = _mixin(xt, sh2, sc2, w_in, ab, rope_cos, rope_sin, layer=l,
                             batch=batch, seq=seq, tm=tl.mix_rows)
        yf = _dft(wf, dft_tables, batch=batch, seq=seq, tm=tl.dft_rows, tn=tl.dft_cols)
        ya = _attn(q, k, v)
        xt = _mixout(xt, yf.reshape(t, -1), ya.reshape(t, -1),
                     g_fourier_out[l].reshape(1, -1), g_attn_out[l].reshape(1, -1),
                     g2, w_out, layer=l, seq=seq, tm=tl.mix_rows)
        xt = _ffn(xt, sh3, sc3, g3, wgu2, wd2, g_fin, layer=l,
                  seq=seq, tm=tl.ffn_rows, tf=tl.ffn_cols, final_norm=(l == depth - 1))
    return xt.reshape(batch, seq, d)
```

```python
import functools
import math
from typing import NamedTuple

import jax
import jax.numpy as jnp
from jax import lax
from jax.experimental import pallas as pl
from jax.experimental.pallas import tpu as pltpu

F32 = jnp.float32
BF16 = jnp.bfloat16

EPS = 1e-6
HEAD_DIM = 128
ROPE_THETA = 10000.0
DILATED_PATTERNS = ((128, 1), (512, 4), (2048, 16))
DILATIONS = tuple(d for _, d in DILATED_PATTERNS)
REACH = (DILATED_PATTERNS[0][0] // 2) // DILATED_PATTERNS[0][1]
assert all((w // 2) // d == REACH for w, d in DILATED_PATTERNS)
MASK_VALUE = -1e30
N_MOD = 9

LANES = 128
SUBLANES = 8
VMEM_LIMIT_BYTES = 56 * 1024 * 1024
BIG_VMEM_LIMIT_BYTES = 62 * 1024 * 1024

DFT_RADIX = 4

FFN_COL_SPLIT = 2
FFN_RESIDENT_BLOCKS = 4

ATTN_BQ = 128
ATTN_KW = ATTN_BQ + 2 * REACH
ATTN_GATHER_ROWS = 256
ATTN_UNROLL = True


def _compiler_params(semantics, vmem_limit_bytes=VMEM_LIMIT_BYTES):
    return pltpu.CompilerParams(dimension_semantics=semantics,
                                vmem_limit_bytes=vmem_limit_bytes)


def _rms(x):
    return x * lax.rsqrt(jnp.mean(x * x, axis=-1, keepdims=True) + EPS)


def _silu(x):
    return x * jax.nn.sigmoid(x)


def _ada_kernel(c_ref, w_ref, b_ref, o_ref):
    ca = _silu(c_ref[...]).astype(BF16)
    o_ref[0] = jnp.dot(ca, w_ref[0].astype(BF16),
                       preferred_element_type=F32) + b_ref[0]


def _ada(c_pad, w_ada, b_ada, *, tn):
    depth, d, n = w_ada.shape
    rows = c_pad.shape[0]
    return pl.pallas_call(
        _ada_kernel,
        out_shape=jax.ShapeDtypeStruct((depth, rows, n), F32),
        grid=(depth, n // tn),
        in_specs=[
            pl.BlockSpec((rows, d), lambda l, j: (0, 0)),
            pl.BlockSpec((1, d, tn), lambda l, j: (l, 0, j)),
            pl.BlockSpec((1, 1, tn), lambda l, j: (l, 0, j)),
        ],
        out_specs=pl.BlockSpec((1, rows, tn), lambda l, j: (l, 0, j)),
        compiler_params=_compiler_params(("parallel", "parallel")),
        name="ada",
    )(c_pad, w_ada, b_ada.reshape(depth, 1, n))


def _ffn_kernel(x_ref, sh_ref, sc_ref, gt_ref, *refs, layer, final_norm, resident):
    wg_res, wu_res, wd_res = (refs[k * resident:(k + 1) * resident] for k in range(3))
    (wgu_hbm, wd_hbm, gf_ref, o_ref,
     wg_buf, wu_buf, wd_buf, sems, h_ref) = refs[3 * resident:]
    d, tf = wg_res[0].shape
    f = wd_hbm.shape[1]
    nj = f // tf
    part = tf // FFN_COL_SPLIT

    def fetch(j):
        slot = (j - resident) % 2
        return (
            pltpu.make_async_copy(wgu_hbm.at[layer, :, pl.ds(j * tf, tf)],
                                  wg_buf.at[slot], sems.at[0, slot]),
            pltpu.make_async_copy(wgu_hbm.at[layer, :, pl.ds(f + j * tf, tf)],
                                  wu_buf.at[slot], sems.at[1, slot]),
            pltpu.make_async_copy(wd_hbm.at[layer, pl.ds(j * tf, tf), :],
                                  wd_buf.at[slot], sems.at[2, slot]),
        )

    def hidden_block(h, wg, wu, wd, first):
        gu = []
        for s in range(FFN_COL_SPLIT):
            cols = slice(s * part, (s + 1) * part)
            gu.append((jnp.dot(h, wg[:, cols], preferred_element_type=F32),
                       jnp.dot(h, wu[:, cols], preferred_element_type=F32)))
        for s, (g, u) in enumerate(gu):
            a = (_silu(g) * u).astype(BF16)
            r = jnp.dot(a, wd[s * part:(s + 1) * part, :], preferred_element_type=F32)
            if first and s == 0:
                o_ref[...] = r
            else:
                o_ref[...] += r

    def streamed_block(j, slot, prefetch):
        for c in fetch(j):
            c.wait()
        if prefetch:
            for c in fetch(j + 1):
                c.start()
        hidden_block(h_ref[...], wg_buf.at[slot], wu_buf.at[slot], wd_buf.at[slot], False)

    streamed = nj - resident
    if streamed:
        for c in fetch(resident):
            c.start()
    h_ref[...] = (_rms(x_ref[...]) * (1.0 + sc_ref[0]) + sh_ref[0]).astype(BF16)
    for j in range(resident):
        hidden_block(h_ref[...], wg_res[j], wu_res[j], wd_res[j], j == 0)

    tail = min(streamed, 2 + streamed % 2)
    pairs = (streamed - tail) // 2

    def pair(k, carry):
        streamed_block(resident + 2 * k, 0, True)
        streamed_block(resident + 2 * k + 1, 1, True)
        return carry

    if pairs:
        lax.fori_loop(0, pairs, pair, 0)
    for j in range(nj - tail, nj):
        streamed_block(j, (j - resident) % 2, j + 1 < nj)

    y = x_ref[...] + (0.5 * gt_ref[0]) * o_ref[...]
    if final_norm:
        y = _rms(y) * gf_ref[...]
    o_ref[...] = y


def _ffn(x, sh, sc, gt, w_gu, w_down, g_final, *, layer, seq, tm, tf, final_norm):
    t, d = x.shape
    f = w_down.shape[1]
    nj = f // tf
    tpb = seq // tm
    resident = min(FFN_RESIDENT_BLOCKS, nj)
    vec = pl.BlockSpec((1, 1, d), lambda i: (i // tpb, 0, 0))
    once = pl.Buffered(1)
    wg_res = [pl.BlockSpec((None, d, tf), lambda i, j=j: (layer, 0, j), pipeline_mode=once)
              for j in range(resident)]
    wu_res = [pl.BlockSpec((None, d, tf), lambda i, j=j: (layer, 0, nj + j), pipeline_mode=once)
              for j in range(resident)]
    wd_res = [pl.BlockSpec((None, tf, d), lambda i, j=j: (layer, j, 0), pipeline_mode=once)
              for j in range(resident)]
    return pl.pallas_call(
        functools.partial(_ffn_kernel, layer=layer, final_norm=final_norm, resident=resident),
        out_shape=jax.ShapeDtypeStruct((t, d), F32),
        grid=(t // tm,),
        in_specs=[
            pl.BlockSpec((tm, d), lambda i: (i, 0)),
            vec, vec, vec,
            *wg_res, *wu_res, *wd_res,
            pl.BlockSpec(memory_space=pl.ANY),
            pl.BlockSpec(memory_space=pl.ANY),
            pl.BlockSpec((1, d), lambda i: (0, 0)),
        ],
        out_specs=pl.BlockSpec((tm, d), lambda i: (i, 0)),
        scratch_shapes=[
            pltpu.VMEM((2, d, tf), BF16),
            pltpu.VMEM((2, d, tf), BF16),
            pltpu.VMEM((2, tf, d), BF16),
            pltpu.SemaphoreType.DMA((3, 2)),
            pltpu.VMEM((tm, d), BF16),
        ],
        compiler_params=_compiler_params(("parallel",), BIG_VMEM_LIMIT_BYTES),
        name="ffn",
    )(x, sh, sc, gt, *([w_gu] * (2 * resident)), *([w_down] * resident), w_gu, w_down, g_final)


def _fprep_kernel(cc_ref, sc_ref, w_ref, ab_ref, *, scale):
    w = w_ref[0]
    hc = w.shape[0]
    a = jnp.dot(cc_ref[...], w, preferred_element_type=F32,
                precision=lax.Precision.HIGHEST)
    b = jnp.dot(sc_ref[...], w, preferred_element_type=F32,
                precision=lax.Precision.HIGHEST)
    ab_ref[0, :, :hc] = (a * scale).astype(BF16)
    ab_ref[0, :, hc:] = (b * scale).astype(BF16)


def _fprep(w_fourier, seq):
    n, hc, _ = w_fourier.shape
    ang = _dft_angles(hc, jnp.arange(hc), jnp.arange(hc))
    scale = 1.0 / math.sqrt(seq * hc)
    sq = pl.BlockSpec((hc, hc), lambda g: (0, 0))
    return pl.pallas_call(
        functools.partial(_fprep_kernel, scale=scale),
        out_shape=jax.ShapeDtypeStruct((n, hc, 2 * hc), BF16),
        grid=(n,),
        in_specs=[sq, sq, pl.BlockSpec((1, hc, hc), lambda g: (g, 0, 0))],
        out_specs=pl.BlockSpec((1, hc, 2 * hc), lambda g: (g, 0, 0)),
        compiler_params=_compiler_params(("parallel",)),
        name="fprep",
    )(jnp.cos(ang), jnp.sin(ang), w_fourier)


def _dft_angles(n, rows, cols):
    idx = (rows[:, None] * cols[None, :]) % n
    return idx.astype(F32) * (2.0 * math.pi / n)


def _mixin_kernel(x_ref, sh_ref, sc_ref, w_ref, ab_ref, cos_ref, sin_ref,
                  wf_ref, q_ref, k_ref, v_ref, *, q_scale):
    hc = HEAD_DIM
    dm = w_ref.shape[1] // 4
    groups = heads = dm // hc
    nqt, tq = x_ref.shape[1], x_ref.shape[2]
    tm = nqt * tq
    x = x_ref[0].reshape(tm, x_ref.shape[3])
    h = (_rms(x) * (1.0 + sc_ref[0]) + sh_ref[0]).astype(BF16)

    def project(n):
        return jnp.dot(h, w_ref[:, n * dm:(n + 1) * dm], preferred_element_type=F32)

    def rope_to(dst_ref, p, scale):
        cos = cos_ref[...].reshape(tm, hc) * scale
        sin = sin_ref[...].reshape(tm, hc) * scale
        for hh in range(heads):
            t = p[:, hh * hc:(hh + 1) * hc]
            y = t * cos + pltpu.roll(t, hc // 2, 1) * sin
            dst_ref[0, hh] = y.astype(BF16).reshape(nqt, tq, hc)

    u = project(0).astype(BF16)
    pq = project(1)
    for g in range(groups):
        r = jnp.dot(u[:, g * hc:(g + 1) * hc], ab_ref[g], preferred_element_type=F32)
        a = [r[q * tq:(q + 1) * tq, :hc] for q in range(nqt)]
        p = [r[q * tq:(q + 1) * tq, hc:] for q in range(nqt)]
        t0r, t0i = a[0] + a[2], -(p[0] + p[2])
        t1r, t1i = a[0] - a[2], p[2] - p[0]
        t2r, t2i = a[1] + a[3], -(p[1] + p[3])
        t3r, t3i = a[1] - a[3], p[3] - p[1]
        parts = ((t0r + t2r, t0i + t2i),
                 (t1r + t3i, t1i - t3r),
                 (t0r - t2r, t0i - t2i),
                 (t1r - t3i, t1i + t3r))
        for res, (re, im) in enumerate(parts):
            wf_ref[0, res, :, g * hc:(g + 1) * hc] = re.astype(BF16)
            wf_ref[0, res, :, dm + g * hc:dm + (g + 1) * hc] = im.astype(BF16)
    rope_to(q_ref, pq, q_scale)
    rope_to(k_ref, project(2), 1.0)
    pv = project(3)
    for hh in range(heads):
        v_ref[0, hh] = pv[:, hh * hc:(hh + 1) * hc].astype(BF16).reshape(nqt, tq, hc)


def _mixin(x, sh, sc, w_in, ab, cos, sin, *, layer, batch, seq, tm):
    t, d = x.shape
    dm = w_in.shape[2] // 4
    heads = dm // HEAD_DIM
    nq, tq = seq // DFT_RADIX, tm // DFT_RADIX
    tpb = nq // tq
    vec = pl.BlockSpec((1, 1, d), lambda i: (i // tpb, 0, 0))
    tab = pl.BlockSpec((DFT_RADIX, tq, HEAD_DIM), lambda i: (0, i % tpb, 0))
    hm = pl.BlockSpec((1, heads, DFT_RADIX, tq, HEAD_DIM),
                      lambda i: (i // tpb, 0, 0, i % tpb, 0))
    hm_shape = jax.ShapeDtypeStruct((batch, heads, DFT_RADIX, nq, HEAD_DIM), BF16)
    resident = pl.Buffered(1)
    wf, q, k, v = pl.pallas_call(
        functools.partial(_mixin_kernel, q_scale=HEAD_DIM ** -0.5 * math.log2(math.e)),
        out_shape=(jax.ShapeDtypeStruct((batch, DFT_RADIX, nq, 2 * dm), BF16),
                   hm_shape, hm_shape, hm_shape),
        grid=(t // tm,),
        in_specs=[
            pl.BlockSpec((1, DFT_RADIX, tq, d), lambda i: (i // tpb, 0, i % tpb, 0)),
            vec, vec,
            pl.BlockSpec((None, d, 4 * dm), lambda i: (layer, 0, 0), pipeline_mode=resident),
            pl.BlockSpec((None,) + ab.shape[1:], lambda i: (layer, 0, 0, 0),
                         pipeline_mode=resident),
            tab, tab,
        ],
        out_specs=(pl.BlockSpec((1, DFT_RADIX, tq, 2 * dm), lambda i: (i // tpb, 0, i % tpb, 0)),
                   hm, hm, hm),
        compiler_params=_compiler_params(("parallel",)),
        name="mix_in",
    )(x.reshape(batch, DFT_RADIX, nq, d), sh, sc, w_in, ab,
      cos.reshape(DFT_RADIX, nq, HEAD_DIM), sin.reshape(DFT_RADIX, nq, HEAD_DIM))
    return wf, *(a.reshape(batch, heads, seq, HEAD_DIM) for a in (q, k, v))


def _rope_tables(seq):
    half = HEAD_DIM // 2
    inv_freq = ROPE_THETA ** (-jnp.arange(half, dtype=F32) / half)
    ang = jnp.arange(seq, dtype=F32)[:, None] * inv_freq[None, :]
    cos, sin = jnp.cos(ang), jnp.sin(ang)
    return (jnp.concatenate([cos, cos], axis=-1),
            jnp.concatenate([-sin, sin], axis=-1))


def _dft_kernel(ca_ref, sa_ref, cb_ref, sb_ref, wre_ref, wim_ref, o_ref,
                cmat, smat, stage):
    r = pl.program_id(3)
    first = jnp.logical_and(pl.program_id(1) == 0, pl.program_id(2) == 0)

    @pl.when(first)
    def _():
        cb = cb_ref[0]
        sb = sb_ref[0]
        for a in range(ca_ref.shape[2]):
            ca = ca_ref[0, :, a:a + 1]
            sa = sa_ref[0, :, a:a + 1]
            cols = slice(a * LANES, (a + 1) * LANES)
            cmat[r, :, cols] = (ca * cb - sa * sb).astype(BF16)
            smat[r, :, cols] = (sa * cb + ca * sb).astype(BF16)

    y = (jnp.dot(cmat[r], wre_ref[0, 0], preferred_element_type=F32)
         + jnp.dot(smat[r], wim_ref[0, 0], preferred_element_type=F32))
    for cc in range(stage.shape[0]):
        stage[cc, pl.ds(r, y.shape[0], stride=DFT_RADIX), :] = (
            y[:, cc * LANES:(cc + 1) * LANES])

    @pl.when(r == DFT_RADIX - 1)
    def _():
        for cc in range(stage.shape[0]):
            o_ref[0, :, cc * LANES:(cc + 1) * LANES] = stage[cc]


def _dft(w, tables, *, batch, seq, tm, tn):
    dm = w.shape[-1] // 2
    nq = seq // DFT_RADIX
    ncb = dm // tn
    ca, sa, cb, sb = tables
    coarse = pl.BlockSpec((1, tm, ca.shape[2]), lambda i, b, c, r: (r, i, 0))
    fine = pl.BlockSpec((1, tm, LANES), lambda i, b, c, r: (r, i, 0))
    twiddle = pltpu.VMEM((DFT_RADIX, tm, nq), BF16)
    return pl.pallas_call(
        _dft_kernel,
        out_shape=jax.ShapeDtypeStruct((batch, seq, dm), F32),
        grid=(nq // tm, batch, ncb, DFT_RADIX),
        in_specs=[
            coarse, coarse, fine, fine,
            pl.BlockSpec((1, 1, nq, tn), lambda i, b, c, r: (b, r, 0, c)),
            pl.BlockSpec((1, 1, nq, tn), lambda i, b, c, r: (b, r, 0, c + ncb)),
        ],
        out_specs=pl.BlockSpec((1, DFT_RADIX * tm, tn), lambda i, b, c, r: (b, i, c)),
        scratch_shapes=[twiddle, twiddle,
                        pltpu.VMEM((tn // LANES, DFT_RADIX * tm, LANES), F32)],
        compiler_params=_compiler_params(
            ("parallel", "arbitrary", "arbitrary", "arbitrary")),
        name="dft",
    )(ca, sa, cb, sb, w, w)


def _dft_tables(seq):
    nq = seq // DFT_RADIX
    k = (DFT_RADIX * jnp.arange(nq)[None, :] + jnp.arange(DFT_RADIX)[:, None]).reshape(-1)
    coarse = _dft_angles(seq, k, LANES * jnp.arange(nq // LANES))
    fine = _dft_angles(seq, k, jnp.arange(LANES))
    return tuple(f(t).reshape(DFT_RADIX, nq, -1)
                 for t in (coarse, fine) for f in (jnp.cos, jnp.sin))


def _attn_kernel(q_ref, k_ref, v_ref, bias_ref, perm_ref, o_ref,
                 qp, kp, vp, s_buf, p_buf, pv, mb, lb):
    seq, hd = q_ref.shape[2], q_ref.shape[3]
    bq, kw = ATTN_BQ, ATTN_KW
    nblk = seq // bq

    assert kw == 2 * hd
    ch = ATTN_GATHER_ROWS
    for c in range(seq // ch):
        rows = slice(c * ch, (c + 1) * ch)
        for group in (((q_ref, qp), (k_ref, kp)), ((v_ref, vp),)):
            x = jnp.concatenate([src_ref[0, 0, rows, :] for src_ref, _ in group], axis=1)
            y = jnp.dot(perm_ref[...], x, preferred_element_type=F32)
            for gi, (_, g) in enumerate(group):
                yc = y[:, gi * hd:(gi + 1) * hd].astype(BF16)
                off = 0
                for bi, d in enumerate(DILATIONS):
                    if d == 1:
                        continue
                    run = ch // d
                    for j in range(d):
                        start = j * (seq // d) + c * run
                        g[bi, start:start + run, :] = yc[off + j * run:off + (j + 1) * run, :]
                    off += ch

    def rows_of(src_ref, gathered, bi, rows):
        return src_ref[0, 0, rows, :] if DILATIONS[bi] == 1 else gathered[bi, rows, :]

    ones = jnp.ones((kw, hd), BF16)

    for bi, d in enumerate(DILATIONS):
        sub = seq // d
        bps = sub // bq

        def place(n, d=d, sub=sub, bps=bps):
            j = n // bps
            t0 = (n % bps) * bq
            ks = jnp.clip(t0 - REACH, 0, sub - kw)
            qrows = pl.ds(pl.multiple_of(j * sub + t0, bq), bq)
            krows = pl.ds(pl.multiple_of(j * sub + ks, REACH), kw)
            if d > 1:
                orows = pl.ds(j + d * t0, bq, stride=d)
            else:
                orows = pl.ds(pl.multiple_of(t0, bq), bq)
            return qrows, krows, orows, (t0 - ks) // REACH

        def scores(n, carry, place=place, bi=bi):
            qrows, krows, _, edge = place(n)
            s = lax.dot_general(rows_of(q_ref, qp, bi, qrows), rows_of(k_ref, kp, bi, krows),
                                (((1,), (1,)), ((), ())), preferred_element_type=F32)
            s = s + bias_ref[edge]
            rows = pl.ds(pl.multiple_of(n * bq, bq), bq)
            s_buf[0, rows, :] = s[:, :hd]
            s_buf[1, rows, :] = s[:, hd:]
            return carry

        def softmax(n, carry, place=place, bi=bi):
            _, _, orows, _ = place(n)
            rows = pl.ds(pl.multiple_of(n * bq, bq), bq)
            s = jnp.concatenate([s_buf[0, rows, :], s_buf[1, rows, :]], axis=1)
            m = jnp.max(s, axis=-1, keepdims=True)
            p_buf[n] = jnp.exp2(s - m).astype(BF16)
            mb[bi, orows, :] = jnp.broadcast_to(m, (bq, hd))
            return carry

        def values(n, carry, place=place, bi=bi):
            _, krows, orows, _ = place(n)
            v1 = jnp.concatenate([rows_of(v_ref, vp, bi, krows), ones], axis=1)
            r = jnp.dot(p_buf[n], v1, preferred_element_type=F32)
            pv[bi, orows, :] = r[:, :hd]
            lb[bi, orows, :] = r[:, hd:]
            return carry

        lax.fori_loop(0, nblk, scores, 0, unroll=ATTN_UNROLL)
        lax.fori_loop(0, nblk, softmax, 0, unroll=ATTN_UNROLL)
        lax.fori_loop(0, nblk, values, 0, unroll=ATTN_UNROLL)

    def merge(c, carry):
        rows = pl.ds(pl.multiple_of(c * bq, bq), bq)
        branches = range(len(DILATIONS))
        ms = [mb[p, rows, :] for p in branches]
        top = functools.reduce(jnp.maximum, ms)
        es = [jnp.exp2(m - top) for m in ms]
        num = sum(es[p] * pv[p, rows, :] for p in branches)
        den = sum(es[p] * lb[p, rows, :] for p in branches)
        o_ref[0, rows, :] = num / den
        return carry

    lax.fori_loop(0, nblk, merge, 0, unroll=2)


def _attn_gather_perm():
    rows = ATTN_GATHER_ROWS
    src_rows = []
    for d in DILATIONS:
        if d > 1:
            r = jnp.arange(rows)
            src_rows.append(d * (r % (rows // d)) + r // (rows // d))
    src_rows = jnp.concatenate(src_rows)
    return (src_rows[:, None] == jnp.arange(rows)[None, :]).astype(BF16)


def _attn_bias():
    r = jnp.arange(ATTN_BQ)[:, None]
    c = jnp.arange(ATTN_KW)[None, :]
    return jnp.stack([
        jnp.where(jnp.abs(edge * REACH + r - c) <= REACH, 0.0, MASK_VALUE).astype(F32)
        for edge in range(3)])


def _attn(q, k, v):
    batch, heads, seq, hd = q.shape
    nb = len(DILATIONS)
    nblk = seq // ATTN_BQ
    bias = _attn_bias()
    perm = _attn_gather_perm()
    blk = pl.BlockSpec((1, 1, seq, hd), lambda b, h: (b, h, 0, 0))
    stat = pltpu.VMEM((nb, seq, hd), F32)
    gathered = pltpu.VMEM((nb, seq, hd), BF16)
    return pl.pallas_call(
        _attn_kernel,
        out_shape=jax.ShapeDtypeStruct((batch, seq, heads * hd), F32),
        grid=(batch, heads),
        in_specs=[blk, blk, blk, pl.BlockSpec(bias.shape, lambda b, h: (0, 0, 0)),
                  pl.BlockSpec(perm.shape, lambda b, h: (0, 0))],
        out_specs=pl.BlockSpec((1, seq, hd), lambda b, h: (b, 0, h)),
        scratch_shapes=[
            gathered, gathered, gathered,
            pltpu.VMEM((ATTN_KW // hd, seq, hd), F32),
            pltpu.VMEM((nblk, ATTN_BQ, ATTN_KW), BF16),
            stat, stat, stat,
        ],
        compiler_params=_compiler_params(("parallel", "parallel"), BIG_VMEM_LIMIT_BYTES),
        name="attn",
    )(q, k, v, bias, perm)


def _mixout_kernel(x_ref, yf_ref, ya_ref, gf_ref, ga_ref, gt_ref, w_ref, o_ref):
    df = yf_ref.shape[1]
    nf = (_rms(yf_ref[...]) * gf_ref[...]).astype(BF16)
    na = (_rms(ya_ref[...]) * ga_ref[...]).astype(BF16)
    out = (jnp.dot(nf, w_ref[:df, :], preferred_element_type=F32)
           + jnp.dot(na, w_ref[df:, :], preferred_element_type=F32))
    o_ref[...] = x_ref[...] + gt_ref[0] * out


def _mixout(x, yf, ya, gf, ga, gt, w_out, *, layer, seq, tm):
    t, d = x.shape
    df, da = yf.shape[1], ya.shape[1]
    tpb = seq // tm
    return pl.pallas_call(
        _mixout_kernel,
        out_shape=jax.ShapeDtypeStruct((t, d), F32),
        grid=(t // tm,),
        in_specs=[
            pl.BlockSpec((tm, d), lambda i: (i, 0)),
            pl.BlockSpec((tm, df), lambda i: (i, 0)),
            pl.BlockSpec((tm, da), lambda i: (i, 0)),
            pl.BlockSpec((1, df), lambda i: (0, 0)),
            pl.BlockSpec((1, da), lambda i: (0, 0)),
            pl.BlockSpec((1, 1, d), lambda i: (i // tpb, 0, 0)),
            pl.BlockSpec((None, df + da, d), lambda i: (layer, 0, 0)),
        ],
        out_specs=pl.BlockSpec((tm, d), lambda i: (i, 0)),
        compiler_params=_compiler_params(("parallel",)),
        name="mix_out",
    )(x, yf, ya, gf, ga, gt, w_out)


class _Tiles(NamedTuple):
    ffn_rows: int
    ffn_cols: int
    mix_rows: int
    dft_rows: int
    dft_cols: int
    ada_cols: int


def _tiles(seq, d, d_ff):
    ffn_cols = 512 if d_ff % 512 == 0 else LANES
    return _Tiles(ffn_rows=min(512, seq), ffn_cols=ffn_cols, mix_rows=min(512, seq),
                  dft_rows=min(512, seq // DFT_RADIX), dft_cols=min(1024, d // 2), ada_cols=min(1024, d))


def kernel(x, c, w_ada, b_ada, w_ffn1_gu, w_ffn1_down, w_mix_in, w_fourier,
           g_fourier_out, g_attn_out, w_mix_out, w_ffn2_gu, w_ffn2_down, g_final):
    batch, seq, d = x.shape
    depth = w_ada.shape[0]
    d_ff = w_ffn1_down.shape[1]
    t = batch * seq
    tl = _tiles(seq, d, d_ff)

    rows = SUBLANES * pl.cdiv(batch, SUBLANES)
    c_pad = jnp.zeros((rows, d), F32).at[:batch].set(c)
    mod = _ada(c_pad, w_ada, b_ada, tn=tl.ada_cols)
    mod = mod[:, :batch].reshape(depth, batch, N_MOD, 1, d)

    groups = w_fourier.shape[1]
    ab = _fprep(w_fourier.reshape(depth * groups, HEAD_DIM, HEAD_DIM), seq)
    ab = ab.reshape(depth, groups, HEAD_DIM, 2 * HEAD_DIM)
    rope_cos, rope_sin = _rope_tables(seq)
    dft_tables = _dft_tables(seq)
    g_fin = g_final.reshape(1, d)

    wgu1, wd1, wgu2, wd2, w_in, w_out = (
        w.astype(BF16) for w in (w_ffn1_gu, w_ffn1_down, w_ffn2_gu, w_ffn2_down,
                                 w_mix_in, w_mix_out))

    xt = x.reshape(t, d)
    for l in range(depth):
        sh1, sc1, g1, sh2, sc2, g2, sh3, sc3, g3 = (mod[l, :, i] for i in range(N_MOD))
        xt = _ffn(xt, sh1, sc1, g1, wgu1, wd1, g_fin, layer=l,
                  seq=seq, tm=tl.ffn_rows, tf=tl.ffn_cols, final_norm=False)
        wf, q, k, v = _mixin(xt, sh2, sc2, w_in, ab, rope_cos, rope_sin, layer=l,
                             batch=batch, seq=seq, tm=tl.mix_rows)
        yf = _dft(wf, dft_tables, batch=batch, seq=seq, tm=tl.dft_rows, tn=tl.dft_cols)
        ya = _attn(q, k, v)
        xt = _mixout(xt, yf.reshape(t, -1), ya.reshape(t, -1),
                     g_fourier_out[l].reshape(1, -1), g_attn_out[l].reshape(1, -1),
                     g2, w_out, layer=l, seq=seq, tm=tl.mix_rows)
        xt = _ffn(xt, sh3, sc3, g3, wgu2, wd2, g_fin, layer=l,
                  seq=seq, tm=tl.ffn_rows, tf=tl.ffn_cols, final_norm=(l == depth - 1))
    return xt.reshape(batch, seq, d)
```
